```python
import math
import jax, jax.numpy as jnp
from jax import lax
import numpy as np

D_MODEL = 1024
BATCH = 8
SEQ = 2048
DEPTH = 2
DEC_BATCH = 32
DEC_SEQ = 32
PAST_LEN = 1024

CHUNK = 64
LEFT_CHUNKS = 8
BAND = (LEFT_CHUNKS + 1) * CHUNK
Q_BLOCK = 128
HEAD_DIM = 64
A_HEADS = D_MODEL // (2 * HEAD_DIM)
B_HEADS = D_MODEL // HEAD_DIM
D_FF = 4 * D_MODEL
REL_CLIP = 128
ROPE_THETA = 10000.0
EPS = 1e-6
NEG_INF = -1e30
N_A = DEPTH // 2
N_B = DEPTH - N_A

kernel_name = "yoco_diffattn_chunkband_stream_step"


def rmsnorm(x, g):
    xf = x.astype(jnp.float32)
    y = xf * lax.rsqrt(jnp.mean(xf * xf, axis=-1, keepdims=True) + EPS)
    return (y * g.astype(jnp.float32)).astype(x.dtype)


def rope(x, pos):
    half = HEAD_DIM // 2
    inv = 1.0 / (ROPE_THETA ** (jnp.arange(half, dtype=jnp.float32) / half))
    ang = pos.astype(jnp.float32)[:, None] * inv[None, :]
    cos = jnp.cos(ang)[:, None, None, :]
    sin = jnp.sin(ang)[:, None, None, :]
    xf = x.astype(jnp.float32)
    x1, x2 = xf[..., :half], xf[..., half:]
    return jnp.concatenate([x1 * cos - x2 * sin, x2 * cos + x1 * sin], axis=-1).astype(x.dtype)


def lambda_init(layer):
    return 0.8 - 0.6 * math.exp(-0.3 * layer)


def diff_lambda(lp, lam0):
    lp = lp.astype(jnp.float32)
    return jnp.exp(jnp.sum(lp[0] * lp[1])) - jnp.exp(jnp.sum(lp[2] * lp[3])) + lam0


def diff_weights(q, k, lam, mask):
    s = jnp.einsum("bqhcd,bkhcd->bhcqk", q, k).astype(jnp.float32) * (HEAD_DIM ** -0.5)
    if mask is not None:
        s = jnp.where(mask, s, NEG_INF)
    p = jax.nn.softmax(s, axis=-1)
    return p[:, :, 0] - lam * p[:, :, 1]


def diff_attn_prompt(q, k, v, lam):
    nb, ns = q.shape[0], q.shape[1]
    nblk = ns // Q_BLOCK
    qb = q.reshape(nb, nblk, Q_BLOCK, A_HEADS, 2, HEAD_DIM).transpose(1, 0, 2, 3, 4, 5)
    k_chunk = jnp.arange(ns) // CHUNK

    def one(args):
        qi, bi = args
        q_chunk = (bi * Q_BLOCK + jnp.arange(Q_BLOCK)) // CHUNK
        mask = q_chunk[:, None] >= k_chunk[None, :]
        p = diff_weights(qi, k, lam, mask)
        return jnp.einsum("bhqk,bkhe->bqhe", p.astype(v.dtype), v)

    o = lax.map(one, (qb, jnp.arange(nblk)))
    return o.transpose(1, 0, 2, 3, 4).reshape(nb, ns, A_HEADS, 2 * HEAD_DIM)


def diff_attn_full(q, k, v, lam):
    p = diff_weights(q, k, lam, None)
    return jnp.einsum("bhqk,bkhe->bqhe", p.astype(v.dtype), v)


def rel_bias(table, rel):
    idx = jnp.clip(rel, -REL_CLIP, REL_CLIP) + REL_CLIP
    return table.astype(jnp.float32)[:, idx]


def band_attn_prompt(q, k, v, table):
    nb, ns = q.shape[0], q.shape[1]
    nc = ns // CHUNK
    pad = LEFT_CHUNKS * CHUNK
    kp = jnp.pad(k, ((0, 0), (pad, 0), (0, 0), (0, 0)))
    vp = jnp.pad(v, ((0, 0), (pad, 0), (0, 0), (0, 0)))
    qc = q.reshape(nb, nc, CHUNK, B_HEADS, HEAD_DIM).transpose(1, 0, 2, 3, 4)
    rel = pad + jnp.arange(CHUNK)[:, None] - jnp.arange(BAND)[None, :]
    bias = rel_bias(table, rel)[None]

    def one(args):
        qi, ci = args
        kb = lax.dynamic_slice_in_dim(kp, ci * CHUNK, BAND, axis=1)
        vb = lax.dynamic_slice_in_dim(vp, ci * CHUNK, BAND, axis=1)
        valid = jnp.arange(BAND) >= (LEFT_CHUNKS - ci) * CHUNK
        s = jnp.einsum("bqhd,bkhd->bhqk", qi, kb).astype(jnp.float32) * (HEAD_DIM ** -0.5) + bias
        p = jax.nn.softmax(jnp.where(valid, s, NEG_INF), axis=-1)
        return jnp.einsum("bhqk,bkhd->bqhd", p.astype(vb.dtype), vb)

    o = lax.map(one, (qc, jnp.arange(nc)))
    return o.transpose(1, 0, 2, 3, 4).reshape(nb, ns, B_HEADS * HEAD_DIM)


def band_attn_sample(q, k_new, v_new, cache_k, cache_v, table, past):
    nb, nt = q.shape[0], q.shape[1]
    lb = cache_k.shape[1]
    k_all = jnp.concatenate([cache_k, k_new], axis=1)
    v_all = jnp.concatenate([cache_v, v_new], axis=1)
    qpos = jnp.arange(past, past + nt)
    kpos = jnp.arange(past - lb, past + nt)
    bias = rel_bias(table, qpos[:, None] - kpos[None, :])[None]
    s = jnp.einsum("bqhd,bkhd->bhqk", q, k_all).astype(jnp.float32) * (HEAD_DIM ** -0.5) + bias
    p = jax.nn.softmax(s, axis=-1)
    return jnp.einsum("bhqk,bkhd->bqhd", p.astype(v_all.dtype), v_all).reshape(nb, nt, B_HEADS * HEAD_DIM)


def sq_relu_mlp(h, w1, w2):
    return jnp.square(jax.nn.relu(h @ w1)) @ w2


def _trunk(x, pos, cache_a_k, cache_a_v, cache_b_k, cache_b_v,
           g_attn, w_a_qkv, a_lambda, a_subln, w_a_o, g_kv, w_kv, w_b_q, b_rel, w_b_o,
           g_mlp, w_ff1, w_ff2, g_final):
    prompt = cache_a_k is None
    nb, ns = x.shape[0], x.shape[1]
    h = x
    new_ak, new_av = [], []
    k_sh = v_sh = None
    for l in range(DEPTH):
        if l < N_A:
            hn = rmsnorm(h, g_attn[l])
            q, k, v = jnp.split(hn @ w_a_qkv[l], 3, axis=-1)
            q = rope(q.reshape(nb, ns, A_HEADS, 2, HEAD_DIM), pos)
            k = rope(k.reshape(nb, ns, A_HEADS, 2, HEAD_DIM), pos)
            v = v.reshape(nb, ns, A_HEADS, 2 * HEAD_DIM)
            new_ak.append(k.reshape(nb, ns, A_HEADS, 2 * HEAD_DIM))
            new_av.append(v)
            lam0 = lambda_init(l)
            lam = diff_lambda(a_lambda[l], lam0)
            if prompt:
                o = diff_attn_prompt(q, k, v, lam)
            else:
                past = cache_a_k.shape[2]
                k_all = jnp.concatenate(
                    [cache_a_k[l].reshape(nb, past, A_HEADS, 2, HEAD_DIM), k], axis=1)
                v_all = jnp.concatenate([cache_a_v[l], v], axis=1)
                o = diff_attn_full(q, k_all, v_all, lam)
            o = rmsnorm(o, a_subln[l]) * (1.0 - lam0)
            h = h + o.reshape(nb, ns, D_MODEL) @ w_a_o[l]
        else:
            j = l - N_A
            if j == 0:
                kv = rmsnorm(h, g_kv) @ w_kv
                k_sh, v_sh = jnp.split(kv, 2, axis=-1)
                k_sh = k_sh.reshape(nb, ns, B_HEADS, HEAD_DIM)
                v_sh = v_sh.reshape(nb, ns, B_HEADS, HEAD_DIM)
            q = (rmsnorm(h, g_attn[l]) @ w_b_q[j]).reshape(nb, ns, B_HEADS, HEAD_DIM)
            if prompt:
                o = band_attn_prompt(q, k_sh, v_sh, b_rel[j])
            else:
                o = band_attn_sample(q, k_sh, v_sh, cache_b_k, cache_b_v, b_rel[j], cache_a_k.shape[2])
            h = h + o @ w_b_o[j]
        h = h + sq_relu_mlp(rmsnorm(h, g_mlp[l]), w_ff1[l], w_ff2[l])
    y = rmsnorm(h, g_final)
    if prompt:
        keep = min(LEFT_CHUNKS * CHUNK, ns)
        k_sh, v_sh = k_sh[:, ns - keep:], v_sh[:, ns - keep:]
    return y, jnp.stack(new_ak), jnp.stack(new_av), k_sh, v_sh


def setup_inputs(seed: int = 0) -> dict:
    key = jax.random.key(seed)
    ks = jax.random.split(key, 20)
    f32 = jnp.float32
    lb = min(LEFT_CHUNKS * CHUNK, PAST_LEN)

    def nrm(k, shape, scale):
        return jax.random.normal(k, shape, f32) * scale

    return {
        "x_prompt": nrm(ks[0], (BATCH, SEQ, D_MODEL), 1.0),
        "x_sample": nrm(ks[1], (DEC_BATCH, DEC_SEQ, D_MODEL), 1.0),
        "cache_a_k": nrm(ks[2], (N_A, DEC_BATCH, PAST_LEN, A_HEADS, 2 * HEAD_DIM), 1.0),
        "cache_a_v": nrm(ks[3], (N_A, DEC_BATCH, PAST_LEN, A_HEADS, 2 * HEAD_DIM), 1.0),
        "cache_b_k": nrm(ks[4], (DEC_BATCH, lb, B_HEADS, HEAD_DIM), 1.0),
        "cache_b_v": nrm(ks[5], (DEC_BATCH, lb, B_HEADS, HEAD_DIM), 1.0),
        "g_attn": 1.0 + nrm(ks[6], (DEPTH, D_MODEL), 0.02),
        "w_a_qkv": nrm(ks[7], (N_A, D_MODEL, 3 * D_MODEL), D_MODEL ** -0.5),
        "a_lambda": nrm(ks[8], (N_A, 4, HEAD_DIM), 0.1),
        "a_subln": 1.0 + nrm(ks[9], (N_A, 2 * HEAD_DIM), 0.02),
        "w_a_o": nrm(ks[10], (N_A, D_MODEL, D_MODEL), D_MODEL ** -0.5),
        "g_kv": 1.0 + nrm(ks[11], (D_MODEL,), 0.02),
        "w_kv": nrm(ks[12], (D_MODEL, 2 * D_MODEL), D_MODEL ** -0.5),
        "w_b_q": nrm(ks[13], (N_B, D_MODEL, D_MODEL), D_MODEL ** -0.5),
        "b_rel": nrm(ks[14], (N_B, B_HEADS, 2 * REL_CLIP + 1), 0.5),
        "w_b_o": nrm(ks[15], (N_B, D_MODEL, D_MODEL), D_MODEL ** -0.5),
        "g_mlp": 1.0 + nrm(ks[16], (DEPTH, D_MODEL), 0.02),
        "w_ff1": nrm(ks[17], (DEPTH, D_MODEL, D_FF), D_MODEL ** -0.5),
        "w_ff2": nrm(ks[18], (DEPTH, D_FF, D_MODEL), 0.5 * D_FF ** -0.5),
        "g_final": 1.0 + nrm(ks[19], (D_MODEL,), 0.02),
    }


def reference(x_prompt, x_sample, cache_a_k, cache_a_v, cache_b_k, cache_b_v,
              g_attn, w_a_qkv, a_lambda, a_subln, w_a_o, g_kv, w_kv, w_b_q, b_rel, w_b_o,
              g_mlp, w_ff1, w_ff2, g_final):
    seq = x_prompt.shape[1]
    past = cache_a_k.shape[2]
    nt = x_sample.shape[1]
    y_prompt, ak_p, av_p, bk_p, bv_p = _trunk(
        x_prompt, jnp.arange(seq), None, None, None, None,
        g_attn, w_a_qkv, a_lambda, a_subln, w_a_o, g_kv, w_kv, w_b_q, b_rel, w_b_o,
        g_mlp, w_ff1, w_ff2, g_final)
    y_sample, ak_s, av_s, bk_s, bv_s = _trunk(
        x_sample, jnp.arange(past, past + nt), cache_a_k, cache_a_v, cache_b_k, cache_b_v,
        g_attn, w_a_qkv, a_lambda, a_subln, w_a_o, g_kv, w_kv, w_b_q, b_rel, w_b_o,
        g_mlp, w_ff1, w_ff2, g_final)
    return (y_prompt, y_sample, ak_p, av_p, bk_p, bv_p, ak_s, av_s, bk_s, bv_s)
```

```python
import functools
import math

import jax
import jax.numpy as jnp
from jax import lax
from jax.experimental import pallas as pl
from jax.experimental.pallas import tpu as pltpu

HEAD_DIM = 64
CHUNK = 64
LEFT_CHUNKS = 8
PAD = LEFT_CHUNKS * CHUNK
REL_CLIP = 128
ROPE_THETA = 10000.0
EPS = 1e-6
NEG_INF = -1e30
LANES = 128
SCALE = HEAD_DIM ** -0.5

TOKEN_TILE = 512
FF_CHUNK = 1024
ATTN_Q_TILE = 256
BAND_Q_TILE = 128
VMEM_LIMIT = 56 * 1024 * 1024

F32 = jnp.float32
BF16 = jnp.bfloat16
NT_DIMS = (((1,), (1,)), ((), ()))


def _rms(x, g):
    ms = jnp.mean(x * x, axis=-1, keepdims=True)
    return x * lax.rsqrt(ms + EPS) * g


def _resident(shape):
    return pl.BlockSpec(shape, lambda *_: (0,) * len(shape), pipeline_mode=pl.Buffered(1))


def _params(*sem):
    return pltpu.CompilerParams(dimension_semantics=sem, vmem_limit_bytes=VMEM_LIMIT)


def _diff_lambda(lp, lam0):
    a = jnp.sum(lp[0:1] * lp[1:2], axis=-1, keepdims=True)
    b = jnp.sum(lp[2:3] * lp[3:4], axis=-1, keepdims=True)
    return jnp.exp(a) - jnp.exp(b) + lam0


def _head_norm(o, g, lam0):
    return _rms(o, g) * (1.0 - lam0)


def _proj_a_kernel(x_ref, g_ref, w_ref, cos_ref, sin_ref,
                   q_ref, k_ref, kb_ref, v_ref, vb_ref):
    d = x_ref.shape[1]
    xn = _rms(x_ref[...], g_ref[...]).astype(BF16)
    cos = cos_ref[...]
    sin = sin_ref[...]
    lane = lax.broadcasted_iota(jnp.int32, cos.shape, 1)
    first_half = (lane % HEAD_DIM) < (HEAD_DIM // 2)

    def rope(a):
        partner = jnp.where(first_half, pltpu.roll(a, LANES - HEAD_DIM // 2, 1),
                            pltpu.roll(a, HEAD_DIM // 2, 1))
        return a * cos + partner * sin

    q = jnp.dot(xn, w_ref[:, 0:d], preferred_element_type=F32)
    for c in range(d // LANES):
        sl = slice(c * LANES, (c + 1) * LANES)
        q_ref[:, sl] = (rope(q[:, sl]) * SCALE).astype(BF16)
    k = jnp.dot(xn, w_ref[:, d:2 * d], preferred_element_type=F32)
    for c in range(d // LANES):
        sl = slice(c * LANES, (c + 1) * LANES)
        kr = rope(k[:, sl])
        k_ref[:, sl] = kr
        kb_ref[:, sl] = kr.astype(BF16)
    v = jnp.dot(xn, w_ref[:, 2 * d:3 * d], preferred_element_type=F32)
    v_ref[...] = v
    vb_ref[...] = v.astype(BF16)


def _proj_a(x, g, w, cos, sin):
    t, d = x.shape
    tm = TOKEN_TILE
    period = cos.shape[0] // tm
    row = pl.BlockSpec((tm, d), lambda i: (i, 0))
    tab = pl.BlockSpec((tm, LANES), lambda i: (i % period, 0))
    return pl.pallas_call(
        _proj_a_kernel,
        grid=(t // tm,),
        in_specs=[row, _resident((1, d)), _resident((d, 3 * d)), tab, tab],
        out_specs=[row, row, row, row, row],
        out_shape=[jax.ShapeDtypeStruct((t, d), BF16), jax.ShapeDtypeStruct((t, d), F32),
                   jax.ShapeDtypeStruct((t, d), BF16), jax.ShapeDtypeStruct((t, d), F32),
                   jax.ShapeDtypeStruct((t, d), BF16)],
        compiler_params=_params("parallel"),
        name="proj_a",
    )(x, g, w, cos, sin)


def _attn_a_prompt_kernel(lp_ref, sg_ref, q_ref, k_ref, v_ref, o_ref, *, lam0):
    tq = ATTN_Q_TILE
    s_len = q_ref.shape[1]
    lam = _diff_lambda(lp_ref[...], lam0)
    sg = sg_ref[...]
    lane = lax.broadcasted_iota(jnp.int32, (tq, LANES), 1)
    row_chunk = lax.broadcasted_iota(jnp.int32, (tq, tq), 0) // CHUNK
    col_chunk = lax.broadcasted_iota(jnp.int32, (tq, tq), 1) // CHUNK
    diag_mask = row_chunk >= col_chunk

    def q_body(qi, _):
        q0 = pl.multiple_of(qi * tq, tq)
        q = q_ref[0, pl.ds(q0, tq), :]
        zero = jnp.zeros_like(q)
        q1 = jnp.where(lane < HEAD_DIM, q, zero)
        q2 = jnp.where(lane >= HEAD_DIM, q, zero)

        def block(j, carry, masked):
            k0 = pl.multiple_of(j * tq, tq)
            k = k_ref[0, pl.ds(k0, tq), :]
            v = v_ref[0, pl.ds(k0, tq), :]

            def update(qz, m, l, acc):
                s = lax.dot_general(qz, k, NT_DIMS, preferred_element_type=F32)
                if masked:
                    s = jnp.where(diag_mask, s, NEG_INF)
                m_new = jnp.maximum(m, jnp.max(s, axis=-1, keepdims=True))
                alpha = jnp.exp(m - m_new)
                p = jnp.exp(s - m_new)
                l = alpha * l + jnp.sum(p, axis=-1, keepdims=True)
                acc = alpha * acc + jnp.dot(p.astype(BF16), v, preferred_element_type=F32)
                return m_new, l, acc

            m1, l1, a1, m2, l2, a2 = carry
            m1, l1, a1 = update(q1, m1, l1, a1)
            m2, l2, a2 = update(q2, m2, l2, a2)
            return m1, l1, a1, m2, l2, a2

        m0 = jnp.full((tq, 1), NEG_INF, F32)
        l0 = jnp.zeros((tq, 1), F32)
        a0 = jnp.zeros((tq, LANES), F32)
        carry = lax.fori_loop(0, qi, lambda j, c: block(j, c, False), (m0, l0, a0, m0, l0, a0))
        m1, l1, a1, m2, l2, a2 = block(qi, carry, True)
        o = a1 / l1 - lam * (a2 / l2)
        o_ref[0, pl.ds(q0, tq), :] = _head_norm(o, sg, lam0).astype(BF16)
        return 0

    lax.fori_loop(0, s_len // tq, q_body, 0)


def _attn_a_prompt(lp, sg, q, k, v, lam0):
    b, s, d = q.shape
    blk = pl.BlockSpec((1, s, LANES), lambda bi, hi: (bi, 0, hi))
    return pl.pallas_call(
        functools.partial(_attn_a_prompt_kernel, lam0=lam0),
        grid=(b, d // LANES),
        in_specs=[_resident(lp.shape), _resident(sg.shape), blk, blk, blk],
        out_specs=blk,
        out_shape=jax.ShapeDtypeStruct((b, s, d), BF16),
        compiler_params=_params("parallel", "parallel"),
        name="attn_a_prompt",
    )(lp, sg, q, k, v)


def _attn_a_sample_kernel(lp_ref, sg_ref, q_ref, kn_ref, vn_ref, ck_ref, cv_ref, o_ref, *, lam0):
    nt, d = q_ref.shape[1], q_ref.shape[2]
    lam = _diff_lambda(lp_ref[...], lam0)
    sg = sg_ref[...]
    lane = lax.broadcasted_iota(jnp.int32, (nt, LANES), 1)

    for h in range(d // LANES):
        sl = slice(h * LANES, (h + 1) * LANES)
        q = q_ref[0, :, sl]
        zero = jnp.zeros_like(q)
        kc = ck_ref[0, :, sl].astype(BF16)
        vc = cv_ref[0, :, sl].astype(BF16)
        kn = kn_ref[0, :, sl]
        vn = vn_ref[0, :, sl]

        def softmax(qz):
            sc = lax.dot_general(qz, kc, NT_DIMS, preferred_element_type=F32)
            sn = lax.dot_general(qz, kn, NT_DIMS, preferred_element_type=F32)
            m = jnp.maximum(jnp.max(sc, axis=-1, keepdims=True), jnp.max(sn, axis=-1, keepdims=True))
            ec = jnp.exp(sc - m)
            en = jnp.exp(sn - m)
            inv = 1.0 / (jnp.sum(ec, axis=-1, keepdims=True) + jnp.sum(en, axis=-1, keepdims=True))
            return ec * inv, en * inv

        p1c, p1n = softmax(jnp.where(lane < HEAD_DIM, q, zero))
        p2c, p2n = softmax(jnp.where(lane >= HEAD_DIM, q, zero))
        pc = (p1c - lam * p2c).astype(BF16)
        pn = (p1n - lam * p2n).astype(BF16)
        o = (jnp.dot(pc, vc, preferred_element_type=F32)
             + jnp.dot(pn, vn, preferred_element_type=F32))
        o_ref[0, :, sl] = _head_norm(o, sg, lam0).astype(BF16)


def _attn_a_sample(lp, sg, q, kn, vn, ck, cv, lam0):
    b, nt, d = q.shape
    past = ck.shape[1]
    new = pl.BlockSpec((1, nt, d), lambda bi: (bi, 0, 0))
    old = pl.BlockSpec((1, past, d), lambda bi: (bi, 0, 0))
    return pl.pallas_call(
        functools.partial(_attn_a_sample_kernel, lam0=lam0),
        grid=(b,),
        in_specs=[_resident(lp.shape), _resident(sg.shape), new, new, new, old, old],
        out_specs=new,
        out_shape=jax.ShapeDtypeStruct((b, nt, d), BF16),
        compiler_params=_params("parallel"),
        name="attn_a_sample",
    )(lp, sg, q, kn, vn, ck, cv)


def _post_kernel(*refs, final):
    if final:
        x_ref, o_ref, wo_ref, g_ref, w1_ref, w2_ref, gf_ref, out_ref = refs
    else:
        x_ref, o_ref, wo_ref, g_ref, w1_ref, w2_ref, out_ref = refs
    h = x_ref[...] + jnp.dot(o_ref[...], wo_ref[...], preferred_element_type=F32)
    hn = _rms(h, g_ref[...]).astype(BF16)
    d_ff = w1_ref.shape[1]
    mlp = None
    for c in range(d_ff // FF_CHUNK):
        sl = slice(c * FF_CHUNK, (c + 1) * FF_CHUNK)
        a = jnp.dot(hn, w1_ref[:, sl], preferred_element_type=F32)
        u = jnp.square(jnp.maximum(a, 0.0)).astype(BF16)
        part = jnp.dot(u, w2_ref[sl, :], preferred_element_type=F32)
        mlp = part if mlp is None else mlp + part
    h = h + mlp
    if final:
        h = _rms(h, gf_ref[...])
    out_ref[...] = h


def _post(x, o, wo, g, w1, w2, gf=None):
    t, d = x.shape
    tm = TOKEN_TILE
    final = gf is not None
    row = pl.BlockSpec((tm, d), lambda i: (i, 0))
    in_specs = [row, row, _resident(wo.shape), _resident(g.shape), _resident(w1.shape), _resident(w2.shape)]
    args = [x, o, wo, g, w1, w2]
    if final:
        in_specs.append(_resident(gf.shape))
        args.append(gf)
    return pl.pallas_call(
        functools.partial(_post_kernel, final=final),
        grid=(t // tm,),
        in_specs=in_specs,
        out_specs=row,
        out_shape=jax.ShapeDtypeStruct((t, d), F32),
        compiler_params=_params("parallel"),
        name="post_final" if final else "post",
    )(*args)


def _proj_b_kernel(x_ref, gkv_ref, wkv_ref, gq_ref, wq_ref,
                   q_ref, kb_ref, vb_ref, k_ref, v_ref, *, period):
    d = x_ref.shape[1]
    x = x_ref[...]
    xs = x * lax.rsqrt(jnp.mean(x * x, axis=-1, keepdims=True) + EPS)
    xkv = (xs * gkv_ref[...]).astype(BF16)
    xq = (xs * gq_ref[...]).astype(BF16)
    q_ref[...] = (jnp.dot(xq, wq_ref[...], preferred_element_type=F32) * SCALE).astype(BF16)
    k = jnp.dot(xkv, wkv_ref[:, 0:d], preferred_element_type=F32)
    v = jnp.dot(xkv, wkv_ref[:, d:2 * d], preferred_element_type=F32)
    kb_ref[...] = k.astype(BF16)
    vb_ref[...] = v.astype(BF16)

    @pl.when(pl.program_id(0) % period == period - 1)
    def _():
        k_ref[...] = k
        v_ref[...] = v


def _proj_b(x, gkv, wkv, gq, wq, period):
    t, d = x.shape
    tm = TOKEN_TILE
    row = pl.BlockSpec((tm, d), lambda i: (i, 0))
    keep = pl.BlockSpec((tm, d), lambda i: (i // period, 0))
    return pl.pallas_call(
        functools.partial(_proj_b_kernel, period=period),
        grid=(t // tm,),
        in_specs=[row, _resident(gkv.shape), _resident(wkv.shape), _resident(gq.shape), _resident(wq.shape)],
        out_specs=[row, row, row, keep, keep],
        out_shape=[jax.ShapeDtypeStruct((t, d), BF16), jax.ShapeDtypeStruct((t, d), BF16),
                   jax.ShapeDtypeStruct((t, d), BF16),
                   jax.ShapeDtypeStruct((t // period, d), F32), jax.ShapeDtypeStruct((t // period, d), F32)],
        compiler_params=_params("arbitrary"),
        name="proj_b",
    )(x, gkv, wkv, gq, wq)


def _toeplitz_bias(r_row, rows, width):
    rb = jnp.broadcast_to(r_row, (rows, r_row.shape[1]))
    return pltpu.roll(rb, 0, 1, stride=1, stride_axis=0)


def _band_prompt_kernel(r_ref, q_ref, k_ref, v_ref, o_ref, kpad, vpad, bias):
    tq = BAND_Q_TILE
    s_len = q_ref.shape[1]
    win = PAD + tq
    kpad[0:PAD, :] = jnp.zeros((PAD, LANES), BF16)
    vpad[0:PAD, :] = jnp.zeros((PAD, LANES), BF16)
    kpad[PAD:, :] = k_ref[0]
    vpad[PAD:, :] = v_ref[0]

    row = lax.broadcasted_iota(jnp.int32, (tq, win), 0)
    col = lax.broadcasted_iota(jnp.int32, (tq, win), 1)
    dist = row // CHUNK + LEFT_CHUNKS - col // CHUNK
    in_band = (dist >= 0) & (dist <= LEFT_CHUNKS)
    for hh in range(2):
        shifted = _toeplitz_bias(r_ref[0, hh:hh + 1, :], tq, win)
        bias[hh] = jnp.where(in_band, shifted[:, tq:tq + win], NEG_INF)

    lane = lax.broadcasted_iota(jnp.int32, (tq, LANES), 1)

    def q_body(qi, check_start):
        q0 = pl.multiple_of(qi * tq, tq)
        q = q_ref[0, pl.ds(q0, tq), :]
        zero = jnp.zeros_like(q)
        kw = kpad[pl.ds(q0, win), :]
        vw = vpad[pl.ds(q0, win), :]
        outs = []
        for hh in range(2):
            qz = jnp.where((lane >= HEAD_DIM) == bool(hh), q, zero)
            s = lax.dot_general(qz, kw, NT_DIMS, preferred_element_type=F32) + bias[hh]
            if check_start:
                s = jnp.where(col + q0 >= PAD, s, NEG_INF)
            m = jnp.max(s, axis=-1, keepdims=True)
            e = jnp.exp(s - m)
            inv = 1.0 / jnp.sum(e, axis=-1, keepdims=True)
            outs.append(jnp.dot((e * inv).astype(BF16), vw, preferred_element_type=F32))
        o_ref[0, pl.ds(q0, tq), :] = jnp.where(lane < HEAD_DIM, outs[0], outs[1]).astype(BF16)

    n_start = PAD // tq

    def start_body(qi, _):
        q_body(qi, True)
        return 0

    def rest_body(qi, _):
        q_body(qi, False)
        return 0

    lax.fori_loop(0, n_start, start_body, 0)
    lax.fori_loop(n_start, s_len // tq, rest_body, 0)


def _band_prompt(r, q, k, v):
    b, s, d = q.shape
    blk = pl.BlockSpec((1, s, LANES), lambda bi, pi: (bi, 0, pi))
    rblk = pl.BlockSpec((1, 2, r.shape[2]), lambda bi, pi: (pi, 0, 0))
    win = PAD + BAND_Q_TILE
    return pl.pallas_call(
        _band_prompt_kernel,
        grid=(b, d // LANES),
        in_specs=[rblk, blk, blk, blk],
        out_specs=blk,
        out_shape=jax.ShapeDtypeStruct((b, s, d), BF16),
        scratch_shapes=[pltpu.VMEM((PAD + s, LANES), BF16), pltpu.VMEM((PAD + s, LANES), BF16),
                        pltpu.VMEM((2, BAND_Q_TILE, win), F32)],
        compiler_params=_params("parallel", "parallel"),
        name="band_prompt",
    )(r, q, k, v)


def _band_sample_kernel(r_ref, q_ref, kn_ref, vn_ref, ck_ref, cv_ref, o_ref):
    nt, d = q_ref.shape[1], q_ref.shape[2]
    lb = ck_ref.shape[1]
    off = BAND_Q_TILE
    lane = lax.broadcasted_iota(jnp.int32, (nt, LANES), 1)
    for p in range(d // LANES):
        sl = slice(p * LANES, (p + 1) * LANES)
        q = q_ref[0, :, sl]
        zero = jnp.zeros_like(q)
        kc = ck_ref[0, :, sl].astype(BF16)
        vc = cv_ref[0, :, sl].astype(BF16)
        kn = kn_ref[0, :, sl]
        vn = vn_ref[0, :, sl]
        outs = []
        for hh in range(2):
            shifted = _toeplitz_bias(r_ref[p, hh:hh + 1, :], nt, lb + nt)
            qz = jnp.where((lane >= HEAD_DIM) == bool(hh), q, zero)
            sc = lax.dot_general(qz, kc, NT_DIMS, preferred_element_type=F32) + shifted[:, off:off + lb]
            sn = (lax.dot_general(qz, kn, NT_DIMS, preferred_element_type=F32)
                  + shifted[:, off + lb:off + lb + nt])
            m = jnp.maximum(jnp.max(sc, axis=-1, keepdims=True), jnp.max(sn, axis=-1, keepdims=True))
            ec = jnp.exp(sc - m)
            en = jnp.exp(sn - m)
            inv = 1.0 / (jnp.sum(ec, axis=-1, keepdims=True) + jnp.sum(en, axis=-1, keepdims=True))
            outs.append(jnp.dot((ec * inv).astype(BF16), vc, preferred_element_type=F32)
                        + jnp.dot((en * inv).astype(BF16), vn, preferred_element_type=F32))
        o_ref[0, :, sl] = jnp.where(lane < HEAD_DIM, outs[0], outs[1]).astype(BF16)


def _band_sample(r, q, kn, vn, ck, cv):
    b, nt, d = q.shape
    lb = ck.shape[1]
    new = pl.BlockSpec((1, nt, d), lambda bi: (bi, 0, 0))
    old = pl.BlockSpec((1, lb, d), lambda bi: (bi, 0, 0))
    return pl.pallas_call(
        _band_sample_kernel,
        grid=(b,),
        in_specs=[_resident(r.shape), new, new, new, old, old],
        out_specs=new,
        out_shape=jax.ShapeDtypeStruct((b, nt, d), BF16),
        compiler_params=_params("parallel"),
        name="band_sample",
    )(r, q, kn, vn, ck, cv)


def _rope_tables(pos):
    half = HEAD_DIM // 2
    inv = 1.0 / (ROPE_THETA ** (jnp.arange(half, dtype=F32) / half))
    ang = pos.astype(F32)[:, None] * inv[None, :]
    cos, sin = jnp.cos(ang), jnp.sin(ang)
    reps = LANES // HEAD_DIM
    return (jnp.tile(jnp.concatenate([cos, cos], axis=-1), (1, reps)),
            jnp.tile(jnp.concatenate([-sin, sin], axis=-1), (1, reps)))


def _bias_rows(table):
    far = PAD + BAND_Q_TILE - REL_CLIP
    near = table[:, 2 * REL_CLIP:0:-1]
    rows = jnp.concatenate([jnp.broadcast_to(table[:, 2 * REL_CLIP:], (table.shape[0], far)), near], axis=1)
    return rows.reshape(table.shape[0] // 2, 2, rows.shape[1])


def _lambda_init(layer):
    return 0.8 - 0.6 * math.exp(-0.3 * layer)


def kernel(x_prompt, x_sample, cache_a_k, cache_a_v, cache_b_k, cache_b_v, g_attn, w_a_qkv, a_lambda, a_subln, w_a_o, g_kv, w_kv, w_b_q, b_rel, w_b_o, g_mlp, w_ff1, w_ff2, g_final):
    nb, seq, d = x_prompt.shape
    db, nt, _ = x_sample.shape
    past = cache_a_k.shape[2]
    lb = cache_b_k.shape[1]
    a_heads = d // (2 * HEAD_DIM)
    b_heads = d // HEAD_DIM
    assert w_a_qkv.shape[0] == 1 and w_b_q.shape[0] == 1, "one differential and one band layer"
    assert (db * nt) % TOKEN_TILE == 0 and TOKEN_TILE % nt == 0 and seq % TOKEN_TILE == 0
    keep = min(PAD, seq)
    assert keep == TOKEN_TILE and lb == PAD

    lam0 = _lambda_init(0)
    w_qkv = w_a_qkv[0].astype(BF16)
    w_ao = w_a_o[0].astype(BF16)
    w_kvb = w_kv.astype(BF16)
    w_bq = w_b_q[0].astype(BF16)
    w_bo = w_b_o[0].astype(BF16)
    w1 = w_ff1.astype(BF16)
    w2 = w_ff2.astype(BF16)
    g_a = g_attn[0][None]
    g_b = g_attn[1][None]
    g_k = g_kv[None]
    g_m0 = g_mlp[0][None]
    g_m1 = g_mlp[1][None]
    g_f = g_final[None]
    lp = a_lambda[0]
    sg = a_subln[0][None]
    r = _bias_rows(b_rel[0])

    cos_p, sin_p = _rope_tables(jnp.arange(seq))
    cos_s, sin_s = _rope_tables(jnp.arange(past, past + nt))
    reps = TOKEN_TILE // nt
    cos_s, sin_s = jnp.tile(cos_s, (reps, 1)), jnp.tile(sin_s, (reps, 1))

    def trunk(x, cos, sin, attn_a, attn_b, period):
        t = x.shape[0]
        q, k, kb, v, vb = _proj_a(x, g_a, w_qkv, cos, sin)
        o = attn_a(q, kb, vb)
        h = _post(x, o, w_ao, g_m0, w1[0], w2[0])
        qb, kbb, vbb, k_sh, v_sh = _proj_b(h, g_k, w_kvb, g_b, w_bq, period)
        ob = attn_b(qb, kbb, vbb)
        y = _post(h, ob, w_bo, g_m1, w1[1], w2[1], g_f)
        return y, k, v, k_sh, v_sh

    def attn_a_p(q, kb, vb):
        sh = (nb, seq, d)
        return _attn_a_prompt(lp, sg, q.reshape(sh), kb.reshape(sh), vb.reshape(sh), lam0).reshape(nb * seq, d)

    def attn_b_p(q, kb, vb):
        sh = (nb, seq, d)
        return _band_prompt(r, q.reshape(sh), kb.reshape(sh), vb.reshape(sh)).reshape(nb * seq, d)

    ck_a = cache_a_k[0].reshape(db, past, d)
    cv_a = cache_a_v[0].reshape(db, past, d)
    ck_b = cache_b_k.reshape(db, lb, d)
    cv_b = cache_b_v.reshape(db, lb, d)

    def attn_a_s(q, kb, vb):
        sh = (db, nt, d)
        return _attn_a_sample(lp, sg, q.reshape(sh), kb.reshape(sh), vb.reshape(sh), ck_a, cv_a, lam0).reshape(db * nt, d)

    def attn_b_s(q, kb, vb):
        sh = (db, nt, d)
        return _band_sample(r, q.reshape(sh), kb.reshape(sh), vb.reshape(sh), ck_b, cv_b).reshape(db * nt, d)

    y_p, ak_p, av_p, bk_p, bv_p = trunk(x_prompt.reshape(nb * seq, d), cos_p, sin_p, attn_a_p, attn_b_p,
                                        seq // TOKEN_TILE)
    y_s, ak_s, av_s, bk_s, bv_s = trunk(x_sample.reshape(db * nt, d), cos_s, sin_s, attn_a_s, attn_b_s, 1)

    return (y_p.reshape(nb, seq, d), y_s.reshape(db, nt, d),
            ak_p.reshape(1, nb, seq, a_heads, 2 * HEAD_DIM), av_p.reshape(1, nb, seq, a_heads, 2 * HEAD_DIM),
            bk_p.reshape(nb, keep, b_heads, HEAD_DIM), bv_p.reshape(nb, keep, b_heads, HEAD_DIM),
            ak_s.reshape(1, db, nt, a_heads, 2 * HEAD_DIM), av_s.reshape(1, db, nt, a_heads, 2 * HEAD_DIM),
            bk_s.reshape(db, nt, b_heads, HEAD_DIM), bv_s.reshape(db, nt, b_heads, HEAD_DIM))
```

```python
import functools
import math

import jax
import jax.numpy as jnp
from jax import lax
from jax.experimental import pallas as pl
from jax.experimental.pallas import tpu as pltpu

HEAD_DIM = 64
CHUNK = 64
LEFT_CHUNKS = 8
PAD = LEFT_CHUNKS * CHUNK
REL_CLIP = 128
ROPE_THETA = 10000.0
EPS = 1e-6
NEG_INF = -1e30
LANES = 128
SCALE = HEAD_DIM ** -0.5

TOKEN_TILE = 512
FF_CHUNK = 1024
ATTN_TILE = 512
BAND_TILE = 256
BAND_Q_TILE = 128
VMEM_LIMIT = 56 * 1024 * 1024

F32 = jnp.float32
BF16 = jnp.bfloat16
NT_DIMS = (((1,), (1,)), ((), ()))


def _rms(x, g):
    ms = jnp.mean(x * x, axis=-1, keepdims=True)
    return x * lax.rsqrt(ms + EPS) * g


def _resident(shape):
    return pl.BlockSpec(shape, lambda *_: (0,) * len(shape), pipeline_mode=pl.Buffered(1))


def _params(*sem):
    return pltpu.CompilerParams(dimension_semantics=sem, vmem_limit_bytes=VMEM_LIMIT)


def _diff_lambda(lp, lam0):
    a = jnp.sum(lp[0:1] * lp[1:2], axis=-1, keepdims=True)
    b = jnp.sum(lp[2:3] * lp[3:4], axis=-1, keepdims=True)
    return jnp.exp(a) - jnp.exp(b) + lam0


def _head_norm(o, g, lam0):
    return _rms(o, g) * (1.0 - lam0)


def _proj_a_kernel(x_ref, g_ref, w_ref, cos_ref, sin_ref,
                   q_ref, k_ref, kb_ref, v_ref, vb_ref):
    d = x_ref.shape[1]
    xn = _rms(x_ref[...], g_ref[...]).astype(BF16)
    cos = cos_ref[...]
    sin = sin_ref[...]
    lane = lax.broadcasted_iota(jnp.int32, cos.shape, 1)
    first_half = (lane % HEAD_DIM) < (HEAD_DIM // 2)

    def rope(a):
        partner = jnp.where(first_half, pltpu.roll(a, LANES - HEAD_DIM // 2, 1),
                            pltpu.roll(a, HEAD_DIM // 2, 1))
        return a * cos + partner * sin

    q = jnp.dot(xn, w_ref[:, 0:d], preferred_element_type=F32)
    for c in range(d // LANES):
        sl = slice(c * LANES, (c + 1) * LANES)
        q_ref[:, sl] = (rope(q[:, sl]) * SCALE).astype(BF16)
    k = jnp.dot(xn, w_ref[:, d:2 * d], preferred_element_type=F32)
    for c in range(d // LANES):
        sl = slice(c * LANES, (c + 1) * LANES)
        kr = rope(k[:, sl])
        k_ref[:, sl] = kr
        kb_ref[:, sl] = kr.astype(BF16)
    v = jnp.dot(xn, w_ref[:, 2 * d:3 * d], preferred_element_type=F32)
    v_ref[...] = v
    vb_ref[...] = v.astype(BF16)


def _proj_a(x, g, w, cos, sin):
    t, d = x.shape
    tm = TOKEN_TILE
    period = cos.shape[0] // tm
    row = pl.BlockSpec((tm, d), lambda i: (i, 0))
    tab = pl.BlockSpec((tm, LANES), lambda i: (i % period, 0))
    return pl.pallas_call(
        _proj_a_kernel,
        grid=(t // tm,),
        in_specs=[row, _resident((1, d)), _resident((d, 3 * d)), tab, tab],
        out_specs=[row, row, row, row, row],
        out_shape=[jax.ShapeDtypeStruct((t, d), BF16), jax.ShapeDtypeStruct((t, d), F32),
                   jax.ShapeDtypeStruct((t, d), BF16), jax.ShapeDtypeStruct((t, d), F32),
                   jax.ShapeDtypeStruct((t, d), BF16)],
        compiler_params=_params("parallel"),
        name="proj_a",
    )(x, g, w, cos, sin)


def _attn_a_prompt_kernel(lp_ref, sg_ref, q_ref, k_ref, v_ref, o_ref, vt_ref, *, lam0):
    t = ATTN_TILE
    s_len = q_ref.shape[1]
    lam = _diff_lambda(lp_ref[...], lam0)
    sg = sg_ref[...]
    for c in range(s_len // t):
        sl = slice(c * t, (c + 1) * t)
        vt_ref[:, sl] = v_ref[0, sl, :].astype(F32).T.astype(BF16)

    lane = lax.broadcasted_iota(jnp.int32, (t, LANES), 1)
    key_chunk = lax.broadcasted_iota(jnp.int32, (t, 2 * t), 0) // CHUNK
    qry_chunk = (lax.broadcasted_iota(jnp.int32, (t, 2 * t), 1) % t) // CHUNK
    diag_mask = qry_chunk >= key_chunk

    def q_body(qi, _):
        q0 = pl.multiple_of(qi * t, t)
        q = q_ref[0, pl.ds(q0, t), :]
        zero = jnp.zeros_like(q)
        qz = jnp.concatenate([jnp.where(lane < HEAD_DIM, q, zero),
                              jnp.where(lane >= HEAD_DIM, q, zero)], axis=0)

        def block(j, carry, masked):
            m, l, acc = carry
            k0 = pl.multiple_of(j * t, t)
            k = k_ref[0, pl.ds(k0, t), :]
            s = lax.dot_general(k, qz, NT_DIMS, preferred_element_type=F32)
            if masked:
                s = jnp.where(diag_mask, s, NEG_INF)
            m_new = jnp.maximum(m, jnp.max(s, axis=0, keepdims=True))
            alpha = jnp.exp(m - m_new)
            p = jnp.exp(s - m_new)
            l = alpha * l + jnp.sum(p, axis=0, keepdims=True)
            acc = alpha * acc + jnp.dot(vt_ref[:, pl.ds(k0, t)], p.astype(BF16),
                                        preferred_element_type=F32)
            return m_new, l, acc

        init = (jnp.full((1, 2 * t), NEG_INF, F32), jnp.zeros((1, 2 * t), F32),
                jnp.zeros((LANES, 2 * t), F32))
        carry = lax.fori_loop(0, qi, lambda j, c: block(j, c, False), init)
        m, l, acc = block(qi, carry, True)
        acc = acc / l
        o = acc[:, 0:t] - lam * acc[:, t:2 * t]
        o = o * lax.rsqrt(jnp.mean(o * o, axis=0, keepdims=True) + EPS)
        o_ref[0, pl.ds(q0, t), :] = (o.T * sg * (1.0 - lam0)).astype(BF16)
        return 0

    lax.fori_loop(0, s_len // t, q_body, 0)


def _attn_a_prompt(lp, sg, q, k, v, lam0):
    b, s, d = q.shape
    blk = pl.BlockSpec((1, s, LANES), lambda bi, hi: (bi, 0, hi))
    return pl.pallas_call(
        functools.partial(_attn_a_prompt_kernel, lam0=lam0),
        grid=(b, d // LANES),
        in_specs=[_resident(lp.shape), _resident(sg.shape), blk, blk, blk],
        out_specs=blk,
        out_shape=jax.ShapeDtypeStruct((b, s, d), BF16),
        scratch_shapes=[pltpu.VMEM((LANES, s), BF16)],
        compiler_params=_params("parallel", "parallel"),
        name="attn_a_prompt",
    )(lp, sg, q, k, v)


def _attn_a_sample_kernel(lp_ref, sg_ref, q_ref, kn_ref, vn_ref, ck_ref, cv_ref, o_ref, *, lam0):
    nt, d = q_ref.shape[1], q_ref.shape[2]
    lam = _diff_lambda(lp_ref[...], lam0)
    sg = sg_ref[...]
    lane = lax.broadcasted_iota(jnp.int32, (nt, LANES), 1)

    for h in range(d // LANES):
        sl = slice(h * LANES, (h + 1) * LANES)
        q = q_ref[0, :, sl]
        zero = jnp.zeros_like(q)
        kc = ck_ref[0, :, sl].astype(BF16)
        vc = cv_ref[0, :, sl].astype(BF16)
        kn = kn_ref[0, :, sl]
        vn = vn_ref[0, :, sl]

        def softmax(qz):
            sc = lax.dot_general(qz, kc, NT_DIMS, preferred_element_type=F32)
            sn = lax.dot_general(qz, kn, NT_DIMS, preferred_element_type=F32)
            m = jnp.maximum(jnp.max(sc, axis=-1, keepdims=True), jnp.max(sn, axis=-1, keepdims=True))
            ec = jnp.exp(sc - m)
            en = jnp.exp(sn - m)
            inv = 1.0 / (jnp.sum(ec, axis=-1, keepdims=True) + jnp.sum(en, axis=-1, keepdims=True))
            return ec * inv, en * inv

        p1c, p1n = softmax(jnp.where(lane < HEAD_DIM, q, zero))
        p2c, p2n = softmax(jnp.where(lane >= HEAD_DIM, q, zero))
        pc = (p1c - lam * p2c).astype(BF16)
        pn = (p1n - lam * p2n).astype(BF16)
        o = (jnp.dot(pc, vc, preferred_element_type=F32)
             + jnp.dot(pn, vn, preferred_element_type=F32))
        o_ref[0, :, sl] = _head_norm(o, sg, lam0).astype(BF16)


def _attn_a_sample(lp, sg, q, kn, vn, ck, cv, lam0):
    b, nt, d = q.shape
    past = ck.shape[1]
    new = pl.BlockSpec((1, nt, d), lambda bi: (bi, 0, 0))
    old = pl.BlockSpec((1, past, d), lambda bi: (bi, 0, 0))
    return pl.pallas_call(
        functools.partial(_attn_a_sample_kernel, lam0=lam0),
        grid=(b,),
        in_specs=[_resident(lp.shape), _resident(sg.shape), new, new, new, old, old],
        out_specs=new,
        out_shape=jax.ShapeDtypeStruct((b, nt, d), BF16),
        compiler_params=_params("parallel"),
        name="attn_a_sample",
    )(lp, sg, q, kn, vn, ck, cv)


def _post_kernel(*refs, final):
    if final:
        x_ref, o_ref, wo_ref, g_ref, w1_ref, w2_ref, gf_ref, out_ref = refs
    else:
        x_ref, o_ref, wo_ref, g_ref, w1_ref, w2_ref, out_ref = refs
    h = x_ref[...] + jnp.dot(o_ref[...], wo_ref[...], preferred_element_type=F32)
    hn = _rms(h, g_ref[...]).astype(BF16)
    d_ff = w1_ref.shape[1]
    mlp = None
    for c in range(d_ff // FF_CHUNK):
        sl = slice(c * FF_CHUNK, (c + 1) * FF_CHUNK)
        a = jnp.dot(hn, w1_ref[:, sl], preferred_element_type=F32)
        u = jnp.square(jnp.maximum(a, 0.0)).astype(BF16)
        part = jnp.dot(u, w2_ref[sl, :], preferred_element_type=F32)
        mlp = part if mlp is None else mlp + part
    h = h + mlp
    if final:
        h = _rms(h, gf_ref[...])
    out_ref[...] = h


def _post(x, o, wo, g, w1, w2, gf=None):
    t, d = x.shape
    tm = TOKEN_TILE
    final = gf is not None
    row = pl.BlockSpec((tm, d), lambda i: (i, 0))
    in_specs = [row, row, _resident(wo.shape), _resident(g.shape), _resident(w1.shape), _resident(w2.shape)]
    args = [x, o, wo, g, w1, w2]
    if final:
        in_specs.append(_resident(gf.shape))
        args.append(gf)
    return pl.pallas_call(
        functools.partial(_post_kernel, final=final),
        grid=(t // tm,),
        in_specs=in_specs,
        out_specs=row,
        out_shape=jax.ShapeDtypeStruct((t, d), F32),
        compiler_params=_params("parallel"),
        name="post_final" if final else "post",
    )(*args)


def _proj_b_kernel(x_ref, gkv_ref, wkv_ref, gq_ref, wq_ref,
                   q_ref, kb_ref, vb_ref, k_ref, v_ref, *, period):
    d = x_ref.shape[1]
    x = x_ref[...]
    xs = x * lax.rsqrt(jnp.mean(x * x, axis=-1, keepdims=True) + EPS)
    xkv = (xs * gkv_ref[...]).astype(BF16)
    xq = (xs * gq_ref[...]).astype(BF16)
    q_ref[...] = (jnp.dot(xq, wq_ref[...], preferred_element_type=F32) * SCALE).astype(BF16)
    k = jnp.dot(xkv, wkv_ref[:, 0:d], preferred_element_type=F32)
    v = jnp.dot(xkv, wkv_ref[:, d:2 * d], preferred_element_type=F32)
    kb_ref[...] = k.astype(BF16)
    vb_ref[...] = v.astype(BF16)

    @pl.when(pl.program_id(0) % period == period - 1)
    def _():
        k_ref[...] = k
        v_ref[...] = v


def _proj_b(x, gkv, wkv, gq, wq, period):
    t, d = x.shape
    tm = TOKEN_TILE
    row = pl.BlockSpec((tm, d), lambda i: (i, 0))
    keep = pl.BlockSpec((tm, d), lambda i: (i // period, 0))
    return pl.pallas_call(
        functools.partial(_proj_b_kernel, period=period),
        grid=(t // tm,),
        in_specs=[row, _resident(gkv.shape), _resident(wkv.shape), _resident(gq.shape), _resident(wq.shape)],
        out_specs=[row, row, row, keep, keep],
        out_shape=[jax.ShapeDtypeStruct((t, d), BF16), jax.ShapeDtypeStruct((t, d), BF16),
                   jax.ShapeDtypeStruct((t, d), BF16),
                   jax.ShapeDtypeStruct((t // period, d), F32), jax.ShapeDtypeStruct((t // period, d), F32)],
        compiler_params=_params("arbitrary"),
        name="proj_b",
    )(x, gkv, wkv, gq, wq)


def _toeplitz_bias(r_row, rows, width):
    rb = jnp.broadcast_to(r_row, (rows, r_row.shape[1]))
    return pltpu.roll(rb, 0, 1, stride=1, stride_axis=0)


def _band_prompt_kernel(r_ref, q_ref, k_ref, v_ref, o_ref, kpad, vtpad, bias):
    tq = BAND_TILE
    s_len = q_ref.shape[1]
    win = PAD + tq
    kpad[0:PAD, :] = jnp.zeros((PAD, LANES), BF16)
    vtpad[:, 0:PAD] = jnp.zeros((LANES, PAD), BF16)
    kpad[PAD:, :] = k_ref[0]
    for c in range(s_len // tq):
        vtpad[:, PAD + c * tq:PAD + (c + 1) * tq] = (
            v_ref[0, c * tq:(c + 1) * tq, :].astype(F32).T.astype(BF16))

    key = lax.broadcasted_iota(jnp.int32, (win, tq), 0)

    @pl.when(pl.program_id(1) == 0)
    def _():
        qry = lax.broadcasted_iota(jnp.int32, (win, tq), 1)
        dist = qry // CHUNK + LEFT_CHUNKS - key // CHUNK
        in_band = (dist >= 0) & (dist <= LEFT_CHUNKS)
        for hh in range(2):
            rb = jnp.broadcast_to(r_ref[0, hh:hh + 1, :], (win, win + tq))
            shifted = pltpu.roll(rb, 0, 1, stride=1, stride_axis=0)
            bias[:, hh * tq:(hh + 1) * tq] = jnp.where(in_band, shifted[:, win:win + tq], NEG_INF)

    lane = lax.broadcasted_iota(jnp.int32, (tq, LANES), 1)
    key2 = lax.broadcasted_iota(jnp.int32, (win, 2 * tq), 0)

    def q_body(qi, check_start):
        q0 = pl.multiple_of(qi * tq, tq)
        q = q_ref[0, pl.ds(q0, tq), :]
        zero = jnp.zeros_like(q)
        qz = jnp.concatenate([jnp.where(lane < HEAD_DIM, q, zero),
                              jnp.where(lane >= HEAD_DIM, q, zero)], axis=0)
        s = lax.dot_general(kpad[pl.ds(q0, win), :], qz, NT_DIMS,
                            preferred_element_type=F32) + bias[...]
        if check_start:
            s = jnp.where(key2 + q0 >= PAD, s, NEG_INF)
        m = jnp.max(s, axis=0, keepdims=True)
        e = jnp.exp(s - m)
        l = jnp.sum(e, axis=0, keepdims=True)
        o = jnp.dot(vtpad[:, pl.ds(q0, win)], e.astype(BF16), preferred_element_type=F32) / l
        o = jnp.concatenate([o[0:HEAD_DIM, 0:tq], o[HEAD_DIM:LANES, tq:2 * tq]], axis=0)
        o_ref[0, pl.ds(q0, tq), :] = o.T.astype(BF16)

    n_start = PAD // tq

    def start_body(qi, _):
        q_body(qi, True)
        return 0

    def rest_body(qi, _):
        q_body(qi, False)
        return 0

    lax.fori_loop(0, n_start, start_body, 0)
    lax.fori_loop(n_start, s_len // tq, rest_body, 0)


def _band_prompt(r, q, k, v):
    b, s, d = q.shape
    blk = pl.BlockSpec((1, s, LANES), lambda pi, bi: (bi, 0, pi))
    rblk = pl.BlockSpec((1, 2, r.shape[2]), lambda pi, bi: (pi, 0, 0))
    win = PAD + BAND_TILE
    return pl.pallas_call(
        _band_prompt_kernel,
        grid=(d // LANES, b),
        in_specs=[rblk, blk, blk, blk],
        out_specs=blk,
        out_shape=jax.ShapeDtypeStruct((b, s, d), BF16),
        scratch_shapes=[pltpu.VMEM((PAD + s, LANES), BF16), pltpu.VMEM((LANES, PAD + s), BF16),
                        pltpu.VMEM((win, 2 * BAND_TILE), F32)],
        compiler_params=_params("arbitrary", "arbitrary"),
        name="band_prompt",
    )(r, q, k, v)


def _band_sample_kernel(r_ref, q_ref, kn_ref, vn_ref, ck_ref, cv_ref, o_ref):
    nt, d = q_ref.shape[1], q_ref.shape[2]
    lb = ck_ref.shape[1]
    off = BAND_Q_TILE
    lane = lax.broadcasted_iota(jnp.int32, (nt, LANES), 1)
    for p in range(d // LANES):
        sl = slice(p * LANES, (p + 1) * LANES)
        q = q_ref[0, :, sl]
        zero = jnp.zeros_like(q)
        kc = ck_ref[0, :, sl].astype(BF16)
        vc = cv_ref[0, :, sl].astype(BF16)
        kn = kn_ref[0, :, sl]
        vn = vn_ref[0, :, sl]
        outs = []
        for hh in range(2):
            shifted = _toeplitz_bias(r_ref[p, hh:hh + 1, :], nt, lb + nt)
            qz = jnp.where((lane >= HEAD_DIM) == bool(hh), q, zero)
            sc = lax.dot_general(qz, kc, NT_DIMS, preferred_element_type=F32) + shifted[:, off:off + lb]
            sn = (lax.dot_general(qz, kn, NT_DIMS, preferred_element_type=F32)
                  + shifted[:, off + lb:off + lb + nt])
            m = jnp.maximum(jnp.max(sc, axis=-1, keepdims=True), jnp.max(sn, axis=-1, keepdims=True))
            ec = jnp.exp(sc - m)
            en = jnp.exp(sn - m)
            inv = 1.0 / (jnp.sum(ec, axis=-1, keepdims=True) + jnp.sum(en, axis=-1, keepdims=True))
            outs.append(jnp.dot((ec * inv).astype(BF16), vc, preferred_element_type=F32)
                        + jnp.dot((en * inv).astype(BF16), vn, preferred_element_type=F32))
        o_ref[0, :, sl] = jnp.where(lane < HEAD_DIM, outs[0], outs[1]).astype(BF16)


def _band_sample(r, q, kn, vn, ck, cv):
    b, nt, d = q.shape
    lb = ck.shape[1]
    new = pl.BlockSpec((1, nt, d), lambda bi: (bi, 0, 0))
    old = pl.BlockSpec((1, lb, d), lambda bi: (bi, 0, 0))
    return pl.pallas_call(
        _band_sample_kernel,
        grid=(b,),
        in_specs=[_resident(r.shape), new, new, new, old, old],
        out_specs=new,
        out_shape=jax.ShapeDtypeStruct((b, nt, d), BF16),
        compiler_params=_params("parallel"),
        name="band_sample",
    )(r, q, kn, vn, ck, cv)


def _rope_tables(pos):
    half = HEAD_DIM // 2
    inv = 1.0 / (ROPE_THETA ** (jnp.arange(half, dtype=F32) / half))
    ang = pos.astype(F32)[:, None] * inv[None, :]
    cos, sin = jnp.cos(ang), jnp.sin(ang)
    reps = LANES // HEAD_DIM
    return (jnp.tile(jnp.concatenate([cos, cos], axis=-1), (1, reps)),
            jnp.tile(jnp.concatenate([-sin, sin], axis=-1), (1, reps)))


def _bias_rows(table):
    far = PAD + BAND_Q_TILE - REL_CLIP
    near = table[:, 2 * REL_CLIP:0:-1]
    rows = jnp.concatenate([jnp.broadcast_to(table[:, 2 * REL_CLIP:], (table.shape[0], far)), near], axis=1)
    return rows.reshape(table.shape[0] // 2, 2, rows.shape[1])


def _bias_rows_t(table):
    heads = table.shape[0]
    lo = BAND_TILE - REL_CLIP
    hi = PAD + 2 * BAND_TILE - lo - (2 * REL_CLIP + 1)
    rows = jnp.concatenate([jnp.broadcast_to(table[:, :1], (heads, lo)), table,
                            jnp.broadcast_to(table[:, 2 * REL_CLIP:], (heads, hi))], axis=1)
    return rows.reshape(heads // 2, 2, rows.shape[1])


def _lambda_init(layer):
    return 0.8 - 0.6 * math.exp(-0.3 * layer)


def kernel(x_prompt, x_sample, cache_a_k, cache_a_v, cache_b_k, cache_b_v, g_attn, w_a_qkv, a_lambda, a_subln, w_a_o, g_kv, w_kv, w_b_q, b_rel, w_b_o, g_mlp, w_ff1, w_ff2, g_final):
    nb, seq, d = x_prompt.shape
    db, nt, _ = x_sample.shape
    past = cache_a_k.shape[2]
    lb = cache_b_k.shape[1]
    a_heads = d // (2 * HEAD_DIM)
    b_heads = d // HEAD_DIM
    assert w_a_qkv.shape[0] == 1 and w_b_q.shape[0] == 1, "one differential and one band layer"
    assert (db * nt) % TOKEN_TILE == 0 and TOKEN_TILE % nt == 0 and seq % TOKEN_TILE == 0
    keep = min(PAD, seq)
    assert keep == TOKEN_TILE and lb == PAD

    lam0 = _lambda_init(0)
    w_qkv = w_a_qkv[0].astype(BF16)
    w_ao = w_a_o[0].astype(BF16)
    w_kvb = w_kv.astype(BF16)
    w_bq = w_b_q[0].astype(BF16)
    w_bo = w_b_o[0].astype(BF16)
    w1 = w_ff1.astype(BF16)
    w2 = w_ff2.astype(BF16)
    g_a = g_attn[0][None]
    g_b = g_attn[1][None]
    g_k = g_kv[None]
    g_m0 = g_mlp[0][None]
    g_m1 = g_mlp[1][None]
    g_f = g_final[None]
    lp = a_lambda[0]
    sg = a_subln[0][None]
    r = _bias_rows(b_rel[0])
    r_t = _bias_rows_t(b_rel[0])

    cos_p, sin_p = _rope_tables(jnp.arange(seq))
    cos_s, sin_s = _rope_tables(jnp.arange(past, past + nt))
    reps = TOKEN_TILE // nt
    cos_s, sin_s = jnp.tile(cos_s, (reps, 1)), jnp.tile(sin_s, (reps, 1))

    def trunk(x, cos, sin, attn_a, attn_b, period):
        t = x.shape[0]
        q, k, kb, v, vb = _proj_a(x, g_a, w_qkv, cos, sin)
        o = attn_a(q, kb, vb)
        h = _post(x, o, w_ao, g_m0, w1[0], w2[0])
        qb, kbb, vbb, k_sh, v_sh = _proj_b(h, g_k, w_kvb, g_b, w_bq, period)
        ob = attn_b(qb, kbb, vbb)
        y = _post(h, ob, w_bo, g_m1, w1[1], w2[1], g_f)
        return y, k, v, k_sh, v_sh

    def attn_a_p(q, kb, vb):
        sh = (nb, seq, d)
        return _attn_a_prompt(lp, sg, q.reshape(sh), kb.reshape(sh), vb.reshape(sh), lam0).reshape(nb * seq, d)

    def attn_b_p(q, kb, vb):
        sh = (nb, seq, d)
        return _band_prompt(r_t, q.reshape(sh), kb.reshape(sh), vb.reshape(sh)).reshape(nb * seq, d)

    ck_a = cache_a_k[0].reshape(db, past, d)
    cv_a = cache_a_v[0].reshape(db, past, d)
    ck_b = cache_b_k.reshape(db, lb, d)
    cv_b = cache_b_v.reshape(db, lb, d)

    def attn_a_s(q, kb, vb):
        sh = (db, nt, d)
        return _attn_a_sample(lp, sg, q.reshape(sh), kb.reshape(sh), vb.reshape(sh), ck_a, cv_a, lam0).reshape(db * nt, d)

    def attn_b_s(q, kb, vb):
        sh = (db, nt, d)
        return _band_sample(r, q.reshape(sh), kb.reshape(sh), vb.reshape(sh), ck_b, cv_b).reshape(db * nt, d)

    y_p, ak_p, av_p, bk_p, bv_p = trunk(x_prompt.reshape(nb * seq, d), cos_p, sin_p, attn_a_p, attn_b_p,
                                        seq // TOKEN_TILE)
    y_s, ak_s, av_s, bk_s, bv_s = trunk(x_sample.reshape(db * nt, d), cos_s, sin_s, attn_a_s, attn_b_s, 1)

    return (y_p.reshape(nb, seq, d), y_s.reshape(db, nt, d),
            ak_p.reshape(1, nb, seq, a_heads, 2 * HEAD_DIM), av_p.reshape(1, nb, seq, a_heads, 2 * HEAD_DIM),
            bk_p.reshape(nb, keep, b_heads, HEAD_DIM), bv_p.reshape(nb, keep, b_heads, HEAD_DIM),
            ak_s.reshape(1, db, nt, a_heads, 2 * HEAD_DIM), av_s.reshape(1, db, nt, a_heads, 2 * HEAD_DIM),
            bk_s.reshape(db, nt, b_heads, HEAD_DIM), bv_s.reshape(db, nt, b_heads, HEAD_DIM))
```

```python
import functools
import math

import jax
import jax.numpy as jnp
from jax import lax
from jax.experimental import pallas as pl
from jax.experimental.pallas import tpu as pltpu

HEAD_DIM = 64
CHUNK = 64
LEFT_CHUNKS = 8
PAD = LEFT_CHUNKS * CHUNK
REL_CLIP = 128
ROPE_THETA = 10000.0
EPS = 1e-6
NEG_INF = -1e30
LANES = 128
SCALE = HEAD_DIM ** -0.5

TOKEN_TILE = 512
FF_CHUNK = 1024
ATTN_TILE = 512
BAND_TILE = 256
BAND_Q_TILE = 128
VMEM_LIMIT = 56 * 1024 * 1024

F32 = jnp.float32
BF16 = jnp.bfloat16
NT_DIMS = (((1,), (1,)), ((), ()))


def _rms(x, g):
    ms = jnp.mean(x * x, axis=-1, keepdims=True)
    return x * lax.rsqrt(ms + EPS) * g


def _resident(shape):
    return pl.BlockSpec(shape, lambda *_: (0,) * len(shape), pipeline_mode=pl.Buffered(1))


def _params(*sem):
    return pltpu.CompilerParams(dimension_semantics=sem, vmem_limit_bytes=VMEM_LIMIT)


def _diff_lambda(lp, lam0):
    a = jnp.sum(lp[0:1] * lp[1:2], axis=-1, keepdims=True)
    b = jnp.sum(lp[2:3] * lp[3:4], axis=-1, keepdims=True)
    return jnp.exp(a) - jnp.exp(b) + lam0


def _head_norm(o, g, lam0):
    return _rms(o, g) * (1.0 - lam0)


def _proj_a_kernel(x_ref, g_ref, w_ref, cos_ref, sin_ref,
                   q_ref, k_ref, kb_ref, v_ref, vb_ref):
    d = x_ref.shape[1]
    xn = _rms(x_ref[...], g_ref[...]).astype(BF16)
    cos = cos_ref[...]
    sin = sin_ref[...]
    lane = lax.broadcasted_iota(jnp.int32, cos.shape, 1)
    first_half = (lane % HEAD_DIM) < (HEAD_DIM // 2)

    def rope(a):
        partner = jnp.where(first_half, pltpu.roll(a, LANES - HEAD_DIM // 2, 1),
                            pltpu.roll(a, HEAD_DIM // 2, 1))
        return a * cos + partner * sin

    q = jnp.dot(xn, w_ref[:, 0:d], preferred_element_type=F32)
    for c in range(d // LANES):
        sl = slice(c * LANES, (c + 1) * LANES)
        q_ref[:, sl] = (rope(q[:, sl]) * SCALE).astype(BF16)
    tm, heads = x_ref.shape[0], d // LANES
    k = jnp.dot(xn, w_ref[:, d:2 * d], preferred_element_type=F32)
    for c in range(heads):
        sl = slice(c * LANES, (c + 1) * LANES)
        kr = rope(k[:, sl])
        k_ref[pl.ds(c, tm, stride=heads), :] = kr
        kb_ref[:, sl] = kr.astype(BF16)
    v = jnp.dot(xn, w_ref[:, 2 * d:3 * d], preferred_element_type=F32)
    for c in range(heads):
        v_ref[pl.ds(c, tm, stride=heads), :] = v[:, c * LANES:(c + 1) * LANES]
    vb_ref[...] = v.astype(BF16)


def _proj_a(x, g, w, cos, sin):
    t, d = x.shape
    tm = TOKEN_TILE
    period = cos.shape[0] // tm
    heads = d // LANES
    row = pl.BlockSpec((tm, d), lambda i: (i, 0))
    native = pl.BlockSpec((tm * heads, LANES), lambda i: (i, 0))
    tab = pl.BlockSpec((tm, LANES), lambda i: (i % period, 0))
    return pl.pallas_call(
        _proj_a_kernel,
        grid=(t // tm,),
        in_specs=[row, _resident((1, d)), _resident((d, 3 * d)), tab, tab],
        out_specs=[row, native, row, native, row],
        out_shape=[jax.ShapeDtypeStruct((t, d), BF16), jax.ShapeDtypeStruct((t * heads, LANES), F32),
                   jax.ShapeDtypeStruct((t, d), BF16), jax.ShapeDtypeStruct((t * heads, LANES), F32),
                   jax.ShapeDtypeStruct((t, d), BF16)],
        compiler_params=_params("parallel"),
        name="proj_a",
    )(x, g, w, cos, sin)


def _attn_a_prompt_kernel(lp_ref, sg_ref, q_ref, k_ref, v_ref, o_ref, vt_ref, *, lam0):
    t = ATTN_TILE
    s_len = q_ref.shape[1]
    lam = _diff_lambda(lp_ref[...], lam0)
    sg = sg_ref[...]
    for c in range(s_len // t):
        sl = slice(c * t, (c + 1) * t)
        vt_ref[:, sl] = v_ref[0, sl, :].astype(F32).T.astype(BF16)

    lane = lax.broadcasted_iota(jnp.int32, (t, LANES), 1)
    key_chunk = lax.broadcasted_iota(jnp.int32, (t, 2 * t), 0) // CHUNK
    qry_chunk = (lax.broadcasted_iota(jnp.int32, (t, 2 * t), 1) % t) // CHUNK
    diag_mask = qry_chunk >= key_chunk

    def q_body(qi, _):
        q0 = pl.multiple_of(qi * t, t)
        q = q_ref[0, pl.ds(q0, t), :]
        zero = jnp.zeros_like(q)
        qz = jnp.concatenate([jnp.where(lane < HEAD_DIM, q, zero),
                              jnp.where(lane >= HEAD_DIM, q, zero)], axis=0)

        def block(j, carry, masked):
            m, l, acc = carry
            k0 = pl.multiple_of(j * t, t)
            k = k_ref[0, pl.ds(k0, t), :]
            s = lax.dot_general(k, qz, NT_DIMS, preferred_element_type=F32)
            if masked:
                s = jnp.where(diag_mask, s, NEG_INF)
            m_new = jnp.maximum(m, jnp.max(s, axis=0, keepdims=True))
            alpha = jnp.exp(m - m_new)
            p = jnp.exp(s - m_new)
            l = alpha * l + jnp.sum(p, axis=0, keepdims=True)
            acc = alpha * acc + jnp.dot(vt_ref[:, pl.ds(k0, t)], p.astype(BF16),
                                        preferred_element_type=F32)
            return m_new, l, acc

        init = (jnp.full((1, 2 * t), NEG_INF, F32), jnp.zeros((1, 2 * t), F32),
                jnp.zeros((LANES, 2 * t), F32))
        carry = lax.fori_loop(0, qi, lambda j, c: block(j, c, False), init)
        m, l, acc = block(qi, carry, True)
        acc = acc / l
        o = acc[:, 0:t] - lam * acc[:, t:2 * t]
        o = o * lax.rsqrt(jnp.mean(o * o, axis=0, keepdims=True) + EPS)
        o_ref[0, pl.ds(q0, t), :] = (o.T * sg * (1.0 - lam0)).astype(BF16)
        return 0

    lax.fori_loop(0, s_len // t, q_body, 0)


def _attn_a_prompt(lp, sg, q, k, v, lam0):
    b, s, d = q.shape
    blk = pl.BlockSpec((1, s, LANES), lambda bi, hi: (bi, 0, hi))
    return pl.pallas_call(
        functools.partial(_attn_a_prompt_kernel, lam0=lam0),
        grid=(b, d // LANES),
        in_specs=[_resident(lp.shape), _resident(sg.shape), blk, blk, blk],
        out_specs=blk,
        out_shape=jax.ShapeDtypeStruct((b, s, d), BF16),
        scratch_shapes=[pltpu.VMEM((LANES, s), BF16)],
        compiler_params=_params("parallel", "parallel"),
        name="attn_a_prompt",
    )(lp, sg, q, k, v)


def _attn_a_sample_kernel(lp_ref, sg_ref, q_ref, kn_ref, vn_ref, ck_ref, cv_ref, o_ref, *, lam0):
    nt, d = q_ref.shape[1], q_ref.shape[2]
    heads = d // LANES
    past = ck_ref.shape[1] // heads
    lam = _diff_lambda(lp_ref[...], lam0)
    sg = sg_ref[...]
    lane = lax.broadcasted_iota(jnp.int32, (nt, LANES), 1)

    for h in range(heads):
        sl = slice(h * LANES, (h + 1) * LANES)
        q = q_ref[0, :, sl]
        zero = jnp.zeros_like(q)
        qz = jnp.concatenate([jnp.where(lane < HEAD_DIM, q, zero),
                              jnp.where(lane >= HEAD_DIM, q, zero)], axis=0)
        kc = ck_ref[0, pl.ds(h, past, stride=heads), :].astype(BF16)
        vc = cv_ref[0, pl.ds(h, past, stride=heads), :].astype(BF16)
        kn = kn_ref[0, :, sl]
        vn = vn_ref[0, :, sl]
        sc = lax.dot_general(qz, kc, NT_DIMS, preferred_element_type=F32)
        sn = lax.dot_general(qz, kn, NT_DIMS, preferred_element_type=F32)
        m = jnp.maximum(jnp.max(sc, axis=-1, keepdims=True), jnp.max(sn, axis=-1, keepdims=True))
        ec = jnp.exp(sc - m)
        en = jnp.exp(sn - m)
        inv = 1.0 / (jnp.sum(ec, axis=-1, keepdims=True) + jnp.sum(en, axis=-1, keepdims=True))
        pc = ec * inv
        pn = en * inv
        pc = (pc[0:nt] - lam * pc[nt:2 * nt]).astype(BF16)
        pn = (pn[0:nt] - lam * pn[nt:2 * nt]).astype(BF16)
        o = (jnp.dot(pc, vc, preferred_element_type=F32)
             + jnp.dot(pn, vn, preferred_element_type=F32))
        o_ref[0, :, sl] = _head_norm(o, sg, lam0).astype(BF16)


def _attn_a_sample(lp, sg, q, kn, vn, ck, cv, lam0):
    b, nt, d = q.shape
    new = pl.BlockSpec((1, nt, d), lambda bi: (bi, 0, 0))
    old = pl.BlockSpec((1,) + ck.shape[1:], lambda bi: (bi, 0, 0))
    return pl.pallas_call(
        functools.partial(_attn_a_sample_kernel, lam0=lam0),
        grid=(b,),
        in_specs=[_resident(lp.shape), _resident(sg.shape), new, new, new, old, old],
        out_specs=new,
        out_shape=jax.ShapeDtypeStruct((b, nt, d), BF16),
        compiler_params=_params("parallel"),
        name="attn_a_sample",
    )(lp, sg, q, kn, vn, ck, cv)


def _post_kernel(*refs, final):
    if final:
        x_ref, o_ref, wo_ref, g_ref, w1_ref, w2_ref, gf_ref, out_ref = refs
    else:
        x_ref, o_ref, wo_ref, g_ref, w1_ref, w2_ref, out_ref = refs
    h = x_ref[...] + jnp.dot(o_ref[...], wo_ref[...], preferred_element_type=F32)
    hn = _rms(h, g_ref[...]).astype(BF16)
    d_ff = w1_ref.shape[1]
    mlp = None
    for c in range(d_ff // FF_CHUNK):
        sl = slice(c * FF_CHUNK, (c + 1) * FF_CHUNK)
        a = jnp.dot(hn, w1_ref[:, sl], preferred_element_type=F32)
        u = jnp.square(jnp.maximum(a, 0.0)).astype(BF16)
        part = jnp.dot(u, w2_ref[sl, :], preferred_element_type=F32)
        mlp = part if mlp is None else mlp + part
    h = h + mlp
    if final:
        h = _rms(h, gf_ref[...])
    out_ref[...] = h


def _post(x, o, wo, g, w1, w2, gf=None):
    t, d = x.shape
    tm = TOKEN_TILE
    final = gf is not None
    row = pl.BlockSpec((tm, d), lambda i: (i, 0))
    in_specs = [row, row, _resident(wo.shape), _resident(g.shape), _resident(w1.shape), _resident(w2.shape)]
    args = [x, o, wo, g, w1, w2]
    if final:
        in_specs.append(_resident(gf.shape))
        args.append(gf)
    return pl.pallas_call(
        functools.partial(_post_kernel, final=final),
        grid=(t // tm,),
        in_specs=in_specs,
        out_specs=row,
        out_shape=jax.ShapeDtypeStruct((t, d), F32),
        compiler_params=_params("parallel"),
        name="post_final" if final else "post",
    )(*args)


def _proj_b_kernel(x_ref, gkv_ref, wkv_ref, gq_ref, wq_ref,
                   q_ref, kb_ref, vb_ref, k_ref, v_ref, *, period):
    d = x_ref.shape[1]
    x = x_ref[...]
    xs = x * lax.rsqrt(jnp.mean(x * x, axis=-1, keepdims=True) + EPS)
    xkv = (xs * gkv_ref[...]).astype(BF16)
    xq = (xs * gq_ref[...]).astype(BF16)
    q_ref[...] = (jnp.dot(xq, wq_ref[...], preferred_element_type=F32) * SCALE).astype(BF16)
    k = jnp.dot(xkv, wkv_ref[:, 0:d], preferred_element_type=F32)
    v = jnp.dot(xkv, wkv_ref[:, d:2 * d], preferred_element_type=F32)
    kb_ref[...] = k.astype(BF16)
    vb_ref[...] = v.astype(BF16)

    @pl.when(pl.program_id(0) % period == period - 1)
    def _():
        k_ref[...] = k
        v_ref[...] = v


def _proj_b(x, gkv, wkv, gq, wq, period):
    t, d = x.shape
    tm = TOKEN_TILE
    row = pl.BlockSpec((tm, d), lambda i: (i, 0))
    keep = pl.BlockSpec((tm, d), lambda i: (i // period, 0))
    return pl.pallas_call(
        functools.partial(_proj_b_kernel, period=period),
        grid=(t // tm,),
        in_specs=[row, _resident(gkv.shape), _resident(wkv.shape), _resident(gq.shape), _resident(wq.shape)],
        out_specs=[row, row, row, keep, keep],
        out_shape=[jax.ShapeDtypeStruct((t, d), BF16), jax.ShapeDtypeStruct((t, d), BF16),
                   jax.ShapeDtypeStruct((t, d), BF16),
                   jax.ShapeDtypeStruct((t // period, d), F32), jax.ShapeDtypeStruct((t // period, d), F32)],
        compiler_params=_params("arbitrary"),
        name="proj_b",
    )(x, gkv, wkv, gq, wq)


def _toeplitz_bias(r_row, rows, width):
    rb = jnp.broadcast_to(r_row, (rows, r_row.shape[1]))
    return pltpu.roll(rb, 0, 1, stride=1, stride_axis=0)


def _band_prompt_kernel(r_ref, q_ref, k_ref, v_ref, o_ref, kpad, vtpad, bias):
    tq = BAND_TILE
    s_len = q_ref.shape[1]
    win = PAD + tq
    kpad[0:PAD, :] = jnp.zeros((PAD, LANES), BF16)
    vtpad[:, 0:PAD] = jnp.zeros((LANES, PAD), BF16)
    kpad[PAD:, :] = k_ref[0]
    for c in range(s_len // tq):
        vtpad[:, PAD + c * tq:PAD + (c + 1) * tq] = (
            v_ref[0, c * tq:(c + 1) * tq, :].astype(F32).T.astype(BF16))

    key = lax.broadcasted_iota(jnp.int32, (win, tq), 0)

    @pl.when(pl.program_id(1) == 0)
    def _():
        qry = lax.broadcasted_iota(jnp.int32, (win, tq), 1)
        dist = qry // CHUNK + LEFT_CHUNKS - key // CHUNK
        in_band = (dist >= 0) & (dist <= LEFT_CHUNKS)
        for hh in range(2):
            rb = jnp.broadcast_to(r_ref[0, hh:hh + 1, :], (win, win + tq))
            shifted = pltpu.roll(rb, 0, 1, stride=1, stride_axis=0)
            bias[:, hh * tq:(hh + 1) * tq] = jnp.where(in_band, shifted[:, win:win + tq], NEG_INF)

    lane = lax.broadcasted_iota(jnp.int32, (tq, LANES), 1)
    key2 = lax.broadcasted_iota(jnp.int32, (win, 2 * tq), 0)

    def q_body(qi, check_start):
        q0 = pl.multiple_of(qi * tq, tq)
        q = q_ref[0, pl.ds(q0, tq), :]
        zero = jnp.zeros_like(q)
        qz = jnp.concatenate([jnp.where(lane < HEAD_DIM, q, zero),
                              jnp.where(lane >= HEAD_DIM, q, zero)], axis=0)
        s = lax.dot_general(kpad[pl.ds(q0, win), :], qz, NT_DIMS,
                            preferred_element_type=F32) + bias[...]
        if check_start:
            s = jnp.where(key2 + q0 >= PAD, s, NEG_INF)
        m = jnp.max(s, axis=0, keepdims=True)
        e = jnp.exp(s - m)
        l = jnp.sum(e, axis=0, keepdims=True)
        o = jnp.dot(vtpad[:, pl.ds(q0, win)], e.astype(BF16), preferred_element_type=F32) / l
        o = jnp.concatenate([o[0:HEAD_DIM, 0:tq], o[HEAD_DIM:LANES, tq:2 * tq]], axis=0)
        o_ref[0, pl.ds(q0, tq), :] = o.T.astype(BF16)

    n_start = PAD // tq

    def start_body(qi, _):
        q_body(qi, True)
        return 0

    def rest_body(qi, _):
        q_body(qi, False)
        return 0

    lax.fori_loop(0, n_start, start_body, 0)
    lax.fori_loop(n_start, s_len // tq, rest_body, 0)


def _band_prompt(r, q, k, v):
    b, s, d = q.shape
    blk = pl.BlockSpec((1, s, LANES), lambda pi, bi: (bi, 0, pi))
    rblk = pl.BlockSpec((1, 2, r.shape[2]), lambda pi, bi: (pi, 0, 0))
    win = PAD + BAND_TILE
    return pl.pallas_call(
        _band_prompt_kernel,
        grid=(d // LANES, b),
        in_specs=[rblk, blk, blk, blk],
        out_specs=blk,
        out_shape=jax.ShapeDtypeStruct((b, s, d), BF16),
        scratch_shapes=[pltpu.VMEM((PAD + s, LANES), BF16), pltpu.VMEM((LANES, PAD + s), BF16),
                        pltpu.VMEM((win, 2 * BAND_TILE), F32)],
        compiler_params=_params("arbitrary", "arbitrary"),
        name="band_prompt",
    )(r, q, k, v)


def _band_sample_kernel(r_ref, q_ref, kn_ref, vn_ref, ck_ref, cv_ref, o_ref):
    nt, d = q_ref.shape[1], q_ref.shape[2]
    lb = ck_ref.shape[1]
    off = BAND_Q_TILE
    lane = lax.broadcasted_iota(jnp.int32, (nt, LANES), 1)
    for p in range(d // LANES):
        sl = slice(p * LANES, (p + 1) * LANES)
        q = q_ref[0, :, sl]
        zero = jnp.zeros_like(q)
        qz = jnp.concatenate([jnp.where(lane < HEAD_DIM, q, zero),
                              jnp.where(lane >= HEAD_DIM, q, zero)], axis=0)
        kc = ck_ref[0, :, sl]
        vc = cv_ref[0, :, sl]
        kn = kn_ref[0, :, sl]
        vn = vn_ref[0, :, sl]
        shifted = jnp.concatenate([_toeplitz_bias(r_ref[p, hh:hh + 1, :], nt, lb + nt)
                                   for hh in range(2)], axis=0)
        sc = lax.dot_general(qz, kc, NT_DIMS, preferred_element_type=F32) + shifted[:, off:off + lb]
        sn = (lax.dot_general(qz, kn, NT_DIMS, preferred_element_type=F32)
              + shifted[:, off + lb:off + lb + nt])
        m = jnp.maximum(jnp.max(sc, axis=-1, keepdims=True), jnp.max(sn, axis=-1, keepdims=True))
        ec = jnp.exp(sc - m)
        en = jnp.exp(sn - m)
        inv = 1.0 / (jnp.sum(ec, axis=-1, keepdims=True) + jnp.sum(en, axis=-1, keepdims=True))
        o = (jnp.dot((ec * inv).astype(BF16), vc, preferred_element_type=F32)
             + jnp.dot((en * inv).astype(BF16), vn, preferred_element_type=F32))
        o_ref[0, :, sl] = jnp.where(lane < HEAD_DIM, o[0:nt], o[nt:2 * nt]).astype(BF16)


def _band_sample(r, q, kn, vn, ck, cv):
    b, nt, d = q.shape
    lb = ck.shape[1]
    new = pl.BlockSpec((1, nt, d), lambda bi: (bi, 0, 0))
    old = pl.BlockSpec((1, lb, d), lambda bi: (bi, 0, 0))
    return pl.pallas_call(
        _band_sample_kernel,
        grid=(b,),
        in_specs=[_resident(r.shape), new, new, new, old, old],
        out_specs=new,
        out_shape=jax.ShapeDtypeStruct((b, nt, d), BF16),
        compiler_params=_params("parallel"),
        name="band_sample",
    )(r, q, kn, vn, ck, cv)


def _rope_tables(pos):
    half = HEAD_DIM // 2
    inv = 1.0 / (ROPE_THETA ** (jnp.arange(half, dtype=F32) / half))
    ang = pos.astype(F32)[:, None] * inv[None, :]
    cos, sin = jnp.cos(ang), jnp.sin(ang)
    reps = LANES // HEAD_DIM
    return (jnp.tile(jnp.concatenate([cos, cos], axis=-1), (1, reps)),
            jnp.tile(jnp.concatenate([-sin, sin], axis=-1), (1, reps)))


def _bias_rows(table):
    far = PAD + BAND_Q_TILE - REL_CLIP
    near = table[:, 2 * REL_CLIP:0:-1]
    rows = jnp.concatenate([jnp.broadcast_to(table[:, 2 * REL_CLIP:], (table.shape[0], far)), near], axis=1)
    return rows.reshape(table.shape[0] // 2, 2, rows.shape[1])


def _bias_rows_t(table):
    heads = table.shape[0]
    lo = BAND_TILE - REL_CLIP
    hi = PAD + 2 * BAND_TILE - lo - (2 * REL_CLIP + 1)
    rows = jnp.concatenate([jnp.broadcast_to(table[:, :1], (heads, lo)), table,
                            jnp.broadcast_to(table[:, 2 * REL_CLIP:], (heads, hi))], axis=1)
    return rows.reshape(heads // 2, 2, rows.shape[1])


def _lambda_init(layer):
    return 0.8 - 0.6 * math.exp(-0.3 * layer)


def kernel(x_prompt, x_sample, cache_a_k, cache_a_v, cache_b_k, cache_b_v, g_attn, w_a_qkv, a_lambda, a_subln, w_a_o, g_kv, w_kv, w_b_q, b_rel, w_b_o, g_mlp, w_ff1, w_ff2, g_final):
    nb, seq, d = x_prompt.shape
    db, nt, _ = x_sample.shape
    past = cache_a_k.shape[2]
    lb = cache_b_k.shape[1]
    a_heads = d // (2 * HEAD_DIM)
    b_heads = d // HEAD_DIM
    assert w_a_qkv.shape[0] == 1 and w_b_q.shape[0] == 1, "one differential and one band layer"
    assert (db * nt) % TOKEN_TILE == 0 and TOKEN_TILE % nt == 0 and seq % TOKEN_TILE == 0
    keep = min(PAD, seq)
    assert keep == TOKEN_TILE and lb == PAD

    lam0 = _lambda_init(0)
    w_qkv = w_a_qkv[0].astype(BF16)
    w_ao = w_a_o[0].astype(BF16)
    w_kvb = w_kv.astype(BF16)
    w_bq = w_b_q[0].astype(BF16)
    w_bo = w_b_o[0].astype(BF16)
    w1 = w_ff1.astype(BF16)
    w2 = w_ff2.astype(BF16)
    g_a = g_attn[0][None]
    g_b = g_attn[1][None]
    g_k = g_kv[None]
    g_m0 = g_mlp[0][None]
    g_m1 = g_mlp[1][None]
    g_f = g_final[None]
    lp = a_lambda[0]
    sg = a_subln[0][None]
    r = _bias_rows(b_rel[0])
    r_t = _bias_rows_t(b_rel[0])

    cos_p, sin_p = _rope_tables(jnp.arange(seq))
    cos_s, sin_s = _rope_tables(jnp.arange(past, past + nt))
    reps = TOKEN_TILE // nt
    cos_s, sin_s = jnp.tile(cos_s, (reps, 1)), jnp.tile(sin_s, (reps, 1))

    def trunk(x, cos, sin, attn_a, attn_b, period):
        t = x.shape[0]
        q, k, kb, v, vb = _proj_a(x, g_a, w_qkv, cos, sin)
        o = attn_a(q, kb, vb)
        h = _post(x, o, w_ao, g_m0, w1[0], w2[0])
        qb, kbb, vbb, k_sh, v_sh = _proj_b(h, g_k, w_kvb, g_b, w_bq, period)
        ob = attn_b(qb, kbb, vbb)
        y = _post(h, ob, w_bo, g_m1, w1[1], w2[1], g_f)
        return y, k, v, k_sh, v_sh

    def attn_a_p(q, kb, vb):
        sh = (nb, seq, d)
        return _attn_a_prompt(lp, sg, q.reshape(sh), kb.reshape(sh), vb.reshape(sh), lam0).reshape(nb * seq, d)

    def attn_b_p(q, kb, vb):
        sh = (nb, seq, d)
        return _band_prompt(r_t, q.reshape(sh), kb.reshape(sh), vb.reshape(sh)).reshape(nb * seq, d)

    ck_a = cache_a_k.reshape(db, past * a_heads, 2 * HEAD_DIM)
    cv_a = cache_a_v.reshape(db, past * a_heads, 2 * HEAD_DIM)
    ck_b = cache_b_k.reshape(db, lb, d).astype(BF16)
    cv_b = cache_b_v.reshape(db, lb, d).astype(BF16)

    def attn_a_s(q, kb, vb):
        sh = (db, nt, d)
        return _attn_a_sample(lp, sg, q.reshape(sh), kb.reshape(sh), vb.reshape(sh), ck_a, cv_a, lam0).reshape(db * nt, d)

    def attn_b_s(q, kb, vb):
        sh = (db, nt, d)
        return _band_sample(r, q.reshape(sh), kb.reshape(sh), vb.reshape(sh), ck_b, cv_b).reshape(db * nt, d)

    y_p, ak_p, av_p, bk_p, bv_p = trunk(x_prompt.reshape(nb * seq, d), cos_p, sin_p, attn_a_p, attn_b_p,
                                        seq // TOKEN_TILE)
    y_s, ak_s, av_s, bk_s, bv_s = trunk(x_sample.reshape(db * nt, d), cos_s, sin_s, attn_a_s, attn_b_s, 1)

    return (y_p.reshape(nb, seq, d), y_s.reshape(db, nt, d),
            ak_p.reshape(1, nb, seq, a_heads, 2 * HEAD_DIM), av_p.reshape(1, nb, seq, a_heads, 2 * HEAD_DIM),
            bk_p.reshape(nb, keep, b_heads, HEAD_DIM), bv_p.reshape(nb, keep, b_heads, HEAD_DIM),
            ak_s.reshape(1, db, nt, a_heads, 2 * HEAD_DIM), av_s.reshape(1, db, nt, a_heads, 2 * HEAD_DIM),
            bk_s.reshape(db, nt, b_heads, HEAD_DIM), bv_s.reshape(db, nt, b_heads, HEAD_DIM))
```

```python
import functools
import math

import jax
import jax.numpy as jnp
from jax import lax
from jax.experimental import pallas as pl
from jax.experimental.pallas import tpu as pltpu

HEAD_DIM = 64
CHUNK = 64
LEFT_CHUNKS = 8
PAD = LEFT_CHUNKS * CHUNK
REL_CLIP = 128
ROPE_THETA = 10000.0
EPS = 1e-6
NEG_INF = -1e30
LANES = 128
SCALE = HEAD_DIM ** -0.5

TOKEN_TILE = 512
FF_CHUNK = 1024
ATTN_TILE = 512
ATTN_GROUP = 256
BAND_TILE = 256
BAND_Q_TILE = 128
VMEM_LIMIT = 56 * 1024 * 1024

F32 = jnp.float32
BF16 = jnp.bfloat16
NT_DIMS = (((1,), (1,)), ((), ()))


def _rms(x, g):
    ms = jnp.mean(x * x, axis=-1, keepdims=True)
    return x * lax.rsqrt(ms + EPS) * g


def _resident(shape):
    return pl.BlockSpec(shape, lambda *_: (0,) * len(shape), pipeline_mode=pl.Buffered(1))


def _params(*sem):
    return pltpu.CompilerParams(dimension_semantics=sem, vmem_limit_bytes=VMEM_LIMIT)


def _diff_lambda(lp, lam0):
    a = jnp.sum(lp[0:1] * lp[1:2], axis=-1, keepdims=True)
    b = jnp.sum(lp[2:3] * lp[3:4], axis=-1, keepdims=True)
    return jnp.exp(a) - jnp.exp(b) + lam0


def _head_norm(o, g, lam0):
    return _rms(o, g) * (1.0 - lam0)


def _proj_a_kernel(x_ref, g_ref, w_ref, cos_ref, sin_ref,
                   q_ref, k_ref, kb_ref, v_ref, vb_ref):
    d = x_ref.shape[1]
    xn = _rms(x_ref[...], g_ref[...]).astype(BF16)
    cos = cos_ref[...]
    sin = sin_ref[...]
    lane = lax.broadcasted_iota(jnp.int32, cos.shape, 1)
    first_half = (lane % HEAD_DIM) < (HEAD_DIM // 2)

    def rope(a):
        partner = jnp.where(first_half, pltpu.roll(a, LANES - HEAD_DIM // 2, 1),
                            pltpu.roll(a, HEAD_DIM // 2, 1))
        return a * cos + partner * sin

    q = jnp.dot(xn, w_ref[:, 0:d], preferred_element_type=F32)
    for c in range(d // LANES):
        sl = slice(c * LANES, (c + 1) * LANES)
        q_ref[:, sl] = (rope(q[:, sl]) * SCALE).astype(BF16)
    tm, heads = x_ref.shape[0], d // LANES
    k = jnp.dot(xn, w_ref[:, d:2 * d], preferred_element_type=F32)
    for c in range(heads):
        sl = slice(c * LANES, (c + 1) * LANES)
        kr = rope(k[:, sl])
        k_ref[pl.ds(c, tm, stride=heads), :] = kr
        kb_ref[:, sl] = kr.astype(BF16)
    v = jnp.dot(xn, w_ref[:, 2 * d:3 * d], preferred_element_type=F32)
    for c in range(heads):
        v_ref[pl.ds(c, tm, stride=heads), :] = v[:, c * LANES:(c + 1) * LANES]
    vb_ref[...] = v.astype(BF16)


def _proj_a(x, g, w, cos, sin):
    t, d = x.shape
    tm = TOKEN_TILE
    period = cos.shape[0] // tm
    heads = d // LANES
    row = pl.BlockSpec((tm, d), lambda i: (i, 0))
    native = pl.BlockSpec((tm * heads, LANES), lambda i: (i, 0))
    tab = pl.BlockSpec((tm, LANES), lambda i: (i % period, 0))
    return pl.pallas_call(
        _proj_a_kernel,
        grid=(t // tm,),
        in_specs=[row, _resident((1, d)), _resident((d, 3 * d)), tab, tab],
        out_specs=[row, native, row, native, row],
        out_shape=[jax.ShapeDtypeStruct((t, d), BF16), jax.ShapeDtypeStruct((t * heads, LANES), F32),
                   jax.ShapeDtypeStruct((t, d), BF16), jax.ShapeDtypeStruct((t * heads, LANES), F32),
                   jax.ShapeDtypeStruct((t, d), BF16)],
        compiler_params=_params("parallel"),
        name="proj_a",
    )(x, g, w, cos, sin)


def _attn_a_prompt_kernel(lp_ref, sg_ref, q_ref, k_ref, v_ref, o_ref, vt_ref, *, lam0):
    t = ATTN_TILE
    grp = ATTN_GROUP
    s_len = q_ref.shape[1]
    lam = _diff_lambda(lp_ref[...], lam0)
    sg = sg_ref[...]
    for c in range(s_len // t):
        sl = slice(c * t, (c + 1) * t)
        vt_ref[:, sl] = v_ref[0, sl, :].astype(F32).T.astype(BF16)

    lane = lax.broadcasted_iota(jnp.int32, (t, LANES), 1)
    key_chunk = lax.broadcasted_iota(jnp.int32, (t, 2 * t), 0) // CHUNK
    qry_chunk = (lax.broadcasted_iota(jnp.int32, (t, 2 * t), 1) % t) // CHUNK
    diag_mask = qry_chunk >= key_chunk

    def scores(qi, j):
        q = q_ref[0, qi * t:(qi + 1) * t, :]
        zero = jnp.zeros_like(q)
        qz = jnp.concatenate([jnp.where(lane < HEAD_DIM, q, zero),
                              jnp.where(lane >= HEAD_DIM, q, zero)], axis=0)
        s = lax.dot_general(k_ref[0, j * t:(j + 1) * t, :], qz, NT_DIMS,
                            preferred_element_type=F32)
        return jnp.where(diag_mask, s, NEG_INF) if j == qi else s

    pairs = [(qi, j) for qi in range(s_len // t) for j in range(qi + 1)]
    s_next = scores(*pairs[0])
    for n, (qi, j) in enumerate(pairs):
        s = s_next
        if n + 1 < len(pairs):
            s_next = scores(*pairs[n + 1])
        if j == 0:
            m = jnp.full((1, 2 * t), NEG_INF, F32)
            l = jnp.zeros((1, 2 * t), F32)
            acc = jnp.zeros((LANES, 2 * t), F32)
        m_new = jnp.maximum(m, jnp.max(s, axis=0, keepdims=True))
        alpha = jnp.exp(m - m_new)
        p = jnp.exp(s - m_new)
        l = alpha * l + jnp.sum(p, axis=0, keepdims=True)
        acc = alpha * acc + jnp.dot(vt_ref[:, j * t:(j + 1) * t], p.astype(BF16),
                                    preferred_element_type=F32)
        m = m_new
        if j == qi:
            a = acc / l
            o = a[:, 0:t] - lam * a[:, t:2 * t]
            o = o * lax.rsqrt(jnp.mean(o * o, axis=0, keepdims=True) + EPS)
            o_ref[0, qi * t:(qi + 1) * t, :] = (o.T * sg * (1.0 - lam0)).astype(BF16)


def _attn_a_prompt(lp, sg, q, k, v, lam0):
    b, s, d = q.shape
    blk = pl.BlockSpec((1, s, LANES), lambda bi, hi: (bi, 0, hi))
    return pl.pallas_call(
        functools.partial(_attn_a_prompt_kernel, lam0=lam0),
        grid=(b, d // LANES),
        in_specs=[_resident(lp.shape), _resident(sg.shape), blk, blk, blk],
        out_specs=blk,
        out_shape=jax.ShapeDtypeStruct((b, s, d), BF16),
        scratch_shapes=[pltpu.VMEM((LANES, s), BF16)],
        compiler_params=_params("parallel", "parallel"),
        name="attn_a_prompt",
    )(lp, sg, q, k, v)


def _attn_a_sample_kernel(lp_ref, sg_ref, q_ref, kn_ref, vn_ref, ck_ref, cv_ref, o_ref, *, lam0):
    nt, d = q_ref.shape[1], q_ref.shape[2]
    heads = d // LANES
    past = ck_ref.shape[1] // heads
    lam = _diff_lambda(lp_ref[...], lam0)
    sg = sg_ref[...]
    lane = lax.broadcasted_iota(jnp.int32, (nt, LANES), 1)

    for h in range(heads):
        sl = slice(h * LANES, (h + 1) * LANES)
        q = q_ref[0, :, sl]
        zero = jnp.zeros_like(q)
        qz = jnp.concatenate([jnp.where(lane < HEAD_DIM, q, zero),
                              jnp.where(lane >= HEAD_DIM, q, zero)], axis=0)
        kc = ck_ref[0, pl.ds(h, past, stride=heads), :].astype(BF16)
        vc = cv_ref[0, pl.ds(h, past, stride=heads), :].astype(BF16)
        kn = kn_ref[0, :, sl]
        vn = vn_ref[0, :, sl]
        sc = lax.dot_general(qz, kc, NT_DIMS, preferred_element_type=F32)
        sn = lax.dot_general(qz, kn, NT_DIMS, preferred_element_type=F32)
        m = jnp.maximum(jnp.max(sc, axis=-1, keepdims=True), jnp.max(sn, axis=-1, keepdims=True))
        ec = jnp.exp(sc - m)
        en = jnp.exp(sn - m)
        inv = 1.0 / (jnp.sum(ec, axis=-1, keepdims=True) + jnp.sum(en, axis=-1, keepdims=True))
        pc = ec * inv
        pn = en * inv
        pc = (pc[0:nt] - lam * pc[nt:2 * nt]).astype(BF16)
        pn = (pn[0:nt] - lam * pn[nt:2 * nt]).astype(BF16)
        o = (jnp.dot(pc, vc, preferred_element_type=F32)
             + jnp.dot(pn, vn, preferred_element_type=F32))
        o_ref[0, :, sl] = _head_norm(o, sg, lam0).astype(BF16)


def _attn_a_sample(lp, sg, q, kn, vn, ck, cv, lam0):
    b, nt, d = q.shape
    new = pl.BlockSpec((1, nt, d), lambda bi: (bi, 0, 0))
    old = pl.BlockSpec((1,) + ck.shape[1:], lambda bi: (bi, 0, 0))
    return pl.pallas_call(
        functools.partial(_attn_a_sample_kernel, lam0=lam0),
        grid=(b,),
        in_specs=[_resident(lp.shape), _resident(sg.shape), new, new, new, old, old],
        out_specs=new,
        out_shape=jax.ShapeDtypeStruct((b, nt, d), BF16),
        compiler_params=_params("parallel"),
        name="attn_a_sample",
    )(lp, sg, q, kn, vn, ck, cv)


def _post_kernel(*refs, final):
    if final:
        x_ref, o_ref, wo_ref, g_ref, w1_ref, w2_ref, gf_ref, out_ref = refs
    else:
        x_ref, o_ref, wo_ref, g_ref, w1_ref, w2_ref, out_ref = refs
    h = x_ref[...] + jnp.dot(o_ref[...], wo_ref[...], preferred_element_type=F32)
    hn = _rms(h, g_ref[...]).astype(BF16)
    d_ff = w1_ref.shape[1]
    mlp = None
    for c in range(d_ff // FF_CHUNK):
        sl = slice(c * FF_CHUNK, (c + 1) * FF_CHUNK)
        a = jnp.dot(hn, w1_ref[:, sl], preferred_element_type=F32)
        u = jnp.square(jnp.maximum(a, 0.0)).astype(BF16)
        part = jnp.dot(u, w2_ref[sl, :], preferred_element_type=F32)
        mlp = part if mlp is None else mlp + part
    h = h + mlp
    if final:
        h = _rms(h, gf_ref[...])
    out_ref[...] = h


def _post(x, o, wo, g, w1, w2, gf=None):
    t, d = x.shape
    tm = TOKEN_TILE
    final = gf is not None
    row = pl.BlockSpec((tm, d), lambda i: (i, 0))
    in_specs = [row, row, _resident(wo.shape), _resident(g.shape), _resident(w1.shape), _resident(w2.shape)]
    args = [x, o, wo, g, w1, w2]
    if final:
        in_specs.append(_resident(gf.shape))
        args.append(gf)
    return pl.pallas_call(
        functools.partial(_post_kernel, final=final),
        grid=(t // tm,),
        in_specs=in_specs,
        out_specs=row,
        out_shape=jax.ShapeDtypeStruct((t, d), F32),
        compiler_params=_params("parallel"),
        name="post_final" if final else "post",
    )(*args)


def _proj_b_kernel(x_ref, gkv_ref, wkv_ref, gq_ref, wq_ref,
                   q_ref, kb_ref, vb_ref, k_ref, v_ref, *, period):
    d = x_ref.shape[1]
    x = x_ref[...]
    xs = x * lax.rsqrt(jnp.mean(x * x, axis=-1, keepdims=True) + EPS)
    xkv = (xs * gkv_ref[...]).astype(BF16)
    xq = (xs * gq_ref[...]).astype(BF16)
    q_ref[...] = (jnp.dot(xq, wq_ref[...], preferred_element_type=F32) * SCALE).astype(BF16)
    k = jnp.dot(xkv, wkv_ref[:, 0:d], preferred_element_type=F32)
    v = jnp.dot(xkv, wkv_ref[:, d:2 * d], preferred_element_type=F32)
    kb_ref[...] = k.astype(BF16)
    vb_ref[...] = v.astype(BF16)

    @pl.when(pl.program_id(0) % period == period - 1)
    def _():
        k_ref[...] = k
        v_ref[...] = v


def _proj_b(x, gkv, wkv, gq, wq, period):
    t, d = x.shape
    tm = TOKEN_TILE
    row = pl.BlockSpec((tm, d), lambda i: (i, 0))
    keep = pl.BlockSpec((tm, d), lambda i: (i // period, 0))
    return pl.pallas_call(
        functools.partial(_proj_b_kernel, period=period),
        grid=(t // tm,),
        in_specs=[row, _resident(gkv.shape), _resident(wkv.shape), _resident(gq.shape), _resident(wq.shape)],
        out_specs=[row, row, row, keep, keep],
        out_shape=[jax.ShapeDtypeStruct((t, d), BF16), jax.ShapeDtypeStruct((t, d), BF16),
                   jax.ShapeDtypeStruct((t, d), BF16),
                   jax.ShapeDtypeStruct((t // period, d), F32), jax.ShapeDtypeStruct((t // period, d), F32)],
        compiler_params=_params("arbitrary"),
        name="proj_b",
    )(x, gkv, wkv, gq, wq)


def _toeplitz_bias(r_row, rows, width):
    rb = jnp.broadcast_to(r_row, (rows, r_row.shape[1]))
    return pltpu.roll(rb, 0, 1, stride=1, stride_axis=0)


def _band_prompt_kernel(r_ref, q_ref, k_ref, v_ref, o_ref, vt_ref, bias):
    tq = BAND_TILE
    s_len = q_ref.shape[1]
    win = PAD + tq
    for c in range(s_len // tq):
        sl = slice(c * tq, (c + 1) * tq)
        vt_ref[:, sl] = v_ref[0, sl, :].astype(F32).T.astype(BF16)

    @pl.when(pl.program_id(1) == 0)
    def _():
        key = lax.broadcasted_iota(jnp.int32, (win, tq), 0)
        qry = lax.broadcasted_iota(jnp.int32, (win, tq), 1)
        dist = qry // CHUNK + LEFT_CHUNKS - key // CHUNK
        in_band = (dist >= 0) & (dist <= LEFT_CHUNKS)
        for hh in range(2):
            rb = jnp.broadcast_to(r_ref[0, hh:hh + 1, :], (win, win + tq))
            shifted = pltpu.roll(rb, 0, 1, stride=1, stride_axis=0)
            bias[:, hh * tq:(hh + 1) * tq] = jnp.where(in_band, shifted[:, win:win + tq], NEG_INF)

    lane = lax.broadcasted_iota(jnp.int32, (tq, LANES), 1)

    def window(qi):
        q0 = qi * tq
        return max(0, PAD - q0), max(0, q0 - PAD), q0 + tq

    def scores(qi):
        lo, k_lo, k_hi = window(qi)
        q = q_ref[0, qi * tq:(qi + 1) * tq, :]
        zero = jnp.zeros_like(q)
        qz = jnp.concatenate([jnp.where(lane < HEAD_DIM, q, zero),
                              jnp.where(lane >= HEAD_DIM, q, zero)], axis=0)
        return lax.dot_general(k_ref[0, k_lo:k_hi, :], qz, NT_DIMS,
                               preferred_element_type=F32) + bias[lo:win, :]

    n_tiles = s_len // tq
    s_next = scores(0)
    for qi in range(n_tiles):
        s = s_next
        if qi + 1 < n_tiles:
            s_next = scores(qi + 1)
        _, k_lo, k_hi = window(qi)
        m = jnp.max(s, axis=0, keepdims=True)
        e = jnp.exp(s - m)
        l = jnp.sum(e, axis=0, keepdims=True)
        o = jnp.dot(vt_ref[:, k_lo:k_hi], e.astype(BF16), preferred_element_type=F32) / l
        o = jnp.concatenate([o[0:HEAD_DIM, 0:tq], o[HEAD_DIM:LANES, tq:2 * tq]], axis=0)
        o_ref[0, qi * tq:(qi + 1) * tq, :] = o.T.astype(BF16)


def _band_prompt(r, q, k, v):
    b, s, d = q.shape
    blk = pl.BlockSpec((1, s, LANES), lambda pi, bi: (bi, 0, pi))
    rblk = pl.BlockSpec((1, 2, r.shape[2]), lambda pi, bi: (pi, 0, 0))
    win = PAD + BAND_TILE
    return pl.pallas_call(
        _band_prompt_kernel,
        grid=(d // LANES, b),
        in_specs=[rblk, blk, blk, blk],
        out_specs=blk,
        out_shape=jax.ShapeDtypeStruct((b, s, d), BF16),
        scratch_shapes=[pltpu.VMEM((LANES, s), BF16), pltpu.VMEM((win, 2 * BAND_TILE), F32)],
        compiler_params=_params("arbitrary", "arbitrary"),
        name="band_prompt",
    )(r, q, k, v)


def _band_sample_kernel(r_ref, q_ref, kn_ref, vn_ref, ck_ref, cv_ref, o_ref):
    nt, d = q_ref.shape[1], q_ref.shape[2]
    lb = ck_ref.shape[1]
    off = BAND_Q_TILE
    lane = lax.broadcasted_iota(jnp.int32, (nt, LANES), 1)
    for p in range(d // LANES):
        sl = slice(p * LANES, (p + 1) * LANES)
        q = q_ref[0, :, sl]
        zero = jnp.zeros_like(q)
        qz = jnp.concatenate([jnp.where(lane < HEAD_DIM, q, zero),
                              jnp.where(lane >= HEAD_DIM, q, zero)], axis=0)
        kc = ck_ref[0, :, sl]
        vc = cv_ref[0, :, sl]
        kn = kn_ref[0, :, sl]
        vn = vn_ref[0, :, sl]
        shifted = jnp.concatenate([_toeplitz_bias(r_ref[p, hh:hh + 1, :], nt, lb + nt)
                                   for hh in range(2)], axis=0)
        sc = lax.dot_general(qz, kc, NT_DIMS, preferred_element_type=F32) + shifted[:, off:off + lb]
        sn = (lax.dot_general(qz, kn, NT_DIMS, preferred_element_type=F32)
              + shifted[:, off + lb:off + lb + nt])
        m = jnp.maximum(jnp.max(sc, axis=-1, keepdims=True), jnp.max(sn, axis=-1, keepdims=True))
        ec = jnp.exp(sc - m)
        en = jnp.exp(sn - m)
        inv = 1.0 / (jnp.sum(ec, axis=-1, keepdims=True) + jnp.sum(en, axis=-1, keepdims=True))
        o = (jnp.dot((ec * inv).astype(BF16), vc, preferred_element_type=F32)
             + jnp.dot((en * inv).astype(BF16), vn, preferred_element_type=F32))
        o_ref[0, :, sl] = jnp.where(lane < HEAD_DIM, o[0:nt], o[nt:2 * nt]).astype(BF16)


def _band_sample(r, q, kn, vn, ck, cv):
    b, nt, d = q.shape
    lb = ck.shape[1]
    new = pl.BlockSpec((1, nt, d), lambda bi: (bi, 0, 0))
    old = pl.BlockSpec((1, lb, d), lambda bi: (bi, 0, 0))
    return pl.pallas_call(
        _band_sample_kernel,
        grid=(b,),
        in_specs=[_resident(r.shape), new, new, new, old, old],
        out_specs=new,
        out_shape=jax.ShapeDtypeStruct((b, nt, d), BF16),
        compiler_params=_params("parallel"),
        name="band_sample",
    )(r, q, kn, vn, ck, cv)


def _rope_tables(pos):
    half = HEAD_DIM // 2
    inv = 1.0 / (ROPE_THETA ** (jnp.arange(half, dtype=F32) / half))
    ang = pos.astype(F32)[:, None] * inv[None, :]
    cos, sin = jnp.cos(ang), jnp.sin(ang)
    reps = LANES // HEAD_DIM
    return (jnp.tile(jnp.concatenate([cos, cos], axis=-1), (1, reps)),
            jnp.tile(jnp.concatenate([-sin, sin], axis=-1), (1, reps)))


def _bias_rows(table):
    far = PAD + BAND_Q_TILE - REL_CLIP
    near = table[:, 2 * REL_CLIP:0:-1]
    rows = jnp.concatenate([jnp.broadcast_to(table[:, 2 * REL_CLIP:], (table.shape[0], far)), near], axis=1)
    return rows.reshape(table.shape[0] // 2, 2, rows.shape[1])


def _bias_rows_t(table):
    heads = table.shape[0]
    lo = BAND_TILE - REL_CLIP
    hi = PAD + 2 * BAND_TILE - lo - (2 * REL_CLIP + 1)
    rows = jnp.concatenate([jnp.broadcast_to(table[:, :1], (heads, lo)), table,
                            jnp.broadcast_to(table[:, 2 * REL_CLIP:], (heads, hi))], axis=1)
    return rows.reshape(heads // 2, 2, rows.shape[1])


def _lambda_init(layer):
    return 0.8 - 0.6 * math.exp(-0.3 * layer)


def kernel(x_prompt, x_sample, cache_a_k, cache_a_v, cache_b_k, cache_b_v, g_attn, w_a_qkv, a_lambda, a_subln, w_a_o, g_kv, w_kv, w_b_q, b_rel, w_b_o, g_mlp, w_ff1, w_ff2, g_final):
    nb, seq, d = x_prompt.shape
    db, nt, _ = x_sample.shape
    past = cache_a_k.shape[2]
    lb = cache_b_k.shape[1]
    a_heads = d // (2 * HEAD_DIM)
    b_heads = d // HEAD_DIM
    assert w_a_qkv.shape[0] == 1 and w_b_q.shape[0] == 1, "one differential and one band layer"
    assert (db * nt) % TOKEN_TILE == 0 and TOKEN_TILE % nt == 0 and seq % TOKEN_TILE == 0
    keep = min(PAD, seq)
    assert keep == TOKEN_TILE and lb == PAD

    lam0 = _lambda_init(0)
    w_qkv = w_a_qkv[0].astype(BF16)
    w_ao = w_a_o[0].astype(BF16)
    w_kvb = w_kv.astype(BF16)
    w_bq = w_b_q[0].astype(BF16)
    w_bo = w_b_o[0].astype(BF16)
    w1 = w_ff1.astype(BF16)
    w2 = w_ff2.astype(BF16)
    g_a = g_attn[0][None]
    g_b = g_attn[1][None]
    g_k = g_kv[None]
    g_m0 = g_mlp[0][None]
    g_m1 = g_mlp[1][None]
    g_f = g_final[None]
    lp = a_lambda[0]
    sg = a_subln[0][None]
    r = _bias_rows(b_rel[0])
    r_t = _bias_rows_t(b_rel[0])

    cos_p, sin_p = _rope_tables(jnp.arange(seq))
    cos_s, sin_s = _rope_tables(jnp.arange(past, past + nt))
    reps = TOKEN_TILE // nt
    cos_s, sin_s = jnp.tile(cos_s, (reps, 1)), jnp.tile(sin_s, (reps, 1))

    def trunk(x, cos, sin, attn_a, attn_b, period):
        t = x.shape[0]
        q, k, kb, v, vb = _proj_a(x, g_a, w_qkv, cos, sin)
        o = attn_a(q, kb, vb)
        h = _post(x, o, w_ao, g_m0, w1[0], w2[0])
        qb, kbb, vbb, k_sh, v_sh = _proj_b(h, g_k, w_kvb, g_b, w_bq, period)
        ob = attn_b(qb, kbb, vbb)
        y = _post(h, ob, w_bo, g_m1, w1[1], w2[1], g_f)
        return y, k, v, k_sh, v_sh

    def attn_a_p(q, kb, vb):
        sh = (nb, seq, d)
        return _attn_a_prompt(lp, sg, q.reshape(sh), kb.reshape(sh), vb.reshape(sh), lam0).reshape(nb * seq, d)

    def attn_b_p(q, kb, vb):
        sh = (nb, seq, d)
        return _band_prompt(r_t, q.reshape(sh), kb.reshape(sh), vb.reshape(sh)).reshape(nb * seq, d)

    ck_a = cache_a_k.reshape(db, past * a_heads, 2 * HEAD_DIM)
    cv_a = cache_a_v.reshape(db, past * a_heads, 2 * HEAD_DIM)
    ck_b = cache_b_k.reshape(db, lb, d).astype(BF16)
    cv_b = cache_b_v.reshape(db, lb, d).astype(BF16)

    def attn_a_s(q, kb, vb):
        sh = (db, nt, d)
        return _attn_a_sample(lp, sg, q.reshape(sh), kb.reshape(sh), vb.reshape(sh), ck_a, cv_a, lam0).reshape(db * nt, d)

    def attn_b_s(q, kb, vb):
        sh = (db, nt, d)
        return _band_sample(r, q.reshape(sh), kb.reshape(sh), vb.reshape(sh), ck_b, cv_b).reshape(db * nt, d)

    y_p, ak_p, av_p, bk_p, bv_p = trunk(x_prompt.reshape(nb * seq, d), cos_p, sin_p, attn_a_p, attn_b_p,
                                        seq // TOKEN_TILE)
    y_s, ak_s, av_s, bk_s, bv_s = trunk(x_sample.reshape(db * nt, d), cos_s, sin_s, attn_a_s, attn_b_s, 1)

    return (y_p.reshape(nb, seq, d), y_s.reshape(db, nt, d),
            ak_p.reshape(1, nb, seq, a_heads, 2 * HEAD_DIM), av_p.reshape(1, nb, seq, a_heads, 2 * HEAD_DIM),
            bk_p.reshape(nb, keep, b_heads, HEAD_DIM), bv_p.reshape(nb, keep, b_heads, HEAD_DIM),
            ak_s.reshape(1, db, nt, a_heads, 2 * HEAD_DIM), av_s.reshape(1, db, nt, a_heads, 2 * HEAD_DIM),
            bk_s.reshape(db, nt, b_heads, HEAD_DIM), bv_s.reshape(db, nt, b_heads, HEAD_DIM))
```

```python
import functools
import math

import jax
import jax.numpy as jnp
from jax import lax
from jax.experimental import pallas as pl
from jax.experimental.pallas import tpu as pltpu

HEAD_DIM = 64
CHUNK = 64
LEFT_CHUNKS = 8
PAD = LEFT_CHUNKS * CHUNK
REL_CLIP = 128
ROPE_THETA = 10000.0
EPS = 1e-6
NEG_INF = -1e30
LANES = 128
LOG2E = math.log2(math.e)
QSCALE = HEAD_DIM ** -0.5 * LOG2E

TOKEN_TILE = 512
FF_CHUNK = 1024
ATTN_TILE = 512
ATTN_GROUP = 256
BAND_TILE = 256
BAND_Q_TILE = 128
VMEM_LIMIT = 56 * 1024 * 1024

F32 = jnp.float32
BF16 = jnp.bfloat16
NT_DIMS = (((1,), (1,)), ((), ()))


def _rms(x, g):
    ms = jnp.mean(x * x, axis=-1, keepdims=True)
    return x * lax.rsqrt(ms + EPS) * g


def _resident(shape):
    return pl.BlockSpec(shape, lambda *_: (0,) * len(shape), pipeline_mode=pl.Buffered(1))


def _params(*sem):
    return pltpu.CompilerParams(dimension_semantics=sem, vmem_limit_bytes=VMEM_LIMIT)


def _diff_lambda(lp, lam0):
    a = jnp.sum(lp[0:1] * lp[1:2], axis=-1, keepdims=True)
    b = jnp.sum(lp[2:3] * lp[3:4], axis=-1, keepdims=True)
    return jnp.exp(a) - jnp.exp(b) + lam0


def _head_norm(o, g, lam0):
    return _rms(o, g) * (1.0 - lam0)


def _proj_a_kernel(x_ref, g_ref, w_ref, cos_ref, sin_ref,
                   q_ref, k_ref, kb_ref, v_ref, vb_ref):
    d = x_ref.shape[1]
    xn = _rms(x_ref[...], g_ref[...]).astype(BF16)
    cos = cos_ref[...]
    sin = sin_ref[...]
    lane = lax.broadcasted_iota(jnp.int32, cos.shape, 1)
    first_half = (lane % HEAD_DIM) < (HEAD_DIM // 2)

    def rope(a):
        partner = jnp.where(first_half, pltpu.roll(a, LANES - HEAD_DIM // 2, 1),
                            pltpu.roll(a, HEAD_DIM // 2, 1))
        return a * cos + partner * sin

    q = jnp.dot(xn, w_ref[:, 0:d], preferred_element_type=F32)
    for c in range(d // LANES):
        sl = slice(c * LANES, (c + 1) * LANES)
        q_ref[:, sl] = (rope(q[:, sl]) * QSCALE).astype(BF16)
    tm, heads = x_ref.shape[0], d // LANES
    k = jnp.dot(xn, w_ref[:, d:2 * d], preferred_element_type=F32)
    for c in range(heads):
        sl = slice(c * LANES, (c + 1) * LANES)
        kr = rope(k[:, sl])
        k_ref[pl.ds(c, tm, stride=heads), :] = kr
        kb_ref[:, sl] = kr.astype(BF16)
    v = jnp.dot(xn, w_ref[:, 2 * d:3 * d], preferred_element_type=F32)
    for c in range(heads):
        v_ref[pl.ds(c, tm, stride=heads), :] = v[:, c * LANES:(c + 1) * LANES]
    vb_ref[...] = v.astype(BF16)


def _proj_a(x, g, w, cos, sin):
    t, d = x.shape
    tm = TOKEN_TILE
    period = cos.shape[0] // tm
    heads = d // LANES
    row = pl.BlockSpec((tm, d), lambda i: (i, 0))
    native = pl.BlockSpec((tm * heads, LANES), lambda i: (i, 0))
    tab = pl.BlockSpec((tm, LANES), lambda i: (i % period, 0))
    return pl.pallas_call(
        _proj_a_kernel,
        grid=(t // tm,),
        in_specs=[row, _resident((1, d)), _resident((d, 3 * d)), tab, tab],
        out_specs=[row, native, row, native, row],
        out_shape=[jax.ShapeDtypeStruct((t, d), BF16), jax.ShapeDtypeStruct((t * heads, LANES), F32),
                   jax.ShapeDtypeStruct((t, d), BF16), jax.ShapeDtypeStruct((t * heads, LANES), F32),
                   jax.ShapeDtypeStruct((t, d), BF16)],
        compiler_params=_params("parallel"),
        name="proj_a",
    )(x, g, w, cos, sin)


def _attn_a_prompt_kernel(lp_ref, sg_ref, q_ref, k_ref, v_ref, o_ref, vt_ref, *, lam0):
    t = ATTN_TILE
    grp = ATTN_GROUP
    s_len = q_ref.shape[1]
    lam = _diff_lambda(lp_ref[...], lam0)
    sg = sg_ref[...]
    for c in range(s_len // t):
        sl = slice(c * t, (c + 1) * t)
        vt_ref[:, sl] = v_ref[0, sl, :].astype(F32).T.astype(BF16)

    lane = lax.broadcasted_iota(jnp.int32, (t, LANES), 1)
    key_chunk = lax.broadcasted_iota(jnp.int32, (t, 2 * t), 0) // CHUNK
    qry_chunk = (lax.broadcasted_iota(jnp.int32, (t, 2 * t), 1) % t) // CHUNK
    diag_mask = qry_chunk >= key_chunk

    def scores(qi, j):
        q = q_ref[0, qi * t:(qi + 1) * t, :]
        zero = jnp.zeros_like(q)
        qz = jnp.concatenate([jnp.where(lane < HEAD_DIM, q, zero),
                              jnp.where(lane >= HEAD_DIM, q, zero)], axis=0)
        s = lax.dot_general(k_ref[0, j * t:(j + 1) * t, :], qz, NT_DIMS,
                            preferred_element_type=F32)
        return jnp.where(diag_mask, s, NEG_INF) if j == qi else s

    pairs = [(qi, j) for qi in range(s_len // t) for j in range(qi + 1)]
    s_next = scores(*pairs[0])
    for n, (qi, j) in enumerate(pairs):
        s = s_next
        if n + 1 < len(pairs):
            s_next = scores(*pairs[n + 1])
        if j == 0:
            m = jnp.full((1, 2 * t), NEG_INF, F32)
            l = jnp.zeros((1, 2 * t), F32)
            acc = jnp.zeros((LANES, 2 * t), F32)
        m_new = jnp.maximum(m, jnp.max(s, axis=0, keepdims=True))
        alpha = jnp.exp2(m - m_new)
        p = jnp.exp2(s - m_new)
        l = alpha * l + jnp.sum(p, axis=0, keepdims=True)
        acc = alpha * acc + jnp.dot(vt_ref[:, j * t:(j + 1) * t], p.astype(BF16),
                                    preferred_element_type=F32)
        m = m_new
        if j == qi:
            a = acc / l
            o = a[:, 0:t] - lam * a[:, t:2 * t]
            o = o * lax.rsqrt(jnp.mean(o * o, axis=0, keepdims=True) + EPS)
            o_ref[0, qi * t:(qi + 1) * t, :] = (o.T * sg * (1.0 - lam0)).astype(BF16)


def _attn_a_prompt(lp, sg, q, k, v, lam0):
    b, s, d = q.shape
    blk = pl.BlockSpec((1, s, LANES), lambda bi, hi: (bi, 0, hi))
    return pl.pallas_call(
        functools.partial(_attn_a_prompt_kernel, lam0=lam0),
        grid=(b, d // LANES),
        in_specs=[_resident(lp.shape), _resident(sg.shape), blk, blk, blk],
        out_specs=blk,
        out_shape=jax.ShapeDtypeStruct((b, s, d), BF16),
        scratch_shapes=[pltpu.VMEM((LANES, s), BF16)],
        compiler_params=_params("parallel", "parallel"),
        name="attn_a_prompt",
    )(lp, sg, q, k, v)


def _attn_a_sample_kernel(lp_ref, sg_ref, q_ref, kn_ref, vn_ref, ck_ref, cv_ref, o_ref, *, lam0):
    nt, d = q_ref.shape[1], q_ref.shape[2]
    heads = d // LANES
    past = ck_ref.shape[1] // heads
    lam = _diff_lambda(lp_ref[...], lam0)
    sg = sg_ref[...]
    lane = lax.broadcasted_iota(jnp.int32, (nt, LANES), 1)

    def scores(h):
        sl = slice(h * LANES, (h + 1) * LANES)
        q = q_ref[0, :, sl]
        zero = jnp.zeros_like(q)
        qz = jnp.concatenate([jnp.where(lane < HEAD_DIM, q, zero),
                              jnp.where(lane >= HEAD_DIM, q, zero)], axis=0)
        kc = ck_ref[0, pl.ds(h, past, stride=heads), :].astype(BF16)
        return (lax.dot_general(qz, kc, NT_DIMS, preferred_element_type=F32),
                lax.dot_general(qz, kn_ref[0, :, sl], NT_DIMS, preferred_element_type=F32))

    def weights(sc, sn):
        m = jnp.maximum(jnp.max(sc, axis=-1, keepdims=True), jnp.max(sn, axis=-1, keepdims=True))
        ec = jnp.exp2(sc - m)
        en = jnp.exp2(sn - m)
        inv = 1.0 / (jnp.sum(ec, axis=-1, keepdims=True) + jnp.sum(en, axis=-1, keepdims=True))
        pc = ec * inv
        pn = en * inv
        return ((pc[0:nt] - lam * pc[nt:2 * nt]).astype(BF16),
                (pn[0:nt] - lam * pn[nt:2 * nt]).astype(BF16))

    def output(h, pc, pn):
        sl = slice(h * LANES, (h + 1) * LANES)
        vc = cv_ref[0, pl.ds(h, past, stride=heads), :].astype(BF16)
        o = (jnp.dot(pc, vc, preferred_element_type=F32)
             + jnp.dot(pn, vn_ref[0, :, sl], preferred_element_type=F32))
        o_ref[0, :, sl] = _head_norm(o, sg, lam0).astype(BF16)

    s_next = scores(0)
    p_prev = None
    for h in range(heads):
        s = s_next
        if h + 1 < heads:
            s_next = scores(h + 1)
        p = weights(*s)
        if p_prev is not None:
            output(h - 1, *p_prev)
        p_prev = p
    output(heads - 1, *p_prev)


def _attn_a_sample(lp, sg, q, kn, vn, ck, cv, lam0):
    b, nt, d = q.shape
    new = pl.BlockSpec((1, nt, d), lambda bi: (bi, 0, 0))
    old = pl.BlockSpec((1,) + ck.shape[1:], lambda bi: (bi, 0, 0))
    return pl.pallas_call(
        functools.partial(_attn_a_sample_kernel, lam0=lam0),
        grid=(b,),
        in_specs=[_resident(lp.shape), _resident(sg.shape), new, new, new, old, old],
        out_specs=new,
        out_shape=jax.ShapeDtypeStruct((b, nt, d), BF16),
        compiler_params=_params("parallel"),
        name="attn_a_sample",
    )(lp, sg, q, kn, vn, ck, cv)


def _post_kernel(*refs, final):
    if final:
        x_ref, o_ref, wo_ref, g_ref, w1_ref, w2_ref, gf_ref, out_ref = refs
    else:
        x_ref, o_ref, wo_ref, g_ref, w1_ref, w2_ref, out_ref = refs
    h = x_ref[...] + jnp.dot(o_ref[...], wo_ref[...], preferred_element_type=F32)
    hn = _rms(h, g_ref[...]).astype(BF16)
    d_ff = w1_ref.shape[1]
    mlp = None
    for c in range(d_ff // FF_CHUNK):
        sl = slice(c * FF_CHUNK, (c + 1) * FF_CHUNK)
        a = jnp.dot(hn, w1_ref[:, sl], preferred_element_type=F32)
        u = jnp.square(jnp.maximum(a, 0.0)).astype(BF16)
        part = jnp.dot(u, w2_ref[sl, :], preferred_element_type=F32)
        mlp = part if mlp is None else mlp + part
    h = h + mlp
    if final:
        h = _rms(h, gf_ref[...])
    out_ref[...] = h


def _post(x, o, wo, g, w1, w2, gf=None):
    t, d = x.shape
    tm = TOKEN_TILE
    final = gf is not None
    row = pl.BlockSpec((tm, d), lambda i: (i, 0))
    in_specs = [row, row, _resident(wo.shape), _resident(g.shape), _resident(w1.shape), _resident(w2.shape)]
    args = [x, o, wo, g, w1, w2]
    if final:
        in_specs.append(_resident(gf.shape))
        args.append(gf)
    return pl.pallas_call(
        functools.partial(_post_kernel, final=final),
        grid=(t // tm,),
        in_specs=in_specs,
        out_specs=row,
        out_shape=jax.ShapeDtypeStruct((t, d), F32),
        compiler_params=_params("parallel"),
        name="post_final" if final else "post",
    )(*args)


def _proj_b_kernel(x_ref, gkv_ref, wkv_ref, gq_ref, wq_ref,
                   q_ref, kb_ref, vb_ref, k_ref, v_ref, *, period):
    d = x_ref.shape[1]
    x = x_ref[...]
    xs = x * lax.rsqrt(jnp.mean(x * x, axis=-1, keepdims=True) + EPS)
    xkv = (xs * gkv_ref[...]).astype(BF16)
    xq = (xs * gq_ref[...]).astype(BF16)
    q_ref[...] = (jnp.dot(xq, wq_ref[...], preferred_element_type=F32) * QSCALE).astype(BF16)
    k = jnp.dot(xkv, wkv_ref[:, 0:d], preferred_element_type=F32)
    v = jnp.dot(xkv, wkv_ref[:, d:2 * d], preferred_element_type=F32)
    kb_ref[...] = k.astype(BF16)
    vb_ref[...] = v.astype(BF16)

    @pl.when(pl.program_id(0) % period == period - 1)
    def _():
        k_ref[...] = k
        v_ref[...] = v


def _proj_b(x, gkv, wkv, gq, wq, period):
    t, d = x.shape
    tm = TOKEN_TILE
    row = pl.BlockSpec((tm, d), lambda i: (i, 0))
    keep = pl.BlockSpec((tm, d), lambda i: (i // period, 0))
    return pl.pallas_call(
        functools.partial(_proj_b_kernel, period=period),
        grid=(t // tm,),
        in_specs=[row, _resident(gkv.shape), _resident(wkv.shape), _resident(gq.shape), _resident(wq.shape)],
        out_specs=[row, row, row, keep, keep],
        out_shape=[jax.ShapeDtypeStruct((t, d), BF16), jax.ShapeDtypeStruct((t, d), BF16),
                   jax.ShapeDtypeStruct((t, d), BF16),
                   jax.ShapeDtypeStruct((t // period, d), F32), jax.ShapeDtypeStruct((t // period, d), F32)],
        compiler_params=_params("arbitrary"),
        name="proj_b",
    )(x, gkv, wkv, gq, wq)


def _toeplitz_bias(r_row, rows, width):
    rb = jnp.broadcast_to(r_row * LOG2E, (rows, r_row.shape[1]))
    return pltpu.roll(rb, 0, 1, stride=1, stride_axis=0)


def _band_prompt_kernel(r_ref, q_ref, k_ref, v_ref, o_ref, vt_ref, bias):
    tq = BAND_TILE
    s_len = q_ref.shape[1]
    win = PAD + tq
    for c in range(s_len // tq):
        sl = slice(c * tq, (c + 1) * tq)
        vt_ref[:, sl] = v_ref[0, sl, :].astype(F32).T.astype(BF16)

    @pl.when(pl.program_id(1) == 0)
    def _():
        key = lax.broadcasted_iota(jnp.int32, (win, tq), 0)
        qry = lax.broadcasted_iota(jnp.int32, (win, tq), 1)
        dist = qry // CHUNK + LEFT_CHUNKS - key // CHUNK
        in_band = (dist >= 0) & (dist <= LEFT_CHUNKS)
        for hh in range(2):
            rb = jnp.broadcast_to(r_ref[0, hh:hh + 1, :] * LOG2E, (win, win + tq))
            shifted = pltpu.roll(rb, 0, 1, stride=1, stride_axis=0)
            bias[:, hh * tq:(hh + 1) * tq] = jnp.where(in_band, shifted[:, win:win + tq], NEG_INF)

    lane = lax.broadcasted_iota(jnp.int32, (tq, LANES), 1)

    def window(qi):
        q0 = qi * tq
        return max(0, PAD - q0), max(0, q0 - PAD), q0 + tq

    def scores(qi):
        lo, k_lo, k_hi = window(qi)
        q = q_ref[0, qi * tq:(qi + 1) * tq, :]
        zero = jnp.zeros_like(q)
        qz = jnp.concatenate([jnp.where(lane < HEAD_DIM, q, zero),
                              jnp.where(lane >= HEAD_DIM, q, zero)], axis=0)
        return lax.dot_general(k_ref[0, k_lo:k_hi, :], qz, NT_DIMS,
                               preferred_element_type=F32) + bias[lo:win, :]

    n_tiles = s_len // tq
    s_next = scores(0)
    for qi in range(n_tiles):
        s = s_next
        if qi + 1 < n_tiles:
            s_next = scores(qi + 1)
        _, k_lo, k_hi = window(qi)
        m = jnp.max(s, axis=0, keepdims=True)
        e = jnp.exp2(s - m)
        l = jnp.sum(e, axis=0, keepdims=True)
        o = jnp.dot(vt_ref[:, k_lo:k_hi], e.astype(BF16), preferred_element_type=F32) / l
        o = jnp.concatenate([o[0:HEAD_DIM, 0:tq], o[HEAD_DIM:LANES, tq:2 * tq]], axis=0)
        o_ref[0, qi * tq:(qi + 1) * tq, :] = o.T.astype(BF16)


def _band_prompt(r, q, k, v):
    b, s, d = q.shape
    blk = pl.BlockSpec((1, s, LANES), lambda pi, bi: (bi, 0, pi))
    rblk = pl.BlockSpec((1, 2, r.shape[2]), lambda pi, bi: (pi, 0, 0))
    win = PAD + BAND_TILE
    return pl.pallas_call(
        _band_prompt_kernel,
        grid=(d // LANES, b),
        in_specs=[rblk, blk, blk, blk],
        out_specs=blk,
        out_shape=jax.ShapeDtypeStruct((b, s, d), BF16),
        scratch_shapes=[pltpu.VMEM((LANES, s), BF16), pltpu.VMEM((win, 2 * BAND_TILE), F32)],
        compiler_params=_params("arbitrary", "arbitrary"),
        name="band_prompt",
    )(r, q, k, v)


def _band_sample_kernel(r_ref, q_ref, kn_ref, vn_ref, ck_ref, cv_ref, o_ref):
    nt, d = q_ref.shape[1], q_ref.shape[2]
    lb = ck_ref.shape[1]
    off = BAND_Q_TILE
    lane = lax.broadcasted_iota(jnp.int32, (nt, LANES), 1)
    pairs = d // LANES

    def scores(p):
        sl = slice(p * LANES, (p + 1) * LANES)
        q = q_ref[0, :, sl]
        zero = jnp.zeros_like(q)
        qz = jnp.concatenate([jnp.where(lane < HEAD_DIM, q, zero),
                              jnp.where(lane >= HEAD_DIM, q, zero)], axis=0)
        shifted = jnp.concatenate([_toeplitz_bias(r_ref[p, hh:hh + 1, :], nt, lb + nt)
                                   for hh in range(2)], axis=0)
        sc = (lax.dot_general(qz, ck_ref[0, :, sl].astype(BF16), NT_DIMS, preferred_element_type=F32)
              + shifted[:, off:off + lb])
        sn = (lax.dot_general(qz, kn_ref[0, :, sl], NT_DIMS, preferred_element_type=F32)
              + shifted[:, off + lb:off + lb + nt])
        return sc, sn

    def weights(sc, sn):
        m = jnp.maximum(jnp.max(sc, axis=-1, keepdims=True), jnp.max(sn, axis=-1, keepdims=True))
        ec = jnp.exp2(sc - m)
        en = jnp.exp2(sn - m)
        inv = 1.0 / (jnp.sum(ec, axis=-1, keepdims=True) + jnp.sum(en, axis=-1, keepdims=True))
        return (ec * inv).astype(BF16), (en * inv).astype(BF16)

    def output(p, pc, pn):
        sl = slice(p * LANES, (p + 1) * LANES)
        o = (jnp.dot(pc, cv_ref[0, :, sl].astype(BF16), preferred_element_type=F32)
             + jnp.dot(pn, vn_ref[0, :, sl], preferred_element_type=F32))
        o_ref[0, :, sl] = jnp.where(lane < HEAD_DIM, o[0:nt], o[nt:2 * nt]).astype(BF16)

    s_next = scores(0)
    w_prev = None
    for p in range(pairs):
        s = s_next
        if p + 1 < pairs:
            s_next = scores(p + 1)
        w = weights(*s)
        if w_prev is not None:
            output(p - 1, *w_prev)
        w_prev = w
    output(pairs - 1, *w_prev)


def _band_sample(r, q, kn, vn, ck, cv):
    b, nt, d = q.shape
    lb = ck.shape[1]
    new = pl.BlockSpec((1, nt, d), lambda bi: (bi, 0, 0))
    old = pl.BlockSpec((1, lb, d), lambda bi: (bi, 0, 0))
    return pl.pallas_call(
        _band_sample_kernel,
        grid=(b,),
        in_specs=[_resident(r.shape), new, new, new, old, old],
        out_specs=new,
        out_shape=jax.ShapeDtypeStruct((b, nt, d), BF16),
        compiler_params=_params("parallel"),
        name="band_sample",
    )(r, q, kn, vn, ck, cv)


def _rope_tables(pos):
    half = HEAD_DIM // 2
    inv = 1.0 / (ROPE_THETA ** (jnp.arange(half, dtype=F32) / half))
    ang = pos.astype(F32)[:, None] * inv[None, :]
    cos, sin = jnp.cos(ang), jnp.sin(ang)
    reps = LANES // HEAD_DIM
    return (jnp.tile(jnp.concatenate([cos, cos], axis=-1), (1, reps)),
            jnp.tile(jnp.concatenate([-sin, sin], axis=-1), (1, reps)))


def _bias_rows(table):
    far = PAD + BAND_Q_TILE - REL_CLIP
    near = table[:, 2 * REL_CLIP:0:-1]
    rows = jnp.concatenate([jnp.broadcast_to(table[:, 2 * REL_CLIP:], (table.shape[0], far)), near], axis=1)
    return rows.reshape(table.shape[0] // 2, 2, rows.shape[1])


def _bias_rows_t(table):
    heads = table.shape[0]
    lo = BAND_TILE - REL_CLIP
    hi = PAD + 2 * BAND_TILE - lo - (2 * REL_CLIP + 1)
    rows = jnp.concatenate([jnp.broadcast_to(table[:, :1], (heads, lo)), table,
                            jnp.broadcast_to(table[:, 2 * REL_CLIP:], (heads, hi))], axis=1)
    return rows.reshape(heads // 2, 2, rows.shape[1])


def _lambda_init(layer):
    return 0.8 - 0.6 * math.exp(-0.3 * layer)


def kernel(x_prompt, x_sample, cache_a_k, cache_a_v, cache_b_k, cache_b_v, g_attn, w_a_qkv, a_lambda, a_subln, w_a_o, g_kv, w_kv, w_b_q, b_rel, w_b_o, g_mlp, w_ff1, w_ff2, g_final):
    nb, seq, d = x_prompt.shape
    db, nt, _ = x_sample.shape
    past = cache_a_k.shape[2]
    lb = cache_b_k.shape[1]
    a_heads = d // (2 * HEAD_DIM)
    b_heads = d // HEAD_DIM
    assert w_a_qkv.shape[0] == 1 and w_b_q.shape[0] == 1, "one differential and one band layer"
    assert (db * nt) % TOKEN_TILE == 0 and TOKEN_TILE % nt == 0 and seq % TOKEN_TILE == 0
    keep = min(PAD, seq)
    assert keep == TOKEN_TILE and lb == PAD

    lam0 = _lambda_init(0)
    w_qkv = w_a_qkv[0].astype(BF16)
    w_ao = w_a_o[0].astype(BF16)
    w_kvb = w_kv.astype(BF16)
    w_bq = w_b_q[0].astype(BF16)
    w_bo = w_b_o[0].astype(BF16)
    w1 = w_ff1.astype(BF16)
    w2 = w_ff2.astype(BF16)
    g_a = g_attn[0][None]
    g_b = g_attn[1][None]
    g_k = g_kv[None]
    g_m0 = g_mlp[0][None]
    g_m1 = g_mlp[1][None]
    g_f = g_final[None]
    lp = a_lambda[0]
    sg = a_subln[0][None]
    r = _bias_rows(b_rel[0])
    r_t = _bias_rows_t(b_rel[0])

    cos_p, sin_p = _rope_tables(jnp.arange(seq))
    cos_s, sin_s = _rope_tables(jnp.arange(past, past + nt))
    reps = TOKEN_TILE // nt
    cos_s, sin_s = jnp.tile(cos_s, (reps, 1)), jnp.tile(sin_s, (reps, 1))

    def trunk(x, cos, sin, attn_a, attn_b, period):
        t = x.shape[0]
        q, k, kb, v, vb = _proj_a(x, g_a, w_qkv, cos, sin)
        o = attn_a(q, kb, vb)
        h = _post(x, o, w_ao, g_m0, w1[0], w2[0])
        qb, kbb, vbb, k_sh, v_sh = _proj_b(h, g_k, w_kvb, g_b, w_bq, period)
        ob = attn_b(qb, kbb, vbb)
        y = _post(h, ob, w_bo, g_m1, w1[1], w2[1], g_f)
        return y, k, v, k_sh, v_sh

    def attn_a_p(q, kb, vb):
        sh = (nb, seq, d)
        return _attn_a_prompt(lp, sg, q.reshape(sh), kb.reshape(sh), vb.reshape(sh), lam0).reshape(nb * seq, d)

    def attn_b_p(q, kb, vb):
        sh = (nb, seq, d)
        return _band_prompt(r_t, q.reshape(sh), kb.reshape(sh), vb.reshape(sh)).reshape(nb * seq, d)

    ck_a = cache_a_k.reshape(db, past * a_heads, 2 * HEAD_DIM)
    cv_a = cache_a_v.reshape(db, past * a_heads, 2 * HEAD_DIM)
    ck_b = cache_b_k.reshape(db, lb, d)
    cv_b = cache_b_v.reshape(db, lb, d)

    def attn_a_s(q, kb, vb):
        sh = (db, nt, d)
        return _attn_a_sample(lp, sg, q.reshape(sh), kb.reshape(sh), vb.reshape(sh), ck_a, cv_a, lam0).reshape(db * nt, d)

    def attn_b_s(q, kb, vb):
        sh = (db, nt, d)
        return _band_sample(r, q.reshape(sh), kb.reshape(sh), vb.reshape(sh), ck_b, cv_b).reshape(db * nt, d)

    y_p, ak_p, av_p, bk_p, bv_p = trunk(x_prompt.reshape(nb * seq, d), cos_p, sin_p, attn_a_p, attn_b_p,
                                        seq // TOKEN_TILE)
    y_s, ak_s, av_s, bk_s, bv_s = trunk(x_sample.reshape(db * nt, d), cos_s, sin_s, attn_a_s, attn_b_s, 1)

    return (y_p.reshape(nb, seq, d), y_s.reshape(db, nt, d),
            ak_p.reshape(1, nb, seq, a_heads, 2 * HEAD_DIM), av_p.reshape(1, nb, seq, a_heads, 2 * HEAD_DIM),
            bk_p.reshape(nb, keep, b_heads, HEAD_DIM), bv_p.reshape(nb, keep, b_heads, HEAD_DIM),
            ak_s.reshape(1, db, nt, a_heads, 2 * HEAD_DIM), av_s.reshape(1, db, nt, a_heads, 2 * HEAD_DIM),
            bk_s.reshape(db, nt, b_heads, HEAD_DIM), bv_s.reshape(db, nt, b_heads, HEAD_DIM))
```

```python
import functools
import math

import jax
import jax.numpy as jnp
from jax import lax
from jax.experimental import pallas as pl
from jax.experimental.pallas import tpu as pltpu

HEAD_DIM = 64
CHUNK = 64
LEFT_CHUNKS = 8
PAD = LEFT_CHUNKS * CHUNK
REL_CLIP = 128
ROPE_THETA = 10000.0
EPS = 1e-6
NEG_INF = -1e30
LANES = 128
LOG2E = math.log2(math.e)
QSCALE = HEAD_DIM ** -0.5 * LOG2E

TOKEN_TILE = 512
FF_CHUNK = 1024
ATTN_TILE = 512
ATTN_GROUP = 256
BAND_TILE = 256
BAND_Q_TILE = 128
VMEM_LIMIT = 56 * 1024 * 1024

F32 = jnp.float32
BF16 = jnp.bfloat16
NT_DIMS = (((1,), (1,)), ((), ()))


def _rms(x, g):
    ms = jnp.mean(x * x, axis=-1, keepdims=True)
    return x * lax.rsqrt(ms + EPS) * g


def _resident(shape):
    return pl.BlockSpec(shape, lambda *_: (0,) * len(shape), pipeline_mode=pl.Buffered(1))


def _params(*sem):
    return pltpu.CompilerParams(dimension_semantics=sem, vmem_limit_bytes=VMEM_LIMIT)


def _diff_lambda(lp, lam0):
    a = jnp.sum(lp[0:1] * lp[1:2], axis=-1, keepdims=True)
    b = jnp.sum(lp[2:3] * lp[3:4], axis=-1, keepdims=True)
    return jnp.exp(a) - jnp.exp(b) + lam0


def _head_norm(o, g, lam0):
    return _rms(o, g) * (1.0 - lam0)


def _proj_a_kernel(x_ref, g_ref, w_ref, cos_ref, sin_ref,
                   q_ref, k_ref, kb_ref, v_ref, vb_ref):
    d = x_ref.shape[1]
    xn = _rms(x_ref[...], g_ref[...]).astype(BF16)
    cos = cos_ref[...]
    sin = sin_ref[...]
    lane = lax.broadcasted_iota(jnp.int32, cos.shape, 1)
    first_half = (lane % HEAD_DIM) < (HEAD_DIM // 2)

    def rope(a):
        partner = jnp.where(first_half, pltpu.roll(a, LANES - HEAD_DIM // 2, 1),
                            pltpu.roll(a, HEAD_DIM // 2, 1))
        return a * cos + partner * sin

    q = jnp.dot(xn, w_ref[:, 0:d], preferred_element_type=F32)
    for c in range(d // LANES):
        sl = slice(c * LANES, (c + 1) * LANES)
        q_ref[:, sl] = (rope(q[:, sl]) * QSCALE).astype(BF16)
    tm, heads = x_ref.shape[0], d // LANES
    k = jnp.dot(xn, w_ref[:, d:2 * d], preferred_element_type=F32)
    for c in range(heads):
        sl = slice(c * LANES, (c + 1) * LANES)
        kr = rope(k[:, sl])
        k_ref[pl.ds(c, tm, stride=heads), :] = kr
        kb_ref[:, sl] = kr.astype(BF16)
    v = jnp.dot(xn, w_ref[:, 2 * d:3 * d], preferred_element_type=F32)
    for c in range(heads):
        v_ref[pl.ds(c, tm, stride=heads), :] = v[:, c * LANES:(c + 1) * LANES]
    vb_ref[...] = v.astype(BF16)


def _proj_a(x, g, w, cos, sin):
    t, d = x.shape
    tm = TOKEN_TILE
    period = cos.shape[0] // tm
    heads = d // LANES
    row = pl.BlockSpec((tm, d), lambda i: (i, 0))
    native = pl.BlockSpec((tm * heads, LANES), lambda i: (i, 0))
    tab = pl.BlockSpec((tm, LANES), lambda i: (i % period, 0))
    return pl.pallas_call(
        _proj_a_kernel,
        grid=(t // tm,),
        in_specs=[row, _resident((1, d)), _resident((d, 3 * d)), tab, tab],
        out_specs=[row, native, row, native, row],
        out_shape=[jax.ShapeDtypeStruct((t, d), BF16), jax.ShapeDtypeStruct((t * heads, LANES), F32),
                   jax.ShapeDtypeStruct((t, d), BF16), jax.ShapeDtypeStruct((t * heads, LANES), F32),
                   jax.ShapeDtypeStruct((t, d), BF16)],
        compiler_params=_params("parallel"),
        name="proj_a",
    )(x, g, w, cos, sin)


def _attn_a_prompt_kernel(lp_ref, sg_ref, q_ref, k_ref, v_ref, o_ref, vt_ref, *, lam0):
    t = ATTN_TILE
    grp = ATTN_GROUP
    s_len = q_ref.shape[1]
    lam = _diff_lambda(lp_ref[...], lam0)
    sg = sg_ref[...]
    for c in range(s_len // t):
        sl = slice(c * t, (c + 1) * t)
        vt_ref[:, sl] = v_ref[0, sl, :].astype(F32).T.astype(BF16)

    lane = lax.broadcasted_iota(jnp.int32, (t, LANES), 1)
    key_chunk = lax.broadcasted_iota(jnp.int32, (t, 2 * t), 0) // CHUNK
    qry_chunk = (lax.broadcasted_iota(jnp.int32, (t, 2 * t), 1) % t) // CHUNK
    diag_mask = qry_chunk >= key_chunk

    def scores(qi, j):
        q = q_ref[0, qi * t:(qi + 1) * t, :]
        zero = jnp.zeros_like(q)
        qz = jnp.concatenate([jnp.where(lane < HEAD_DIM, q, zero),
                              jnp.where(lane >= HEAD_DIM, q, zero)], axis=0)
        s = lax.dot_general(k_ref[0, j * t:(j + 1) * t, :], qz, NT_DIMS,
                            preferred_element_type=F32)
        return jnp.where(diag_mask, s, NEG_INF) if j == qi else s

    pairs = [(qi, j) for qi in range(s_len // t) for j in range(qi + 1)]
    s_next = scores(*pairs[0])
    for n, (qi, j) in enumerate(pairs):
        s = s_next
        if n + 1 < len(pairs):
            s_next = scores(*pairs[n + 1])
        if j == 0:
            m = jnp.full((1, 2 * t), NEG_INF, F32)
            l = jnp.zeros((1, 2 * t), F32)
            acc = jnp.zeros((LANES, 2 * t), F32)
        m_new = jnp.maximum(m, jnp.max(s, axis=0, keepdims=True))
        alpha = jnp.exp2(m - m_new)
        p = jnp.exp2(s - m_new)
        l = alpha * l + jnp.sum(p, axis=0, keepdims=True)
        acc = alpha * acc + jnp.dot(vt_ref[:, j * t:(j + 1) * t], p.astype(BF16),
                                    preferred_element_type=F32)
        m = m_new
        if j == qi:
            a = acc / l
            o = a[:, 0:t] - lam * a[:, t:2 * t]
            o = o * lax.rsqrt(jnp.mean(o * o, axis=0, keepdims=True) + EPS)
            o_ref[0, qi * t:(qi + 1) * t, :] = (o.T * sg * (1.0 - lam0)).astype(BF16)


def _attn_a_prompt(lp, sg, q, k, v, lam0):
    b, s, d = q.shape
    blk = pl.BlockSpec((1, s, LANES), lambda bi, hi: (bi, 0, hi))
    return pl.pallas_call(
        functools.partial(_attn_a_prompt_kernel, lam0=lam0),
        grid=(b, d // LANES),
        in_specs=[_resident(lp.shape), _resident(sg.shape), blk, blk, blk],
        out_specs=blk,
        out_shape=jax.ShapeDtypeStruct((b, s, d), BF16),
        scratch_shapes=[pltpu.VMEM((LANES, s), BF16)],
        compiler_params=_params("parallel", "parallel"),
        name="attn_a_prompt",
    )(lp, sg, q, k, v)


def _attn_a_sample_kernel(lp_ref, sg_ref, q_ref, kn_ref, vn_ref, ck_ref, cv_ref, o_ref, *, lam0):
    nt, d = q_ref.shape[1], q_ref.shape[2]
    heads = d // LANES
    past = ck_ref.shape[1] // heads
    lam = _diff_lambda(lp_ref[...], lam0)
    sg = sg_ref[...]
    lane = lax.broadcasted_iota(jnp.int32, (nt, LANES), 1)

    def scores(h):
        sl = slice(h * LANES, (h + 1) * LANES)
        q = q_ref[0, :, sl]
        zero = jnp.zeros_like(q)
        qz = jnp.concatenate([jnp.where(lane < HEAD_DIM, q, zero),
                              jnp.where(lane >= HEAD_DIM, q, zero)], axis=0)
        kc = ck_ref[0, pl.ds(h, past, stride=heads), :].astype(BF16)
        return (lax.dot_general(qz, kc, NT_DIMS, preferred_element_type=F32),
                lax.dot_general(qz, kn_ref[0, :, sl], NT_DIMS, preferred_element_type=F32))

    def weights(sc, sn):
        m = jnp.maximum(jnp.max(sc, axis=-1, keepdims=True), jnp.max(sn, axis=-1, keepdims=True))
        ec = jnp.exp2(sc - m)
        en = jnp.exp2(sn - m)
        inv = 1.0 / (jnp.sum(ec, axis=-1, keepdims=True) + jnp.sum(en, axis=-1, keepdims=True))
        pc = ec * inv
        pn = en * inv
        return ((pc[0:nt] - lam * pc[nt:2 * nt]).astype(BF16),
                (pn[0:nt] - lam * pn[nt:2 * nt]).astype(BF16))

    def output(h, pc, pn):
        sl = slice(h * LANES, (h + 1) * LANES)
        vc = cv_ref[0, pl.ds(h, past, stride=heads), :].astype(BF16)
        o = (jnp.dot(pc, vc, preferred_element_type=F32)
             + jnp.dot(pn, vn_ref[0, :, sl], preferred_element_type=F32))
        o_ref[0, :, sl] = _head_norm(o, sg, lam0).astype(BF16)

    s_next = scores(0)
    p_prev = None
    for h in range(heads):
        s = s_next
        if h + 1 < heads:
            s_next = scores(h + 1)
        p = weights(*s)
        if p_prev is not None:
            output(h - 1, *p_prev)
        p_prev = p
    output(heads - 1, *p_prev)


def _attn_a_sample(lp, sg, q, kn, vn, ck, cv, lam0):
    b, nt, d = q.shape
    new = pl.BlockSpec((1, nt, d), lambda bi: (bi, 0, 0))
    old = pl.BlockSpec((1,) + ck.shape[1:], lambda bi: (bi, 0, 0))
    return pl.pallas_call(
        functools.partial(_attn_a_sample_kernel, lam0=lam0),
        grid=(b,),
        in_specs=[_resident(lp.shape), _resident(sg.shape), new, new, new, old, old],
        out_specs=new,
        out_shape=jax.ShapeDtypeStruct((b, nt, d), BF16),
        compiler_params=_params("parallel"),
        name="attn_a_sample",
    )(lp, sg, q, kn, vn, ck, cv)


def _post_kernel(*refs, final):
    if final:
        x_ref, o_ref, wo_ref, g_ref, w1_ref, w2_ref, gf_ref, out_ref = refs
    else:
        x_ref, o_ref, wo_ref, g_ref, w1_ref, w2_ref, out_ref = refs
    h = x_ref[...] + jnp.dot(o_ref[...], wo_ref[...], preferred_element_type=F32)
    hn = _rms(h, g_ref[...]).astype(BF16)
    d_ff = w1_ref.shape[1]
    mlp = None
    for c in range(d_ff // FF_CHUNK):
        sl = slice(c * FF_CHUNK, (c + 1) * FF_CHUNK)
        a = jnp.dot(hn, w1_ref[:, sl], preferred_element_type=F32)
        u = jnp.square(jnp.maximum(a, 0.0)).astype(BF16)
        part = jnp.dot(u, w2_ref[sl, :], preferred_element_type=F32)
        mlp = part if mlp is None else mlp + part
    h = h + mlp
    if final:
        h = _rms(h, gf_ref[...])
    out_ref[...] = h


def _post(x, o, wo, g, w1, w2, gf=None):
    t, d = x.shape
    tm = TOKEN_TILE
    final = gf is not None
    row = pl.BlockSpec((tm, d), lambda i: (i, 0))
    in_specs = [row, row, _resident(wo.shape), _resident(g.shape), _resident(w1.shape), _resident(w2.shape)]
    args = [x, o, wo, g, w1, w2]
    if final:
        in_specs.append(_resident(gf.shape))
        args.append(gf)
    return pl.pallas_call(
        functools.partial(_post_kernel, final=final),
        grid=(t // tm,),
        in_specs=in_specs,
        out_specs=row,
        out_shape=jax.ShapeDtypeStruct((t, d), F32),
        compiler_params=_params("parallel"),
        name="post_final" if final else "post",
    )(*args)


def _proj_b_kernel(x_ref, gkv_ref, wkv_ref, gq_ref, wq_ref,
                   q_ref, kb_ref, vb_ref, k_ref, v_ref, *, period):
    d = x_ref.shape[1]
    x = x_ref[...]
    xs = x * lax.rsqrt(jnp.mean(x * x, axis=-1, keepdims=True) + EPS)
    xkv = (xs * gkv_ref[...]).astype(BF16)
    xq = (xs * gq_ref[...]).astype(BF16)
    q_ref[...] = (jnp.dot(xq, wq_ref[...], preferred_element_type=F32) * QSCALE).astype(BF16)
    k = jnp.dot(xkv, wkv_ref[:, 0:d], preferred_element_type=F32)
    v = jnp.dot(xkv, wkv_ref[:, d:2 * d], preferred_element_type=F32)
    kb_ref[...] = k.astype(BF16)
    vb_ref[...] = v.astype(BF16)

    @pl.when(pl.program_id(0) % period == period - 1)
    def _():
        k_ref[...] = k
        v_ref[...] = v


def _proj_b(x, gkv, wkv, gq, wq, period):
    t, d = x.shape
    tm = TOKEN_TILE
    row = pl.BlockSpec((tm, d), lambda i: (i, 0))
    keep = pl.BlockSpec((tm, d), lambda i: (i // period, 0))
    return pl.pallas_call(
        functools.partial(_proj_b_kernel, period=period),
        grid=(t // tm,),
        in_specs=[row, _resident(gkv.shape), _resident(wkv.shape), _resident(gq.shape), _resident(wq.shape)],
        out_specs=[row, row, row, keep, keep],
        out_shape=[jax.ShapeDtypeStruct((t, d), BF16), jax.ShapeDtypeStruct((t, d), BF16),
                   jax.ShapeDtypeStruct((t, d), BF16),
                   jax.ShapeDtypeStruct((t // period, d), F32), jax.ShapeDtypeStruct((t // period, d), F32)],
        compiler_params=_params("arbitrary"),
        name="proj_b",
    )(x, gkv, wkv, gq, wq)


def _toeplitz_bias(r_row, rows, width):
    rb = jnp.broadcast_to(r_row * LOG2E, (rows, r_row.shape[1]))
    return pltpu.roll(rb, 0, 1, stride=1, stride_axis=0)


def _band_prompt_kernel(r_ref, q_ref, k_ref, v_ref, o_ref, vt_ref, bias):
    tq = BAND_TILE
    s_len = q_ref.shape[1]
    win = PAD + tq
    for c in range(s_len // tq):
        sl = slice(c * tq, (c + 1) * tq)
        vt_ref[:, sl] = v_ref[0, sl, :].astype(F32).T.astype(BF16)

    @pl.when(pl.program_id(1) == 0)
    def _():
        key = lax.broadcasted_iota(jnp.int32, (win, tq), 0)
        qry = lax.broadcasted_iota(jnp.int32, (win, tq), 1)
        dist = qry // CHUNK + LEFT_CHUNKS - key // CHUNK
        in_band = (dist >= 0) & (dist <= LEFT_CHUNKS)
        for hh in range(2):
            rb = jnp.broadcast_to(r_ref[0, hh:hh + 1, :] * LOG2E, (win, win + tq))
            shifted = pltpu.roll(rb, 0, 1, stride=1, stride_axis=0)
            bias[:, hh * tq:(hh + 1) * tq] = jnp.where(in_band, shifted[:, win:win + tq], NEG_INF)

    lane = lax.broadcasted_iota(jnp.int32, (tq, LANES), 1)

    def window(qi):
        q0 = qi * tq
        return max(0, PAD - q0), max(0, q0 - PAD), q0 + tq

    def scores(qi):
        lo, k_lo, k_hi = window(qi)
        q = q_ref[0, qi * tq:(qi + 1) * tq, :]
        zero = jnp.zeros_like(q)
        qz = jnp.concatenate([jnp.where(lane < HEAD_DIM, q, zero),
                              jnp.where(lane >= HEAD_DIM, q, zero)], axis=0)
        return lax.dot_general(k_ref[0, k_lo:k_hi, :], qz, NT_DIMS,
                               preferred_element_type=F32) + bias[lo:win, :]

    n_tiles = s_len // tq
    s_next = scores(0)
    for qi in range(n_tiles):
        s = s_next
        if qi + 1 < n_tiles:
            s_next = scores(qi + 1)
        _, k_lo, k_hi = window(qi)
        m = jnp.max(s, axis=0, keepdims=True)
        e = jnp.exp2(s - m)
        l = jnp.sum(e, axis=0, keepdims=True)
        o = jnp.dot(vt_ref[:, k_lo:k_hi], e.astype(BF16), preferred_element_type=F32) / l
        o = jnp.concatenate([o[0:HEAD_DIM, 0:tq], o[HEAD_DIM:LANES, tq:2 * tq]], axis=0)
        o_ref[0, qi * tq:(qi + 1) * tq, :] = o.T.astype(BF16)


def _band_prompt(r, q, k, v):
    b, s, d = q.shape
    blk = pl.BlockSpec((1, s, LANES), lambda pi, bi: (bi, 0, pi))
    rblk = pl.BlockSpec((1, 2, r.shape[2]), lambda pi, bi: (pi, 0, 0))
    win = PAD + BAND_TILE
    return pl.pallas_call(
        _band_prompt_kernel,
        grid=(d // LANES, b),
        in_specs=[rblk, blk, blk, blk],
        out_specs=blk,
        out_shape=jax.ShapeDtypeStruct((b, s, d), BF16),
        scratch_shapes=[pltpu.VMEM((LANES, s), BF16), pltpu.VMEM((win, 2 * BAND_TILE), F32)],
        compiler_params=_params("arbitrary", "arbitrary"),
        name="band_prompt",
    )(r, q, k, v)


def _band_sample_kernel(r_ref, q_ref, kn_ref, vn_ref, ck_hbm, cv_hbm, o_ref, kbuf, vbuf, sem):
    nt, d = q_ref.shape[1], q_ref.shape[2]
    lb, heads = ck_hbm.shape[1], ck_hbm.shape[2]
    off = BAND_Q_TILE
    lane = lax.broadcasted_iota(jnp.int32, (nt, LANES), 1)
    pairs = d // LANES
    b = pl.program_id(0)
    slot = b % 2

    def head_copies(batch, to_slot):
        return ([pltpu.make_async_copy(ck_hbm.at[batch, :, h, :], kbuf.at[to_slot, h], sem.at[to_slot, 0])
                 for h in range(heads)]
                + [pltpu.make_async_copy(cv_hbm.at[batch, :, h, :], vbuf.at[to_slot, h], sem.at[to_slot, 1])
                   for h in range(heads)])

    @pl.when(b == 0)
    def _():
        for c in head_copies(0, 0):
            c.start()

    @pl.when(b + 1 < pl.num_programs(0))
    def _():
        for c in head_copies(b + 1, 1 - slot):
            c.start()

    for c in head_copies(b, slot):
        c.wait()

    def pair_block(buf, p):
        return jnp.concatenate([buf[slot, 2 * p], buf[slot, 2 * p + 1]], axis=1).astype(BF16)

    def scores(p):
        sl = slice(p * LANES, (p + 1) * LANES)
        q = q_ref[0, :, sl]
        zero = jnp.zeros_like(q)
        qz = jnp.concatenate([jnp.where(lane < HEAD_DIM, q, zero),
                              jnp.where(lane >= HEAD_DIM, q, zero)], axis=0)
        shifted = jnp.concatenate([_toeplitz_bias(r_ref[p, hh:hh + 1, :], nt, lb + nt)
                                   for hh in range(2)], axis=0)
        sc = (lax.dot_general(qz, pair_block(kbuf, p), NT_DIMS, preferred_element_type=F32)
              + shifted[:, off:off + lb])
        sn = (lax.dot_general(qz, kn_ref[0, :, sl], NT_DIMS, preferred_element_type=F32)
              + shifted[:, off + lb:off + lb + nt])
        return sc, sn

    def weights(sc, sn):
        m = jnp.maximum(jnp.max(sc, axis=-1, keepdims=True), jnp.max(sn, axis=-1, keepdims=True))
        ec = jnp.exp2(sc - m)
        en = jnp.exp2(sn - m)
        inv = 1.0 / (jnp.sum(ec, axis=-1, keepdims=True) + jnp.sum(en, axis=-1, keepdims=True))
        return (ec * inv).astype(BF16), (en * inv).astype(BF16)

    def output(p, pc, pn):
        sl = slice(p * LANES, (p + 1) * LANES)
        o = (jnp.dot(pc, pair_block(vbuf, p), preferred_element_type=F32)
             + jnp.dot(pn, vn_ref[0, :, sl], preferred_element_type=F32))
        o_ref[0, :, sl] = jnp.where(lane < HEAD_DIM, o[0:nt], o[nt:2 * nt]).astype(BF16)

    s_next = scores(0)
    w_prev = None
    for p in range(pairs):
        s = s_next
        if p + 1 < pairs:
            s_next = scores(p + 1)
        w = weights(*s)
        if w_prev is not None:
            output(p - 1, *w_prev)
        w_prev = w
    output(pairs - 1, *w_prev)


def _band_sample(r, q, kn, vn, ck, cv):
    b, nt, d = q.shape
    _, lb, heads, hd = ck.shape
    new = pl.BlockSpec((1, nt, d), lambda bi: (bi, 0, 0))
    hbm = pl.BlockSpec(memory_space=pl.ANY)
    return pl.pallas_call(
        _band_sample_kernel,
        grid=(b,),
        in_specs=[_resident(r.shape), new, new, new, hbm, hbm],
        out_specs=new,
        out_shape=jax.ShapeDtypeStruct((b, nt, d), BF16),
        scratch_shapes=[pltpu.VMEM((2, heads, lb, hd), F32), pltpu.VMEM((2, heads, lb, hd), F32),
                        pltpu.SemaphoreType.DMA((2, 2))],
        compiler_params=_params("arbitrary"),
        name="band_sample",
    )(r, q, kn, vn, ck, cv)


def _rope_tables(pos):
    half = HEAD_DIM // 2
    inv = 1.0 / (ROPE_THETA ** (jnp.arange(half, dtype=F32) / half))
    ang = pos.astype(F32)[:, None] * inv[None, :]
    cos, sin = jnp.cos(ang), jnp.sin(ang)
    reps = LANES // HEAD_DIM
    return (jnp.tile(jnp.concatenate([cos, cos], axis=-1), (1, reps)),
            jnp.tile(jnp.concatenate([-sin, sin], axis=-1), (1, reps)))


def _bias_rows(table):
    far = PAD + BAND_Q_TILE - REL_CLIP
    near = table[:, 2 * REL_CLIP:0:-1]
    rows = jnp.concatenate([jnp.broadcast_to(table[:, 2 * REL_CLIP:], (table.shape[0], far)), near], axis=1)
    return rows.reshape(table.shape[0] // 2, 2, rows.shape[1])


def _bias_rows_t(table):
    heads = table.shape[0]
    lo = BAND_TILE - REL_CLIP
    hi = PAD + 2 * BAND_TILE - lo - (2 * REL_CLIP + 1)
    rows = jnp.concatenate([jnp.broadcast_to(table[:, :1], (heads, lo)), table,
                            jnp.broadcast_to(table[:, 2 * REL_CLIP:], (heads, hi))], axis=1)
    return rows.reshape(heads // 2, 2, rows.shape[1])


def _lambda_init(layer):
    return 0.8 - 0.6 * math.exp(-0.3 * layer)


def kernel(x_prompt, x_sample, cache_a_k, cache_a_v, cache_b_k, cache_b_v, g_attn, w_a_qkv, a_lambda, a_subln, w_a_o, g_kv, w_kv, w_b_q, b_rel, w_b_o, g_mlp, w_ff1, w_ff2, g_final):
    nb, seq, d = x_prompt.shape
    db, nt, _ = x_sample.shape
    past = cache_a_k.shape[2]
    lb = cache_b_k.shape[1]
    a_heads = d // (2 * HEAD_DIM)
    b_heads = d // HEAD_DIM
    assert w_a_qkv.shape[0] == 1 and w_b_q.shape[0] == 1, "one differential and one band layer"
    assert (db * nt) % TOKEN_TILE == 0 and TOKEN_TILE % nt == 0 and seq % TOKEN_TILE == 0
    keep = min(PAD, seq)
    assert keep == TOKEN_TILE and lb == PAD

    lam0 = _lambda_init(0)
    w_qkv = w_a_qkv[0].astype(BF16)
    w_ao = w_a_o[0].astype(BF16)
    w_kvb = w_kv.astype(BF16)
    w_bq = w_b_q[0].astype(BF16)
    w_bo = w_b_o[0].astype(BF16)
    w1 = w_ff1.astype(BF16)
    w2 = w_ff2.astype(BF16)
    g_a = g_attn[0][None]
    g_b = g_attn[1][None]
    g_k = g_kv[None]
    g_m0 = g_mlp[0][None]
    g_m1 = g_mlp[1][None]
    g_f = g_final[None]
    lp = a_lambda[0]
    sg = a_subln[0][None]
    r = _bias_rows(b_rel[0])
    r_t = _bias_rows_t(b_rel[0])

    cos_p, sin_p = _rope_tables(jnp.arange(seq))
    cos_s, sin_s = _rope_tables(jnp.arange(past, past + nt))
    reps = TOKEN_TILE // nt
    cos_s, sin_s = jnp.tile(cos_s, (reps, 1)), jnp.tile(sin_s, (reps, 1))

    def trunk(x, cos, sin, attn_a, attn_b, period):
        t = x.shape[0]
        q, k, kb, v, vb = _proj_a(x, g_a, w_qkv, cos, sin)
        o = attn_a(q, kb, vb)
        h = _post(x, o, w_ao, g_m0, w1[0], w2[0])
        qb, kbb, vbb, k_sh, v_sh = _proj_b(h, g_k, w_kvb, g_b, w_bq, period)
        ob = attn_b(qb, kbb, vbb)
        y = _post(h, ob, w_bo, g_m1, w1[1], w2[1], g_f)
        return y, k, v, k_sh, v_sh

    def attn_a_p(q, kb, vb):
        sh = (nb, seq, d)
        return _attn_a_prompt(lp, sg, q.reshape(sh), kb.reshape(sh), vb.reshape(sh), lam0).reshape(nb * seq, d)

    def attn_b_p(q, kb, vb):
        sh = (nb, seq, d)
        return _band_prompt(r_t, q.reshape(sh), kb.reshape(sh), vb.reshape(sh)).reshape(nb * seq, d)

    ck_a = cache_a_k.reshape(db, past * a_heads, 2 * HEAD_DIM)
    cv_a = cache_a_v.reshape(db, past * a_heads, 2 * HEAD_DIM)
    ck_b = cache_b_k
    cv_b = cache_b_v

    def attn_a_s(q, kb, vb):
        sh = (db, nt, d)
        return _attn_a_sample(lp, sg, q.reshape(sh), kb.reshape(sh), vb.reshape(sh), ck_a, cv_a, lam0).reshape(db * nt, d)

    def attn_b_s(q, kb, vb):
        sh = (db, nt, d)
        return _band_sample(r, q.reshape(sh), kb.reshape(sh), vb.reshape(sh), ck_b, cv_b).reshape(db * nt, d)

    y_p, ak_p, av_p, bk_p, bv_p = trunk(x_prompt.reshape(nb * seq, d), cos_p, sin_p, attn_a_p, attn_b_p,
                                        seq // TOKEN_TILE)
    y_s, ak_s, av_s, bk_s, bv_s = trunk(x_sample.reshape(db * nt, d), cos_s, sin_s, attn_a_s, attn_b_s, 1)

    return (y_p.reshape(nb, seq, d), y_s.reshape(db, nt, d),
            ak_p.reshape(1, nb, seq, a_heads, 2 * HEAD_DIM), av_p.reshape(1, nb, seq, a_heads, 2 * HEAD_DIM),
            bk_p.reshape(nb, keep, b_heads, HEAD_DIM), bv_p.reshape(nb, keep, b_heads, HEAD_DIM),
            ak_s.reshape(1, db, nt, a_heads, 2 * HEAD_DIM), av_s.reshape(1, db, nt, a_heads, 2 * HEAD_DIM),
            bk_s.reshape(db, nt, b_heads, HEAD_DIM), bv_s.reshape(db, nt, b_heads, HEAD_DIM))
```

```python
import functools
import math

import jax
import jax.numpy as jnp
from jax import lax
from jax.experimental import pallas as pl
from jax.experimental.pallas import tpu as pltpu

HEAD_DIM = 64
CHUNK = 64
LEFT_CHUNKS = 8
PAD = LEFT_CHUNKS * CHUNK
REL_CLIP = 128
ROPE_THETA = 10000.0
EPS = 1e-6
NEG_INF = -1e30
LANES = 128
LOG2E = math.log2(math.e)
QSCALE = HEAD_DIM ** -0.5 * LOG2E

TOKEN_TILE = 512
FF_CHUNK = 1024
ATTN_TILE = 512
ONES_ROWS = 16
BAND_TILE = 256
BAND_Q_TILE = 128
VMEM_LIMIT = 56 * 1024 * 1024

F32 = jnp.float32
BF16 = jnp.bfloat16
NT_DIMS = (((1,), (1,)), ((), ()))


def _rms(x, g):
    ms = jnp.mean(x * x, axis=-1, keepdims=True)
    return x * lax.rsqrt(ms + EPS) * g


def _resident(shape):
    return pl.BlockSpec(shape, lambda *_: (0,) * len(shape), pipeline_mode=pl.Buffered(1))


def _params(*sem):
    return pltpu.CompilerParams(dimension_semantics=sem, vmem_limit_bytes=VMEM_LIMIT)


def _diff_lambda(lp, lam0):
    a = jnp.sum(lp[0:1] * lp[1:2], axis=-1, keepdims=True)
    b = jnp.sum(lp[2:3] * lp[3:4], axis=-1, keepdims=True)
    return jnp.exp(a) - jnp.exp(b) + lam0


def _head_norm(o, g, lam0):
    return _rms(o, g) * (1.0 - lam0)


def _proj_a_kernel(x_ref, g_ref, w_ref, cos_ref, sin_ref,
                   q_ref, k_ref, kb_ref, v_ref, vb_ref):
    d = x_ref.shape[1]
    xn = _rms(x_ref[...], g_ref[...]).astype(BF16)
    cos = cos_ref[...]
    sin = sin_ref[...]
    lane = lax.broadcasted_iota(jnp.int32, cos.shape, 1)
    first_half = (lane % HEAD_DIM) < (HEAD_DIM // 2)

    def rope(a):
        partner = jnp.where(first_half, pltpu.roll(a, LANES - HEAD_DIM // 2, 1),
                            pltpu.roll(a, HEAD_DIM // 2, 1))
        return a * cos + partner * sin

    q = jnp.dot(xn, w_ref[:, 0:d], preferred_element_type=F32)
    for c in range(d // LANES):
        sl = slice(c * LANES, (c + 1) * LANES)
        q_ref[:, sl] = (rope(q[:, sl]) * QSCALE).astype(BF16)
    tm, heads = x_ref.shape[0], d // LANES
    k = jnp.dot(xn, w_ref[:, d:2 * d], preferred_element_type=F32)
    for c in range(heads):
        sl = slice(c * LANES, (c + 1) * LANES)
        kr = rope(k[:, sl])
        k_ref[pl.ds(c, tm, stride=heads), :] = kr
        kb_ref[:, sl] = kr.astype(BF16)
    v = jnp.dot(xn, w_ref[:, 2 * d:3 * d], preferred_element_type=F32)
    for c in range(heads):
        v_ref[pl.ds(c, tm, stride=heads), :] = v[:, c * LANES:(c + 1) * LANES]
    vb_ref[...] = v.astype(BF16)


def _proj_a(x, g, w, cos, sin):
    t, d = x.shape
    tm = TOKEN_TILE
    period = cos.shape[0] // tm
    heads = d // LANES
    row = pl.BlockSpec((tm, d), lambda i: (i, 0))
    native = pl.BlockSpec((tm * heads, LANES), lambda i: (i, 0))
    tab = pl.BlockSpec((tm, LANES), lambda i: (i % period, 0))
    return pl.pallas_call(
        _proj_a_kernel,
        grid=(t // tm,),
        in_specs=[row, _resident((1, d)), _resident((d, 3 * d)), tab, tab],
        out_specs=[row, native, row, native, row],
        out_shape=[jax.ShapeDtypeStruct((t, d), BF16), jax.ShapeDtypeStruct((t * heads, LANES), F32),
                   jax.ShapeDtypeStruct((t, d), BF16), jax.ShapeDtypeStruct((t * heads, LANES), F32),
                   jax.ShapeDtypeStruct((t, d), BF16)],
        compiler_params=_params("parallel"),
        name="proj_a",
    )(x, g, w, cos, sin)


def _attn_a_prompt_kernel(lp_ref, sg_ref, q_ref, k_ref, v_ref, o_ref, vt_ref, *, lam0):
    t = ATTN_TILE
    s_len = q_ref.shape[1]
    lam = _diff_lambda(lp_ref[...], lam0)
    sg = sg_ref[...]
    for c in range(s_len // t):
        sl = slice(c * t, (c + 1) * t)
        vt_ref[0:LANES, sl] = v_ref[0, sl, :].astype(F32).T.astype(BF16)

    vt_ref[LANES:, :] = jnp.ones((vt_ref.shape[0] - LANES, s_len), BF16)

    lane = lax.broadcasted_iota(jnp.int32, (t, LANES), 1)
    key_chunk = lax.broadcasted_iota(jnp.int32, (t, 2 * t), 0) // CHUNK
    qry_chunk = (lax.broadcasted_iota(jnp.int32, (t, 2 * t), 1) % t) // CHUNK
    diag_mask = qry_chunk >= key_chunk

    def scores(qi, j):
        q = q_ref[0, qi * t:(qi + 1) * t, :]
        zero = jnp.zeros_like(q)
        qz = jnp.concatenate([jnp.where(lane < HEAD_DIM, q, zero),
                              jnp.where(lane >= HEAD_DIM, q, zero)], axis=0)
        s = lax.dot_general(k_ref[0, j * t:(j + 1) * t, :], qz, NT_DIMS,
                            preferred_element_type=F32)
        return jnp.where(diag_mask, s, NEG_INF) if j == qi else s

    pairs = [(qi, j) for qi in range(s_len // t) for j in range(qi + 1)]
    s_next = scores(*pairs[0])
    for n, (qi, j) in enumerate(pairs):
        s = s_next
        if n + 1 < len(pairs):
            s_next = scores(*pairs[n + 1])
        if j == 0:
            m = jnp.full((1, 2 * t), NEG_INF, F32)
            acc = jnp.zeros((vt_ref.shape[0], 2 * t), F32)
        m_new = jnp.maximum(m, jnp.max(s, axis=0, keepdims=True))
        p = jnp.exp2(s - m_new)
        acc = jnp.exp2(m - m_new) * acc + jnp.dot(vt_ref[:, j * t:(j + 1) * t], p.astype(BF16),
                                                  preferred_element_type=F32)
        m = m_new
        if j == qi:
            a = acc[0:LANES] / acc[LANES:LANES + 1]
            o = a[:, 0:t] - lam * a[:, t:2 * t]
            o = o * lax.rsqrt(jnp.mean(o * o, axis=0, keepdims=True) + EPS)
            o_ref[0, qi * t:(qi + 1) * t, :] = (o.T * sg * (1.0 - lam0)).astype(BF16)


def _attn_a_prompt(lp, sg, q, k, v, lam0):
    b, s, d = q.shape
    blk = pl.BlockSpec((1, s, LANES), lambda bi, hi: (bi, 0, hi))
    return pl.pallas_call(
        functools.partial(_attn_a_prompt_kernel, lam0=lam0),
        grid=(b, d // LANES),
        in_specs=[_resident(lp.shape), _resident(sg.shape), blk, blk, blk],
        out_specs=blk,
        out_shape=jax.ShapeDtypeStruct((b, s, d), BF16),
        scratch_shapes=[pltpu.VMEM((LANES + ONES_ROWS, s), BF16)],
        compiler_params=_params("parallel", "parallel"),
        name="attn_a_prompt",
    )(lp, sg, q, k, v)


def _attn_a_sample_kernel(lp_ref, sg_ref, q_ref, kn_ref, vn_ref, ck_ref, cv_ref, o_ref, *, lam0):
    nt, d = q_ref.shape[1], q_ref.shape[2]
    heads = d // LANES
    past = ck_ref.shape[1] // heads
    lam = _diff_lambda(lp_ref[...], lam0)
    sg = sg_ref[...]
    lane = lax.broadcasted_iota(jnp.int32, (nt, LANES), 1)

    def scores(h):
        sl = slice(h * LANES, (h + 1) * LANES)
        q = q_ref[0, :, sl]
        zero = jnp.zeros_like(q)
        qz = jnp.concatenate([jnp.where(lane < HEAD_DIM, q, zero),
                              jnp.where(lane >= HEAD_DIM, q, zero)], axis=0)
        kc = ck_ref[0, pl.ds(h, past, stride=heads), :].astype(BF16)
        return (lax.dot_general(qz, kc, NT_DIMS, preferred_element_type=F32),
                lax.dot_general(qz, kn_ref[0, :, sl], NT_DIMS, preferred_element_type=F32))

    def weights(sc, sn):
        m = jnp.maximum(jnp.max(sc, axis=-1, keepdims=True), jnp.max(sn, axis=-1, keepdims=True))
        ec = jnp.exp2(sc - m)
        en = jnp.exp2(sn - m)
        inv = 1.0 / (jnp.sum(ec, axis=-1, keepdims=True) + jnp.sum(en, axis=-1, keepdims=True))
        pc = ec * inv
        pn = en * inv
        return ((pc[0:nt] - lam * pc[nt:2 * nt]).astype(BF16),
                (pn[0:nt] - lam * pn[nt:2 * nt]).astype(BF16))

    def output(h, pc, pn):
        sl = slice(h * LANES, (h + 1) * LANES)
        vc = cv_ref[0, pl.ds(h, past, stride=heads), :].astype(BF16)
        o = (jnp.dot(pc, vc, preferred_element_type=F32)
             + jnp.dot(pn, vn_ref[0, :, sl], preferred_element_type=F32))
        o_ref[0, :, sl] = _head_norm(o, sg, lam0).astype(BF16)

    s_next = scores(0)
    p_prev = None
    for h in range(heads):
        s = s_next
        if h + 1 < heads:
            s_next = scores(h + 1)
        p = weights(*s)
        if p_prev is not None:
            output(h - 1, *p_prev)
        p_prev = p
    output(heads - 1, *p_prev)


def _attn_a_sample(lp, sg, q, kn, vn, ck, cv, lam0):
    b, nt, d = q.shape
    new = pl.BlockSpec((1, nt, d), lambda bi: (bi, 0, 0))
    old = pl.BlockSpec((1,) + ck.shape[1:], lambda bi: (bi, 0, 0))
    return pl.pallas_call(
        functools.partial(_attn_a_sample_kernel, lam0=lam0),
        grid=(b,),
        in_specs=[_resident(lp.shape), _resident(sg.shape), new, new, new, old, old],
        out_specs=new,
        out_shape=jax.ShapeDtypeStruct((b, nt, d), BF16),
        compiler_params=_params("parallel"),
        name="attn_a_sample",
    )(lp, sg, q, kn, vn, ck, cv)


def _post_kernel(*refs, final):
    if final:
        x_ref, o_ref, wo_ref, g_ref, w1_ref, w2_ref, gf_ref, out_ref = refs
    else:
        x_ref, o_ref, wo_ref, g_ref, w1_ref, w2_ref, out_ref = refs
    h = x_ref[...] + jnp.dot(o_ref[...], wo_ref[...], preferred_element_type=F32)
    hn = _rms(h, g_ref[...]).astype(BF16)
    d_ff = w1_ref.shape[1]
    mlp = None
    for c in range(d_ff // FF_CHUNK):
        sl = slice(c * FF_CHUNK, (c + 1) * FF_CHUNK)
        a = jnp.dot(hn, w1_ref[:, sl], preferred_element_type=F32)
        u = jnp.square(jnp.maximum(a, 0.0)).astype(BF16)
        part = jnp.dot(u, w2_ref[sl, :], preferred_element_type=F32)
        mlp = part if mlp is None else mlp + part
    h = h + mlp
    if final:
        h = _rms(h, gf_ref[...])
    out_ref[...] = h


def _post(x, o, wo, g, w1, w2, gf=None):
    t, d = x.shape
    tm = TOKEN_TILE
    final = gf is not None
    row = pl.BlockSpec((tm, d), lambda i: (i, 0))
    in_specs = [row, row, _resident(wo.shape), _resident(g.shape), _resident(w1.shape), _resident(w2.shape)]
    args = [x, o, wo, g, w1, w2]
    if final:
        in_specs.append(_resident(gf.shape))
        args.append(gf)
    return pl.pallas_call(
        functools.partial(_post_kernel, final=final),
        grid=(t // tm,),
        in_specs=in_specs,
        out_specs=row,
        out_shape=jax.ShapeDtypeStruct((t, d), F32),
        compiler_params=_params("parallel"),
        name="post_final" if final else "post",
    )(*args)


def _proj_b_kernel(x_ref, gkv_ref, wkv_ref, gq_ref, wq_ref,
                   q_ref, kb_ref, vb_ref, k_ref, v_ref, *, period):
    d = x_ref.shape[1]
    x = x_ref[...]
    xs = x * lax.rsqrt(jnp.mean(x * x, axis=-1, keepdims=True) + EPS)
    xkv = (xs * gkv_ref[...]).astype(BF16)
    xq = (xs * gq_ref[...]).astype(BF16)
    q_ref[...] = (jnp.dot(xq, wq_ref[...], preferred_element_type=F32) * QSCALE).astype(BF16)
    k = jnp.dot(xkv, wkv_ref[:, 0:d], preferred_element_type=F32)
    v = jnp.dot(xkv, wkv_ref[:, d:2 * d], preferred_element_type=F32)
    kb_ref[...] = k.astype(BF16)
    vb_ref[...] = v.astype(BF16)

    @pl.when(pl.program_id(0) % period == period - 1)
    def _():
        k_ref[...] = k
        v_ref[...] = v


def _proj_b(x, gkv, wkv, gq, wq, period):
    t, d = x.shape
    tm = TOKEN_TILE
    row = pl.BlockSpec((tm, d), lambda i: (i, 0))
    keep = pl.BlockSpec((tm, d), lambda i: (i // period, 0))
    return pl.pallas_call(
        functools.partial(_proj_b_kernel, period=period),
        grid=(t // tm,),
        in_specs=[row, _resident(gkv.shape), _resident(wkv.shape), _resident(gq.shape), _resident(wq.shape)],
        out_specs=[row, row, row, keep, keep],
        out_shape=[jax.ShapeDtypeStruct((t, d), BF16), jax.ShapeDtypeStruct((t, d), BF16),
                   jax.ShapeDtypeStruct((t, d), BF16),
                   jax.ShapeDtypeStruct((t // period, d), F32), jax.ShapeDtypeStruct((t // period, d), F32)],
        compiler_params=_params("arbitrary"),
        name="proj_b",
    )(x, gkv, wkv, gq, wq)


def _toeplitz_bias(r_row, rows, width):
    rb = jnp.broadcast_to(r_row * LOG2E, (rows, r_row.shape[1]))
    return pltpu.roll(rb, 0, 1, stride=1, stride_axis=0)


def _band_prompt_kernel(r_ref, q_ref, k_ref, v_ref, o_ref, vt_ref, bias):
    tq = BAND_TILE
    s_len = q_ref.shape[1]
    win = PAD + tq
    for c in range(s_len // tq):
        sl = slice(c * tq, (c + 1) * tq)
        vt_ref[0:LANES, sl] = v_ref[0, sl, :].astype(F32).T.astype(BF16)
    vt_ref[LANES:, :] = jnp.ones((vt_ref.shape[0] - LANES, s_len), BF16)

    @pl.when(pl.program_id(1) == 0)
    def _():
        key = lax.broadcasted_iota(jnp.int32, (win, tq), 0)
        qry = lax.broadcasted_iota(jnp.int32, (win, tq), 1)
        dist = qry // CHUNK + LEFT_CHUNKS - key // CHUNK
        in_band = (dist >= 0) & (dist <= LEFT_CHUNKS)
        for hh in range(2):
            rb = jnp.broadcast_to(r_ref[0, hh:hh + 1, :] * LOG2E, (win, win + tq))
            shifted = pltpu.roll(rb, 0, 1, stride=1, stride_axis=0)
            bias[:, hh * tq:(hh + 1) * tq] = jnp.where(in_band, shifted[:, win:win + tq], NEG_INF)

    lane = lax.broadcasted_iota(jnp.int32, (tq, LANES), 1)

    def window(qi):
        q0 = qi * tq
        return max(0, PAD - q0), max(0, q0 - PAD), q0 + tq

    def scores(qi):
        lo, k_lo, k_hi = window(qi)
        q = q_ref[0, qi * tq:(qi + 1) * tq, :]
        zero = jnp.zeros_like(q)
        qz = jnp.concatenate([jnp.where(lane < HEAD_DIM, q, zero),
                              jnp.where(lane >= HEAD_DIM, q, zero)], axis=0)
        return lax.dot_general(k_ref[0, k_lo:k_hi, :], qz, NT_DIMS,
                               preferred_element_type=F32) + bias[lo:win, :]

    n_tiles = s_len // tq
    s_next = scores(0)
    for qi in range(n_tiles):
        s = s_next
        if qi + 1 < n_tiles:
            s_next = scores(qi + 1)
        _, k_lo, k_hi = window(qi)
        m = jnp.max(s, axis=0, keepdims=True)
        e = jnp.exp2(s - m)
        o = jnp.dot(vt_ref[:, k_lo:k_hi], e.astype(BF16), preferred_element_type=F32)
        o = o[0:LANES] / o[LANES:LANES + 1]
        o = jnp.concatenate([o[0:HEAD_DIM, 0:tq], o[HEAD_DIM:LANES, tq:2 * tq]], axis=0)
        o_ref[0, qi * tq:(qi + 1) * tq, :] = o.T.astype(BF16)


def _band_prompt(r, q, k, v):
    b, s, d = q.shape
    blk = pl.BlockSpec((1, s, LANES), lambda pi, bi: (bi, 0, pi))
    rblk = pl.BlockSpec((1, 2, r.shape[2]), lambda pi, bi: (pi, 0, 0))
    win = PAD + BAND_TILE
    return pl.pallas_call(
        _band_prompt_kernel,
        grid=(d // LANES, b),
        in_specs=[rblk, blk, blk, blk],
        out_specs=blk,
        out_shape=jax.ShapeDtypeStruct((b, s, d), BF16),
        scratch_shapes=[pltpu.VMEM((LANES + ONES_ROWS, s), BF16), pltpu.VMEM((win, 2 * BAND_TILE), F32)],
        compiler_params=_params("arbitrary", "arbitrary"),
        name="band_prompt",
    )(r, q, k, v)


def _band_sample_kernel(r_ref, q_ref, kn_ref, vn_ref, ck_ref, cv_ref, o_ref):
    nt, d = q_ref.shape[1], q_ref.shape[2]
    lb = ck_ref.shape[1]
    off = BAND_Q_TILE
    lane = lax.broadcasted_iota(jnp.int32, (nt, LANES), 1)
    pairs = d // LANES

    def scores(p):
        sl = slice(p * LANES, (p + 1) * LANES)
        q = q_ref[0, :, sl]
        zero = jnp.zeros_like(q)
        qz = jnp.concatenate([jnp.where(lane < HEAD_DIM, q, zero),
                              jnp.where(lane >= HEAD_DIM, q, zero)], axis=0)
        shifted = jnp.concatenate([_toeplitz_bias(r_ref[p, hh:hh + 1, :], nt, lb + nt)
                                   for hh in range(2)], axis=0)
        sc = (lax.dot_general(qz, ck_ref[0, :, sl].astype(BF16), NT_DIMS, preferred_element_type=F32)
              + shifted[:, off:off + lb])
        sn = (lax.dot_general(qz, kn_ref[0, :, sl], NT_DIMS, preferred_element_type=F32)
              + shifted[:, off + lb:off + lb + nt])
        return sc, sn

    def weights(sc, sn):
        m = jnp.maximum(jnp.max(sc, axis=-1, keepdims=True), jnp.max(sn, axis=-1, keepdims=True))
        ec = jnp.exp2(sc - m)
        en = jnp.exp2(sn - m)
        inv = 1.0 / (jnp.sum(ec, axis=-1, keepdims=True) + jnp.sum(en, axis=-1, keepdims=True))
        return (ec * inv).astype(BF16), (en * inv).astype(BF16)

    def output(p, pc, pn):
        sl = slice(p * LANES, (p + 1) * LANES)
        o = (jnp.dot(pc, cv_ref[0, :, sl].astype(BF16), preferred_element_type=F32)
             + jnp.dot(pn, vn_ref[0, :, sl], preferred_element_type=F32))
        o_ref[0, :, sl] = jnp.where(lane < HEAD_DIM, o[0:nt], o[nt:2 * nt]).astype(BF16)

    s_next = scores(0)
    w_prev = None
    for p in range(pairs):
        s = s_next
        if p + 1 < pairs:
            s_next = scores(p + 1)
        w = weights(*s)
        if w_prev is not None:
            output(p - 1, *w_prev)
        w_prev = w
    output(pairs - 1, *w_prev)


def _band_sample(r, q, kn, vn, ck, cv):
    b, nt, d = q.shape
    lb = ck.shape[1]
    new = pl.BlockSpec((1, nt, d), lambda bi: (bi, 0, 0))
    old = pl.BlockSpec((1, lb, d), lambda bi: (bi, 0, 0))
    return pl.pallas_call(
        _band_sample_kernel,
        grid=(b,),
        in_specs=[_resident(r.shape), new, new, new, old, old],
        out_specs=new,
        out_shape=jax.ShapeDtypeStruct((b, nt, d), BF16),
        compiler_params=_params("parallel"),
        name="band_sample",
    )(r, q, kn, vn, ck, cv)


def _rope_tables(pos):
    half = HEAD_DIM // 2
    inv = 1.0 / (ROPE_THETA ** (jnp.arange(half, dtype=F32) / half))
    ang = pos.astype(F32)[:, None] * inv[None, :]
    cos, sin = jnp.cos(ang), jnp.sin(ang)
    reps = LANES // HEAD_DIM
    return (jnp.tile(jnp.concatenate([cos, cos], axis=-1), (1, reps)),
            jnp.tile(jnp.concatenate([-sin, sin], axis=-1), (1, reps)))


def _bias_rows(table):
    far = PAD + BAND_Q_TILE - REL_CLIP
    near = table[:, 2 * REL_CLIP:0:-1]
    rows = jnp.concatenate([jnp.broadcast_to(table[:, 2 * REL_CLIP:], (table.shape[0], far)), near], axis=1)
    return rows.reshape(table.shape[0] // 2, 2, rows.shape[1])


def _bias_rows_t(table):
    heads = table.shape[0]
    lo = BAND_TILE - REL_CLIP
    hi = PAD + 2 * BAND_TILE - lo - (2 * REL_CLIP + 1)
    rows = jnp.concatenate([jnp.broadcast_to(table[:, :1], (heads, lo)), table,
                            jnp.broadcast_to(table[:, 2 * REL_CLIP:], (heads, hi))], axis=1)
    return rows.reshape(heads // 2, 2, rows.shape[1])


def _lambda_init(layer):
    return 0.8 - 0.6 * math.exp(-0.3 * layer)


def kernel(x_prompt, x_sample, cache_a_k, cache_a_v, cache_b_k, cache_b_v, g_attn, w_a_qkv, a_lambda, a_subln, w_a_o, g_kv, w_kv, w_b_q, b_rel, w_b_o, g_mlp, w_ff1, w_ff2, g_final):
    nb, seq, d = x_prompt.shape
    db, nt, _ = x_sample.shape
    past = cache_a_k.shape[2]
    lb = cache_b_k.shape[1]
    a_heads = d // (2 * HEAD_DIM)
    b_heads = d // HEAD_DIM
    assert w_a_qkv.shape[0] == 1 and w_b_q.shape[0] == 1, "one differential and one band layer"
    assert (db * nt) % TOKEN_TILE == 0 and TOKEN_TILE % nt == 0 and seq % TOKEN_TILE == 0
    keep = min(PAD, seq)
    assert keep == TOKEN_TILE and lb == PAD

    lam0 = _lambda_init(0)
    w_qkv = w_a_qkv[0].astype(BF16)
    w_ao = w_a_o[0].astype(BF16)
    w_kvb = w_kv.astype(BF16)
    w_bq = w_b_q[0].astype(BF16)
    w_bo = w_b_o[0].astype(BF16)
    w1 = [w_ff1[l].astype(BF16) for l in range(w_ff1.shape[0])]
    w2 = [w_ff2[l].astype(BF16) for l in range(w_ff2.shape[0])]
    g_a = g_attn[0][None]
    g_b = g_attn[1][None]
    g_k = g_kv[None]
    g_m0 = g_mlp[0][None]
    g_m1 = g_mlp[1][None]
    g_f = g_final[None]
    lp = a_lambda[0]
    sg = a_subln[0][None]
    r = _bias_rows(b_rel[0])
    r_t = _bias_rows_t(b_rel[0])

    cos_p, sin_p = _rope_tables(jnp.arange(seq))
    cos_s, sin_s = _rope_tables(jnp.arange(past, past + nt))
    reps = TOKEN_TILE // nt
    cos_s, sin_s = jnp.tile(cos_s, (reps, 1)), jnp.tile(sin_s, (reps, 1))

    def trunk(x, cos, sin, attn_a, attn_b, period):
        t = x.shape[0]
        q, k, kb, v, vb = _proj_a(x, g_a, w_qkv, cos, sin)
        o = attn_a(q, kb, vb)
        h = _post(x, o, w_ao, g_m0, w1[0], w2[0])
        qb, kbb, vbb, k_sh, v_sh = _proj_b(h, g_k, w_kvb, g_b, w_bq, period)
        ob = attn_b(qb, kbb, vbb)
        y = _post(h, ob, w_bo, g_m1, w1[1], w2[1], g_f)
        return y, k, v, k_sh, v_sh

    def attn_a_p(q, kb, vb):
        sh = (nb, seq, d)
        return _attn_a_prompt(lp, sg, q.reshape(sh), kb.reshape(sh), vb.reshape(sh), lam0).reshape(nb * seq, d)

    def attn_b_p(q, kb, vb):
        sh = (nb, seq, d)
        return _band_prompt(r_t, q.reshape(sh), kb.reshape(sh), vb.reshape(sh)).reshape(nb * seq, d)

    ck_a = cache_a_k.reshape(db, past * a_heads, 2 * HEAD_DIM)
    cv_a = cache_a_v.reshape(db, past * a_heads, 2 * HEAD_DIM)
    ck_b = cache_b_k.reshape(db, lb, d)
    cv_b = cache_b_v.reshape(db, lb, d)

    def attn_a_s(q, kb, vb):
        sh = (db, nt, d)
        return _attn_a_sample(lp, sg, q.reshape(sh), kb.reshape(sh), vb.reshape(sh), ck_a, cv_a, lam0).reshape(db * nt, d)

    def attn_b_s(q, kb, vb):
        sh = (db, nt, d)
        return _band_sample(r, q.reshape(sh), kb.reshape(sh), vb.reshape(sh), ck_b, cv_b).reshape(db * nt, d)

    y_p, ak_p, av_p, bk_p, bv_p = trunk(x_prompt.reshape(nb * seq, d), cos_p, sin_p, attn_a_p, attn_b_p,
                                        seq // TOKEN_TILE)
    y_s, ak_s, av_s, bk_s, bv_s = trunk(x_sample.reshape(db * nt, d), cos_s, sin_s, attn_a_s, attn_b_s, 1)

    return (y_p.reshape(nb, seq, d), y_s.reshape(db, nt, d),
            ak_p.reshape(1, nb, seq, a_heads, 2 * HEAD_DIM), av_p.reshape(1, nb, seq, a_heads, 2 * HEAD_DIM),
            bk_p.reshape(nb, keep, b_heads, HEAD_DIM), bv_p.reshape(nb, keep, b_heads, HEAD_DIM),
            ak_s.reshape(1, db, nt, a_heads, 2 * HEAD_DIM), av_s.reshape(1, db, nt, a_heads, 2 * HEAD_DIM),
            bk_s.reshape(db, nt, b_heads, HEAD_DIM), bv_s.reshape(db, nt, b_heads, HEAD_DIM))
```

```python
import functools
import math

import jax
import jax.numpy as jnp
from jax import lax
from jax.experimental import pallas as pl
from jax.experimental.pallas import tpu as pltpu

HEAD_DIM = 64
CHUNK = 64
LEFT_CHUNKS = 8
PAD = LEFT_CHUNKS * CHUNK
REL_CLIP = 128
ROPE_THETA = 10000.0
EPS = 1e-6
NEG_INF = -1e30
LANES = 128
LOG2E = math.log2(math.e)
QSCALE = HEAD_DIM ** -0.5 * LOG2E

TOKEN_TILE = 512
FF_CHUNK = 1024
ATTN_TILE = 512
ONES_ROWS = 16
BAND_TILE = 256
BAND_Q_TILE = 128
VMEM_LIMIT = 56 * 1024 * 1024

F32 = jnp.float32
BF16 = jnp.bfloat16
NT_DIMS = (((1,), (1,)), ((), ()))


def _rms(x, g):
    ms = jnp.mean(x * x, axis=-1, keepdims=True)
    return x * lax.rsqrt(ms + EPS) * g


def _resident(shape):
    return pl.BlockSpec(shape, lambda *_: (0,) * len(shape), pipeline_mode=pl.Buffered(1))


def _params(*sem):
    return pltpu.CompilerParams(dimension_semantics=sem, vmem_limit_bytes=VMEM_LIMIT)


def _diff_lambda(lp, lam0):
    a = jnp.sum(lp[0:1] * lp[1:2], axis=-1, keepdims=True)
    b = jnp.sum(lp[2:3] * lp[3:4], axis=-1, keepdims=True)
    return jnp.exp(a) - jnp.exp(b) + lam0


def _head_norm(o, g, lam0):
    return _rms(o, g) * (1.0 - lam0)


def _bf16_values(v, transposed):
    return (v.T if transposed else v).astype(BF16)


def _proj_a_kernel(x_ref, g_ref, w_ref, cos_ref, sin_ref,
                   q_ref, k_ref, kb_ref, v_ref, vb_ref, *, v_transposed):
    d = x_ref.shape[1]
    xn = _rms(x_ref[...], g_ref[...]).astype(BF16)
    cos = cos_ref[...]
    sin = sin_ref[...]
    lane = lax.broadcasted_iota(jnp.int32, cos.shape, 1)
    first_half = (lane % HEAD_DIM) < (HEAD_DIM // 2)

    def rope(a):
        partner = jnp.where(first_half, pltpu.roll(a, LANES - HEAD_DIM // 2, 1),
                            pltpu.roll(a, HEAD_DIM // 2, 1))
        return a * cos + partner * sin

    q = jnp.dot(xn, w_ref[:, 0:d], preferred_element_type=F32)
    for c in range(d // LANES):
        sl = slice(c * LANES, (c + 1) * LANES)
        q_ref[:, sl] = (rope(q[:, sl]) * QSCALE).astype(BF16)
    tm, heads = x_ref.shape[0], d // LANES
    k = jnp.dot(xn, w_ref[:, d:2 * d], preferred_element_type=F32)
    for c in range(heads):
        sl = slice(c * LANES, (c + 1) * LANES)
        kr = rope(k[:, sl])
        k_ref[pl.ds(c, tm, stride=heads), :] = kr
        kb_ref[:, sl] = kr.astype(BF16)
    v = jnp.dot(xn, w_ref[:, 2 * d:3 * d], preferred_element_type=F32)
    for c in range(heads):
        v_ref[pl.ds(c, tm, stride=heads), :] = v[:, c * LANES:(c + 1) * LANES]
    vb_ref[...] = _bf16_values(v, v_transposed)


def _values_out(t, d, tm, transposed):
    if transposed:
        return pl.BlockSpec((d, tm), lambda i: (0, i)), jax.ShapeDtypeStruct((d, t), BF16)
    return pl.BlockSpec((tm, d), lambda i: (i, 0)), jax.ShapeDtypeStruct((t, d), BF16)


def _proj_a(x, g, w, cos, sin, v_transposed):
    t, d = x.shape
    tm = TOKEN_TILE
    period = cos.shape[0] // tm
    heads = d // LANES
    row = pl.BlockSpec((tm, d), lambda i: (i, 0))
    native = pl.BlockSpec((tm * heads, LANES), lambda i: (i, 0))
    tab = pl.BlockSpec((tm, LANES), lambda i: (i % period, 0))
    vb_spec, vb_shape = _values_out(t, d, tm, v_transposed)
    return pl.pallas_call(
        functools.partial(_proj_a_kernel, v_transposed=v_transposed),
        grid=(t // tm,),
        in_specs=[row, _resident((1, d)), _resident((d, 3 * d)), tab, tab],
        out_specs=[row, native, row, native, vb_spec],
        out_shape=[jax.ShapeDtypeStruct((t, d), BF16), jax.ShapeDtypeStruct((t * heads, LANES), F32),
                   jax.ShapeDtypeStruct((t, d), BF16), jax.ShapeDtypeStruct((t * heads, LANES), F32),
                   vb_shape],
        compiler_params=_params("parallel"),
        name="proj_a",
    )(x, g, w, cos, sin)


def _attn_a_prompt_kernel(lp_ref, sg_ref, q_ref, k_ref, v_ref, o_ref, vt_ref, *, lam0):
    t = ATTN_TILE
    s_len = q_ref.shape[1]
    lam = _diff_lambda(lp_ref[...], lam0)
    sg = sg_ref[...]
    vt_ref[0:LANES, :] = v_ref[...]

    vt_ref[LANES:, :] = jnp.ones((vt_ref.shape[0] - LANES, s_len), BF16)

    th = t // 2
    lane = lax.broadcasted_iota(jnp.int32, (th, LANES), 1)
    key_chunk = lax.broadcasted_iota(jnp.int32, (th, t), 0) // CHUNK
    qry_chunk = (lax.broadcasted_iota(jnp.int32, (th, t), 1) % th) // CHUNK
    diag_half = qry_chunk >= key_chunk

    def query_rows(qi, halves):
        rows = []
        for u in halves:
            q = q_ref[0, qi * t + u * th:qi * t + (u + 1) * th, :]
            zero = jnp.zeros_like(q)
            rows += [jnp.where(lane < HEAD_DIM, q, zero), jnp.where(lane >= HEAD_DIM, q, zero)]
        return jnp.concatenate(rows, axis=0)

    def scores(qi, j):
        if j < qi:
            return lax.dot_general(k_ref[0, j * t:(j + 1) * t, :], query_rows(qi, (0, 1)), NT_DIMS,
                                   preferred_element_type=F32)
        top = lax.dot_general(k_ref[0, qi * t:qi * t + th, :], query_rows(qi, (0, 1)), NT_DIMS,
                              preferred_element_type=F32)
        bot = lax.dot_general(k_ref[0, qi * t + th:(qi + 1) * t, :], query_rows(qi, (1,)), NT_DIMS,
                              preferred_element_type=F32)
        first = jnp.where(diag_half, top[:, 0:t], NEG_INF)
        second = jnp.concatenate([top[:, t:2 * t], jnp.where(diag_half, bot, NEG_INF)], axis=0)
        return first, second

    def update(s, k_lo, k_hi, m, acc):
        m_new = jnp.maximum(m, jnp.max(s, axis=0, keepdims=True))
        p = jnp.exp2(s - m_new)
        acc = jnp.exp2(m - m_new) * acc + jnp.dot(vt_ref[:, k_lo:k_hi], p.astype(BF16),
                                                  preferred_element_type=F32)
        return m_new, acc

    pairs = [(qi, j) for qi in range(s_len // t) for j in range(qi + 1)]
    s_next = scores(*pairs[0])
    for n, (qi, j) in enumerate(pairs):
        s = s_next
        if n + 1 < len(pairs):
            s_next = scores(*pairs[n + 1])
        if j == 0:
            m = jnp.full((1, 2 * t), NEG_INF, F32)
            acc = jnp.zeros((vt_ref.shape[0], 2 * t), F32)
        if j < qi:
            m, acc = update(s, j * t, (j + 1) * t, m, acc)
        else:
            _, acc0 = update(s[0], qi * t, qi * t + th, m[:, 0:t], acc[:, 0:t])
            _, acc1 = update(s[1], qi * t, (qi + 1) * t, m[:, t:2 * t], acc[:, t:2 * t])
            halves = []
            for a in (acc0, acc1):
                a = a[0:LANES] / a[LANES:LANES + 1]
                halves.append(a[:, 0:th] - lam * a[:, th:t])
            o = jnp.concatenate(halves, axis=1)
            o = o * lax.rsqrt(jnp.mean(o * o, axis=0, keepdims=True) + EPS)
            o_ref[0, qi * t:(qi + 1) * t, :] = (o.T * sg * (1.0 - lam0)).astype(BF16)


def _attn_a_prompt(lp, sg, q, k, v, lam0):
    b, s, d = q.shape
    blk = pl.BlockSpec((1, s, LANES), lambda bi, hi: (bi, 0, hi))
    vt_blk = pl.BlockSpec((LANES, s), lambda bi, hi: (hi, bi))
    return pl.pallas_call(
        functools.partial(_attn_a_prompt_kernel, lam0=lam0),
        grid=(b, d // LANES),
        in_specs=[_resident(lp.shape), _resident(sg.shape), blk, blk, vt_blk],
        out_specs=blk,
        out_shape=jax.ShapeDtypeStruct((b, s, d), BF16),
        scratch_shapes=[pltpu.VMEM((LANES + ONES_ROWS, s), BF16)],
        compiler_params=_params("parallel", "parallel"),
        name="attn_a_prompt",
    )(lp, sg, q, k, v)


def _attn_a_sample_kernel(lp_ref, sg_ref, q_ref, kn_ref, vn_ref, ck_ref, cv_ref, o_ref, *, lam0):
    nt, d = q_ref.shape[1], q_ref.shape[2]
    heads = d // LANES
    past = ck_ref.shape[1] // heads
    lam = _diff_lambda(lp_ref[...], lam0)
    sg = sg_ref[...]
    lane = lax.broadcasted_iota(jnp.int32, (nt, LANES), 1)

    def scores(h):
        sl = slice(h * LANES, (h + 1) * LANES)
        q = q_ref[0, :, sl]
        zero = jnp.zeros_like(q)
        qz = jnp.concatenate([jnp.where(lane < HEAD_DIM, q, zero),
                              jnp.where(lane >= HEAD_DIM, q, zero)], axis=0)
        kc = ck_ref[0, pl.ds(h, past, stride=heads), :].astype(BF16)
        return (lax.dot_general(qz, kc, NT_DIMS, preferred_element_type=F32),
                lax.dot_general(qz, kn_ref[0, :, sl], NT_DIMS, preferred_element_type=F32))

    def weights(sc, sn):
        m = jnp.maximum(jnp.max(sc, axis=-1, keepdims=True), jnp.max(sn, axis=-1, keepdims=True))
        ec = jnp.exp2(sc - m)
        en = jnp.exp2(sn - m)
        inv = 1.0 / (jnp.sum(ec, axis=-1, keepdims=True) + jnp.sum(en, axis=-1, keepdims=True))
        pc = ec * inv
        pn = en * inv
        return ((pc[0:nt] - lam * pc[nt:2 * nt]).astype(BF16),
                (pn[0:nt] - lam * pn[nt:2 * nt]).astype(BF16))

    def output(h, pc, pn):
        sl = slice(h * LANES, (h + 1) * LANES)
        vc = cv_ref[0, pl.ds(h, past, stride=heads), :].astype(BF16)
        o = (jnp.dot(pc, vc, preferred_element_type=F32)
             + jnp.dot(pn, vn_ref[0, :, sl], preferred_element_type=F32))
        o_ref[0, :, sl] = _head_norm(o, sg, lam0).astype(BF16)

    s_next = scores(0)
    p_prev = None
    for h in range(heads):
        s = s_next
        if h + 1 < heads:
            s_next = scores(h + 1)
        p = weights(*s)
        if p_prev is not None:
            output(h - 1, *p_prev)
        p_prev = p
    output(heads - 1, *p_prev)


def _attn_a_sample(lp, sg, q, kn, vn, ck, cv, lam0):
    b, nt, d = q.shape
    new = pl.BlockSpec((1, nt, d), lambda bi: (bi, 0, 0))
    old = pl.BlockSpec((1,) + ck.shape[1:], lambda bi: (bi, 0, 0))
    return pl.pallas_call(
        functools.partial(_attn_a_sample_kernel, lam0=lam0),
        grid=(b,),
        in_specs=[_resident(lp.shape), _resident(sg.shape), new, new, new, old, old],
        out_specs=new,
        out_shape=jax.ShapeDtypeStruct((b, nt, d), BF16),
        compiler_params=_params("parallel"),
        name="attn_a_sample",
    )(lp, sg, q, kn, vn, ck, cv)


def _post_kernel(*refs, final):
    if final:
        x_ref, o_ref, wo_ref, g_ref, w1_ref, w2_ref, gf_ref, out_ref = refs
    else:
        x_ref, o_ref, wo_ref, g_ref, w1_ref, w2_ref, out_ref = refs
    h = x_ref[...] + jnp.dot(o_ref[...], wo_ref[...], preferred_element_type=F32)
    hn = _rms(h, g_ref[...]).astype(BF16)
    d_ff = w1_ref.shape[1]
    mlp = None
    for c in range(d_ff // FF_CHUNK):
        sl = slice(c * FF_CHUNK, (c + 1) * FF_CHUNK)
        a = jnp.dot(hn, w1_ref[:, sl], preferred_element_type=F32)
        u = jnp.square(jnp.maximum(a, 0.0)).astype(BF16)
        part = jnp.dot(u, w2_ref[sl, :], preferred_element_type=F32)
        mlp = part if mlp is None else mlp + part
    h = h + mlp
    if final:
        h = _rms(h, gf_ref[...])
    out_ref[...] = h


def _post(x, o, wo, g, w1, w2, gf=None):
    t, d = x.shape
    tm = TOKEN_TILE
    final = gf is not None
    row = pl.BlockSpec((tm, d), lambda i: (i, 0))
    in_specs = [row, row, _resident(wo.shape), _resident(g.shape), _resident(w1.shape), _resident(w2.shape)]
    args = [x, o, wo, g, w1, w2]
    if final:
        in_specs.append(_resident(gf.shape))
        args.append(gf)
    return pl.pallas_call(
        functools.partial(_post_kernel, final=final),
        grid=(t // tm,),
        in_specs=in_specs,
        out_specs=row,
        out_shape=jax.ShapeDtypeStruct((t, d), F32),
        compiler_params=_params("parallel"),
        name="post_final" if final else "post",
    )(*args)


def _proj_b_kernel(x_ref, gkv_ref, wkv_ref, gq_ref, wq_ref,
                   q_ref, kb_ref, vb_ref, k_ref, v_ref, *, period, v_transposed):
    d = x_ref.shape[1]
    x = x_ref[...]
    xs = x * lax.rsqrt(jnp.mean(x * x, axis=-1, keepdims=True) + EPS)
    xkv = (xs * gkv_ref[...]).astype(BF16)
    xq = (xs * gq_ref[...]).astype(BF16)
    q_ref[...] = (jnp.dot(xq, wq_ref[...], preferred_element_type=F32) * QSCALE).astype(BF16)
    k = jnp.dot(xkv, wkv_ref[:, 0:d], preferred_element_type=F32)
    v = jnp.dot(xkv, wkv_ref[:, d:2 * d], preferred_element_type=F32)
    kb_ref[...] = k.astype(BF16)
    vb_ref[...] = _bf16_values(v, v_transposed)

    @pl.when(pl.program_id(0) % period == period - 1)
    def _():
        k_ref[...] = k
        v_ref[...] = v


def _proj_b(x, gkv, wkv, gq, wq, period, v_transposed):
    t, d = x.shape
    tm = TOKEN_TILE
    row = pl.BlockSpec((tm, d), lambda i: (i, 0))
    keep = pl.BlockSpec((tm, d), lambda i: (i // period, 0))
    vb_spec, vb_shape = _values_out(t, d, tm, v_transposed)
    return pl.pallas_call(
        functools.partial(_proj_b_kernel, period=period, v_transposed=v_transposed),
        grid=(t // tm,),
        in_specs=[row, _resident(gkv.shape), _resident(wkv.shape), _resident(gq.shape), _resident(wq.shape)],
        out_specs=[row, row, vb_spec, keep, keep],
        out_shape=[jax.ShapeDtypeStruct((t, d), BF16), jax.ShapeDtypeStruct((t, d), BF16),
                   vb_shape,
                   jax.ShapeDtypeStruct((t // period, d), F32), jax.ShapeDtypeStruct((t // period, d), F32)],
        compiler_params=_params("arbitrary"),
        name="proj_b",
    )(x, gkv, wkv, gq, wq)


def _toeplitz_bias(r_row, rows, width):
    rb = jnp.broadcast_to(r_row * LOG2E, (rows, r_row.shape[1]))
    return pltpu.roll(rb, 0, 1, stride=1, stride_axis=0)


def _band_prompt_kernel(r_ref, q_ref, k_ref, v_ref, o_ref, vt_ref, bias):
    tq = BAND_TILE
    s_len = q_ref.shape[1]
    win = PAD + tq
    vt_ref[0:LANES, :] = v_ref[...]
    vt_ref[LANES:, :] = jnp.ones((vt_ref.shape[0] - LANES, s_len), BF16)

    @pl.when(pl.program_id(1) == 0)
    def _():
        key = lax.broadcasted_iota(jnp.int32, (win, tq), 0)
        qry = lax.broadcasted_iota(jnp.int32, (win, tq), 1)
        dist = qry // CHUNK + LEFT_CHUNKS - key // CHUNK
        in_band = (dist >= 0) & (dist <= LEFT_CHUNKS)
        for hh in range(2):
            rb = jnp.broadcast_to(r_ref[0, hh:hh + 1, :] * LOG2E, (win, win + tq))
            shifted = pltpu.roll(rb, 0, 1, stride=1, stride_axis=0)
            bias[:, hh * tq:(hh + 1) * tq] = jnp.where(in_band, shifted[:, win:win + tq], NEG_INF)

    lane = lax.broadcasted_iota(jnp.int32, (tq, LANES), 1)

    def window(qi):
        q0 = qi * tq
        return max(0, PAD - q0), max(0, q0 - PAD), q0 + tq

    def scores(qi):
        lo, k_lo, k_hi = window(qi)
        q = q_ref[0, qi * tq:(qi + 1) * tq, :]
        zero = jnp.zeros_like(q)
        qz = jnp.concatenate([jnp.where(lane < HEAD_DIM, q, zero),
                              jnp.where(lane >= HEAD_DIM, q, zero)], axis=0)
        return lax.dot_general(k_ref[0, k_lo:k_hi, :], qz, NT_DIMS,
                               preferred_element_type=F32) + bias[lo:win, :]

    n_tiles = s_len // tq
    s_next = scores(0)
    for qi in range(n_tiles):
        s = s_next
        if qi + 1 < n_tiles:
            s_next = scores(qi + 1)
        _, k_lo, k_hi = window(qi)
        m = jnp.max(s, axis=0, keepdims=True)
        e = jnp.exp2(s - m)
        o = jnp.dot(vt_ref[:, k_lo:k_hi], e.astype(BF16), preferred_element_type=F32)
        o = o[0:LANES] / o[LANES:LANES + 1]
        o = jnp.concatenate([o[0:HEAD_DIM, 0:tq], o[HEAD_DIM:LANES, tq:2 * tq]], axis=0)
        o_ref[0, qi * tq:(qi + 1) * tq, :] = o.T.astype(BF16)


def _band_prompt(r, q, k, v):
    b, s, d = q.shape
    blk = pl.BlockSpec((1, s, LANES), lambda pi, bi: (bi, 0, pi))
    vt_blk = pl.BlockSpec((LANES, s), lambda pi, bi: (pi, bi))
    rblk = pl.BlockSpec((1, 2, r.shape[2]), lambda pi, bi: (pi, 0, 0))
    win = PAD + BAND_TILE
    return pl.pallas_call(
        _band_prompt_kernel,
        grid=(d // LANES, b),
        in_specs=[rblk, blk, blk, vt_blk],
        out_specs=blk,
        out_shape=jax.ShapeDtypeStruct((b, s, d), BF16),
        scratch_shapes=[pltpu.VMEM((LANES + ONES_ROWS, s), BF16), pltpu.VMEM((win, 2 * BAND_TILE), F32)],
        compiler_params=_params("arbitrary", "arbitrary"),
        name="band_prompt",
    )(r, q, k, v)


def _band_sample_kernel(r_ref, q_ref, kn_ref, vn_ref, ck_ref, cv_ref, o_ref):
    nt, d = q_ref.shape[1], q_ref.shape[2]
    lb = ck_ref.shape[1]
    off = BAND_Q_TILE
    lane = lax.broadcasted_iota(jnp.int32, (nt, LANES), 1)
    pairs = d // LANES

    def scores(p):
        sl = slice(p * LANES, (p + 1) * LANES)
        q = q_ref[0, :, sl]
        zero = jnp.zeros_like(q)
        qz = jnp.concatenate([jnp.where(lane < HEAD_DIM, q, zero),
                              jnp.where(lane >= HEAD_DIM, q, zero)], axis=0)
        shifted = jnp.concatenate([_toeplitz_bias(r_ref[p, hh:hh + 1, :], nt, lb + nt)
                                   for hh in range(2)], axis=0)
        sc = (lax.dot_general(qz, ck_ref[0, :, sl].astype(BF16), NT_DIMS, preferred_element_type=F32)
              + shifted[:, off:off + lb])
        sn = (lax.dot_general(qz, kn_ref[0, :, sl], NT_DIMS, preferred_element_type=F32)
              + shifted[:, off + lb:off + lb + nt])
        return sc, sn

    def weights(sc, sn):
        m = jnp.maximum(jnp.max(sc, axis=-1, keepdims=True), jnp.max(sn, axis=-1, keepdims=True))
        ec = jnp.exp2(sc - m)
        en = jnp.exp2(sn - m)
        inv = 1.0 / (jnp.sum(ec, axis=-1, keepdims=True) + jnp.sum(en, axis=-1, keepdims=True))
        return (ec * inv).astype(BF16), (en * inv).astype(BF16)

    def output(p, pc, pn):
        sl = slice(p * LANES, (p + 1) * LANES)
        o = (jnp.dot(pc, cv_ref[0, :, sl].astype(BF16), preferred_element_type=F32)
             + jnp.dot(pn, vn_ref[0, :, sl], preferred_element_type=F32))
        o_ref[0, :, sl] = jnp.where(lane < HEAD_DIM, o[0:nt], o[nt:2 * nt]).astype(BF16)

    s_next = scores(0)
    w_prev = None
    for p in range(pairs):
        s = s_next
        if p + 1 < pairs:
            s_next = scores(p + 1)
        w = weights(*s)
        if w_prev is not None:
            output(p - 1, *w_prev)
        w_prev = w
    output(pairs - 1, *w_prev)


def _band_sample(r, q, kn, vn, ck, cv):
    b, nt, d = q.shape
    lb = ck.shape[1]
    new = pl.BlockSpec((1, nt, d), lambda bi: (bi, 0, 0))
    old = pl.BlockSpec((1, lb, d), lambda bi: (bi, 0, 0))
    return pl.pallas_call(
        _band_sample_kernel,
        grid=(b,),
        in_specs=[_resident(r.shape), new, new, new, old, old],
        out_specs=new,
        out_shape=jax.ShapeDtypeStruct((b, nt, d), BF16),
        compiler_params=_params("parallel"),
        name="band_sample",
    )(r, q, kn, vn, ck, cv)


def _rope_tables(pos):
    half = HEAD_DIM // 2
    inv = 1.0 / (ROPE_THETA ** (jnp.arange(half, dtype=F32) / half))
    ang = pos.astype(F32)[:, None] * inv[None, :]
    cos, sin = jnp.cos(ang), jnp.sin(ang)
    reps = LANES // HEAD_DIM
    return (jnp.tile(jnp.concatenate([cos, cos], axis=-1), (1, reps)),
            jnp.tile(jnp.concatenate([-sin, sin], axis=-1), (1, reps)))


def _bias_rows(table):
    far = PAD + BAND_Q_TILE - REL_CLIP
    near = table[:, 2 * REL_CLIP:0:-1]
    rows = jnp.concatenate([jnp.broadcast_to(table[:, 2 * REL_CLIP:], (table.shape[0], far)), near], axis=1)
    return rows.reshape(table.shape[0] // 2, 2, rows.shape[1])


def _bias_rows_t(table):
    heads = table.shape[0]
    lo = BAND_TILE - REL_CLIP
    hi = PAD + 2 * BAND_TILE - lo - (2 * REL_CLIP + 1)
    rows = jnp.concatenate([jnp.broadcast_to(table[:, :1], (heads, lo)), table,
                            jnp.broadcast_to(table[:, 2 * REL_CLIP:], (heads, hi))], axis=1)
    return rows.reshape(heads // 2, 2, rows.shape[1])


def _lambda_init(layer):
    return 0.8 - 0.6 * math.exp(-0.3 * layer)


def kernel(x_prompt, x_sample, cache_a_k, cache_a_v, cache_b_k, cache_b_v, g_attn, w_a_qkv, a_lambda, a_subln, w_a_o, g_kv, w_kv, w_b_q, b_rel, w_b_o, g_mlp, w_ff1, w_ff2, g_final):
    nb, seq, d = x_prompt.shape
    db, nt, _ = x_sample.shape
    past = cache_a_k.shape[2]
    lb = cache_b_k.shape[1]
    a_heads = d // (2 * HEAD_DIM)
    b_heads = d // HEAD_DIM
    assert w_a_qkv.shape[0] == 1 and w_b_q.shape[0] == 1, "one differential and one band layer"
    assert (db * nt) % TOKEN_TILE == 0 and TOKEN_TILE % nt == 0 and seq % TOKEN_TILE == 0
    keep = min(PAD, seq)
    assert keep == TOKEN_TILE and lb == PAD

    lam0 = _lambda_init(0)
    w_qkv = w_a_qkv[0].astype(BF16)
    w_ao = w_a_o[0].astype(BF16)
    w_kvb = w_kv.astype(BF16)
    w_bq = w_b_q[0].astype(BF16)
    w_bo = w_b_o[0].astype(BF16)
    w1 = [w_ff1[l].astype(BF16) for l in range(w_ff1.shape[0])]
    w2 = [w_ff2[l].astype(BF16) for l in range(w_ff2.shape[0])]
    g_a = g_attn[0][None]
    g_b = g_attn[1][None]
    g_k = g_kv[None]
    g_m0 = g_mlp[0][None]
    g_m1 = g_mlp[1][None]
    g_f = g_final[None]
    lp = a_lambda[0]
    sg = a_subln[0][None]
    r = _bias_rows(b_rel[0])
    r_t = _bias_rows_t(b_rel[0])

    cos_p, sin_p = _rope_tables(jnp.arange(seq))
    cos_s, sin_s = _rope_tables(jnp.arange(past, past + nt))
    reps = TOKEN_TILE // nt
    cos_s, sin_s = jnp.tile(cos_s, (reps, 1)), jnp.tile(sin_s, (reps, 1))

    def trunk(x, cos, sin, attn_a, attn_b, period, v_transposed):
        q, k, kb, v, vb = _proj_a(x, g_a, w_qkv, cos, sin, v_transposed)
        o = attn_a(q, kb, vb)
        h = _post(x, o, w_ao, g_m0, w1[0], w2[0])
        qb, kbb, vbb, k_sh, v_sh = _proj_b(h, g_k, w_kvb, g_b, w_bq, period, v_transposed)
        ob = attn_b(qb, kbb, vbb)
        y = _post(h, ob, w_bo, g_m1, w1[1], w2[1], g_f)
        return y, k, v, k_sh, v_sh

    def attn_a_p(q, kb, vt):
        sh = (nb, seq, d)
        return _attn_a_prompt(lp, sg, q.reshape(sh), kb.reshape(sh), vt, lam0).reshape(nb * seq, d)

    def attn_b_p(q, kb, vt):
        sh = (nb, seq, d)
        return _band_prompt(r_t, q.reshape(sh), kb.reshape(sh), vt).reshape(nb * seq, d)

    ck_a = cache_a_k.reshape(db, past * a_heads, 2 * HEAD_DIM)
    cv_a = cache_a_v.reshape(db, past * a_heads, 2 * HEAD_DIM)
    ck_b = cache_b_k.reshape(db, lb, d)
    cv_b = cache_b_v.reshape(db, lb, d)

    def attn_a_s(q, kb, vb):
        sh = (db, nt, d)
        return _attn_a_sample(lp, sg, q.reshape(sh), kb.reshape(sh), vb.reshape(sh), ck_a, cv_a, lam0).reshape(db * nt, d)

    def attn_b_s(q, kb, vb):
        sh = (db, nt, d)
        return _band_sample(r, q.reshape(sh), kb.reshape(sh), vb.reshape(sh), ck_b, cv_b).reshape(db * nt, d)

    y_p, ak_p, av_p, bk_p, bv_p = trunk(x_prompt.reshape(nb * seq, d), cos_p, sin_p, attn_a_p, attn_b_p,
                                        seq // TOKEN_TILE, True)
    y_s, ak_s, av_s, bk_s, bv_s = trunk(x_sample.reshape(db * nt, d), cos_s, sin_s, attn_a_s, attn_b_s, 1,
                                        False)

    return (y_p.reshape(nb, seq, d), y_s.reshape(db, nt, d),
            ak_p.reshape(1, nb, seq, a_heads, 2 * HEAD_DIM), av_p.reshape(1, nb, seq, a_heads, 2 * HEAD_DIM),
            bk_p.reshape(nb, keep, b_heads, HEAD_DIM), bv_p.reshape(nb, keep, b_heads, HEAD_DIM),
            ak_s.reshape(1, db, nt, a_heads, 2 * HEAD_DIM), av_s.reshape(1, db, nt, a_heads, 2 * HEAD_DIM),
            bk_s.reshape(db, nt, b_heads, HEAD_DIM), bv_s.reshape(db, nt, b_heads, HEAD_DIM))
```

```python
import functools
import math
from typing import Callable, NamedTuple

import jax
import jax.numpy as jnp
from jax import lax
from jax.experimental import pallas as pl
from jax.experimental.pallas import tpu as pltpu

HEAD_DIM = 64
CHUNK = 64
LEFT_CHUNKS = 8
PAD = LEFT_CHUNKS * CHUNK
REL_CLIP = 128
ROPE_THETA = 10000.0
EPS = 1e-6
NEG_INF = -1e30
LANES = 128
LOG2E = math.log2(math.e)
QSCALE = HEAD_DIM ** -0.5 * LOG2E

TOKEN_TILE = 512
FF_CHUNK = 1024
ATTN_TILE = 512
ONES_ROWS = 16
BAND_TILE = 256
BAND_Q_TILE = 128
VMEM_LIMIT = 56 * 1024 * 1024

F32 = jnp.float32
BF16 = jnp.bfloat16
NT_DIMS = (((1,), (1,)), ((), ()))


def _rms(x, g):
    ms = jnp.mean(x * x, axis=-1, keepdims=True)
    return x * lax.rsqrt(ms + EPS) * g


def _resident(shape):
    return pl.BlockSpec(shape, lambda *_: (0,) * len(shape), pipeline_mode=pl.Buffered(1))


def _params(*sem):
    return pltpu.CompilerParams(dimension_semantics=sem, vmem_limit_bytes=VMEM_LIMIT)


class _Call(NamedTuple):
    body: Callable
    grid: tuple
    in_specs: list
    args: list
    out_specs: list
    out_shape: list
    name: str


def _run(stages):
    for _ in stages:
        pass


def _launch(call):
    outs = pl.pallas_call(lambda *refs: _run(call.body(*refs)), grid=call.grid, in_specs=call.in_specs,
                          out_specs=call.out_specs, out_shape=call.out_shape,
                          compiler_params=_params(*["parallel"] * len(call.grid)),
                          name=call.name)(*call.args)
    return outs


def _launch_pair(heavy, light):
    if heavy.grid != light.grid:
        return _launch(heavy), _launch(light)
    n_in_h, n_in_l, n_out_h = len(heavy.args), len(light.args), len(heavy.out_shape)

    def body(*refs):
        ins, outs = refs[:n_in_h + n_in_l], refs[n_in_h + n_in_l:]
        a = light.body(*ins[n_in_h:], *outs[n_out_h:])
        b = heavy.body(*ins[:n_in_h], *outs[:n_out_h])
        live = [a, b]
        while live:
            for g in list(live):
                if next(g, StopIteration) is StopIteration:
                    live.remove(g)

    outs = pl.pallas_call(body, grid=heavy.grid, in_specs=heavy.in_specs + light.in_specs,
                          out_specs=heavy.out_specs + light.out_specs,
                          out_shape=heavy.out_shape + light.out_shape,
                          compiler_params=_params(*["parallel"] * len(heavy.grid)),
                          name=heavy.name + "_" + light.name)(*heavy.args, *light.args)
    return outs[:n_out_h], outs[n_out_h:]


def _diff_lambda(lp, lam0):
    a = jnp.sum(lp[0:1] * lp[1:2], axis=-1, keepdims=True)
    b = jnp.sum(lp[2:3] * lp[3:4], axis=-1, keepdims=True)
    return jnp.exp(a) - jnp.exp(b) + lam0


def _head_norm(o, g, lam0):
    return _rms(o, g) * (1.0 - lam0)


def _bf16_values(v, transposed):
    return (v.T if transposed else v).astype(BF16)


def _proj_a_kernel(x_ref, g_ref, w_ref, cos_ref, sin_ref,
                   q_ref, k_ref, kb_ref, v_ref, vb_ref, *, v_transposed):
    d = x_ref.shape[1]
    xn = _rms(x_ref[...], g_ref[...]).astype(BF16)
    cos = cos_ref[...]
    sin = sin_ref[...]
    lane = lax.broadcasted_iota(jnp.int32, cos.shape, 1)
    first_half = (lane % HEAD_DIM) < (HEAD_DIM // 2)

    def rope(a):
        partner = jnp.where(first_half, pltpu.roll(a, LANES - HEAD_DIM // 2, 1),
                            pltpu.roll(a, HEAD_DIM // 2, 1))
        return a * cos + partner * sin

    q = jnp.dot(xn, w_ref[:, 0:d], preferred_element_type=F32)
    for c in range(d // LANES):
        sl = slice(c * LANES, (c + 1) * LANES)
        q_ref[:, sl] = (rope(q[:, sl]) * QSCALE).astype(BF16)
    tm, heads = x_ref.shape[0], d // LANES
    k = jnp.dot(xn, w_ref[:, d:2 * d], preferred_element_type=F32)
    for c in range(heads):
        sl = slice(c * LANES, (c + 1) * LANES)
        kr = rope(k[:, sl])
        k_ref[pl.ds(c, tm, stride=heads), :] = kr
        kb_ref[:, sl] = kr.astype(BF16)
    v = jnp.dot(xn, w_ref[:, 2 * d:3 * d], preferred_element_type=F32)
    for c in range(heads):
        v_ref[pl.ds(c, tm, stride=heads), :] = v[:, c * LANES:(c + 1) * LANES]
    vb_ref[...] = _bf16_values(v, v_transposed)


def _values_out(t, d, tm, transposed):
    if transposed:
        return pl.BlockSpec((d, tm), lambda i: (0, i)), jax.ShapeDtypeStruct((d, t), BF16)
    return pl.BlockSpec((tm, d), lambda i: (i, 0)), jax.ShapeDtypeStruct((t, d), BF16)


def _proj_a(x, g, w, cos, sin, v_transposed):
    t, d = x.shape
    tm = TOKEN_TILE
    period = cos.shape[0] // tm
    heads = d // LANES
    row = pl.BlockSpec((tm, d), lambda i: (i, 0))
    native = pl.BlockSpec((tm * heads, LANES), lambda i: (i, 0))
    tab = pl.BlockSpec((tm, LANES), lambda i: (i % period, 0))
    vb_spec, vb_shape = _values_out(t, d, tm, v_transposed)
    return pl.pallas_call(
        functools.partial(_proj_a_kernel, v_transposed=v_transposed),
        grid=(t // tm,),
        in_specs=[row, _resident((1, d)), _resident((d, 3 * d)), tab, tab],
        out_specs=[row, native, row, native, vb_spec],
        out_shape=[jax.ShapeDtypeStruct((t, d), BF16), jax.ShapeDtypeStruct((t * heads, LANES), F32),
                   jax.ShapeDtypeStruct((t, d), BF16), jax.ShapeDtypeStruct((t * heads, LANES), F32),
                   vb_shape],
        compiler_params=_params("parallel"),
        name="proj_a",
    )(x, g, w, cos, sin)


def _attn_a_prompt_kernel(lp_ref, sg_ref, q_ref, k_ref, v_ref, o_ref, vt_ref, *, lam0):
    t = ATTN_TILE
    s_len = q_ref.shape[1]
    lam = _diff_lambda(lp_ref[...], lam0)
    sg = sg_ref[...]
    vt_ref[0:LANES, :] = v_ref[...]

    vt_ref[LANES:, :] = jnp.ones((vt_ref.shape[0] - LANES, s_len), BF16)

    th = t // 2
    lane = lax.broadcasted_iota(jnp.int32, (th, LANES), 1)
    key_chunk = lax.broadcasted_iota(jnp.int32, (th, t), 0) // CHUNK
    qry_chunk = (lax.broadcasted_iota(jnp.int32, (th, t), 1) % th) // CHUNK
    diag_half = qry_chunk >= key_chunk

    def query_rows(qi, halves):
        rows = []
        for u in halves:
            q = q_ref[0, qi * t + u * th:qi * t + (u + 1) * th, :]
            zero = jnp.zeros_like(q)
            rows += [jnp.where(lane < HEAD_DIM, q, zero), jnp.where(lane >= HEAD_DIM, q, zero)]
        return jnp.concatenate(rows, axis=0)

    def scores(qi, j):
        if j < qi:
            return lax.dot_general(k_ref[0, j * t:(j + 1) * t, :], query_rows(qi, (0, 1)), NT_DIMS,
                                   preferred_element_type=F32)
        top = lax.dot_general(k_ref[0, qi * t:qi * t + th, :], query_rows(qi, (0, 1)), NT_DIMS,
                              preferred_element_type=F32)
        bot = lax.dot_general(k_ref[0, qi * t + th:(qi + 1) * t, :], query_rows(qi, (1,)), NT_DIMS,
                              preferred_element_type=F32)
        first = jnp.where(diag_half, top[:, 0:t], NEG_INF)
        second = jnp.concatenate([top[:, t:2 * t], jnp.where(diag_half, bot, NEG_INF)], axis=0)
        return first, second

    def update(s, k_lo, k_hi, m, acc):
        m_new = jnp.maximum(m, jnp.max(s, axis=0, keepdims=True))
        p = jnp.exp2(s - m_new)
        acc = jnp.exp2(m - m_new) * acc + jnp.dot(vt_ref[:, k_lo:k_hi], p.astype(BF16),
                                                  preferred_element_type=F32)
        return m_new, acc

    pairs = [(qi, j) for qi in range(s_len // t) for j in range(qi + 1)]
    s_next = scores(*pairs[0])
    for n, (qi, j) in enumerate(pairs):
        s = s_next
        if n + 1 < len(pairs):
            s_next = scores(*pairs[n + 1])
        if j == 0:
            m = jnp.full((1, 2 * t), NEG_INF, F32)
            acc = jnp.zeros((vt_ref.shape[0], 2 * t), F32)
        if j < qi:
            m, acc = update(s, j * t, (j + 1) * t, m, acc)
        else:
            _, acc0 = update(s[0], qi * t, qi * t + th, m[:, 0:t], acc[:, 0:t])
            _, acc1 = update(s[1], qi * t, (qi + 1) * t, m[:, t:2 * t], acc[:, t:2 * t])
            halves = []
            for a in (acc0, acc1):
                a = a[0:LANES] / a[LANES:LANES + 1]
                halves.append(a[:, 0:th] - lam * a[:, th:t])
            o = jnp.concatenate(halves, axis=1)
            o = o * lax.rsqrt(jnp.mean(o * o, axis=0, keepdims=True) + EPS)
            o_ref[0, qi * t:(qi + 1) * t, :] = (o.T * sg * (1.0 - lam0)).astype(BF16)


def _attn_a_prompt(lp, sg, q, k, v, lam0):
    b, s, d = q.shape
    blk = pl.BlockSpec((1, s, LANES), lambda bi, hi: (bi, 0, hi))
    vt_blk = pl.BlockSpec((LANES, s), lambda bi, hi: (hi, bi))
    return pl.pallas_call(
        functools.partial(_attn_a_prompt_kernel, lam0=lam0),
        grid=(b, d // LANES),
        in_specs=[_resident(lp.shape), _resident(sg.shape), blk, blk, vt_blk],
        out_specs=blk,
        out_shape=jax.ShapeDtypeStruct((b, s, d), BF16),
        scratch_shapes=[pltpu.VMEM((LANES + ONES_ROWS, s), BF16)],
        compiler_params=_params("parallel", "parallel"),
        name="attn_a_prompt",
    )(lp, sg, q, k, v)


def _attn_a_sample_kernel(lp_ref, sg_ref, q_ref, kn_ref, vn_ref, ck_ref, cv_ref, o_ref, *, lam0):
    nt, d = q_ref.shape[1], q_ref.shape[2]
    heads = d // LANES
    past = ck_ref.shape[1] // heads
    lam = _diff_lambda(lp_ref[...], lam0)
    sg = sg_ref[...]
    lane = lax.broadcasted_iota(jnp.int32, (nt, LANES), 1)

    def scores(h):
        sl = slice(h * LANES, (h + 1) * LANES)
        q = q_ref[0, :, sl]
        zero = jnp.zeros_like(q)
        qz = jnp.concatenate([jnp.where(lane < HEAD_DIM, q, zero),
                              jnp.where(lane >= HEAD_DIM, q, zero)], axis=0)
        kc = ck_ref[0, pl.ds(h, past, stride=heads), :].astype(BF16)
        return (lax.dot_general(qz, kc, NT_DIMS, preferred_element_type=F32),
                lax.dot_general(qz, kn_ref[0, :, sl], NT_DIMS, preferred_element_type=F32))

    def weights(sc, sn):
        m = jnp.maximum(jnp.max(sc, axis=-1, keepdims=True), jnp.max(sn, axis=-1, keepdims=True))
        ec = jnp.exp2(sc - m)
        en = jnp.exp2(sn - m)
        inv = 1.0 / (jnp.sum(ec, axis=-1, keepdims=True) + jnp.sum(en, axis=-1, keepdims=True))
        pc = ec * inv
        pn = en * inv
        return ((pc[0:nt] - lam * pc[nt:2 * nt]).astype(BF16),
                (pn[0:nt] - lam * pn[nt:2 * nt]).astype(BF16))

    def output(h, pc, pn):
        sl = slice(h * LANES, (h + 1) * LANES)
        vc = cv_ref[0, pl.ds(h, past, stride=heads), :].astype(BF16)
        o = (jnp.dot(pc, vc, preferred_element_type=F32)
             + jnp.dot(pn, vn_ref[0, :, sl], preferred_element_type=F32))
        o_ref[0, :, sl] = _head_norm(o, sg, lam0).astype(BF16)

    s_next = scores(0)
    p_prev = None
    for h in range(heads):
        s = s_next
        if h + 1 < heads:
            s_next = scores(h + 1)
        p = weights(*s)
        if p_prev is not None:
            output(h - 1, *p_prev)
        p_prev = p
        yield
    output(heads - 1, *p_prev)


def _attn_a_sample(lp, sg, q, kn, vn, ck, cv, lam0):
    b, nt, d = q.shape
    new = pl.BlockSpec((1, nt, d), lambda bi: (bi, 0, 0))
    old = pl.BlockSpec((1,) + ck.shape[1:], lambda bi: (bi, 0, 0))
    return _Call(functools.partial(_attn_a_sample_kernel, lam0=lam0), (b,),
                 [_resident(lp.shape), _resident(sg.shape), new, new, new, old, old],
                 [lp, sg, q, kn, vn, ck, cv], [new], [jax.ShapeDtypeStruct((b, nt, d), BF16)],
                 "attn_a_sample")


def _post_kernel(*refs, final):
    if final:
        x_ref, o_ref, wo_ref, g_ref, w1_ref, w2_ref, gf_ref, out_ref = refs
    else:
        x_ref, o_ref, wo_ref, g_ref, w1_ref, w2_ref, out_ref = refs
    h = x_ref[...] + jnp.dot(o_ref[...], wo_ref[...], preferred_element_type=F32)
    hn = _rms(h, g_ref[...]).astype(BF16)
    yield
    d_ff = w1_ref.shape[1]
    mlp = None
    for c in range(d_ff // FF_CHUNK):
        sl = slice(c * FF_CHUNK, (c + 1) * FF_CHUNK)
        a = jnp.dot(hn, w1_ref[:, sl], preferred_element_type=F32)
        u = jnp.square(jnp.maximum(a, 0.0)).astype(BF16)
        yield
        part = jnp.dot(u, w2_ref[sl, :], preferred_element_type=F32)
        mlp = part if mlp is None else mlp + part
        yield
    h = h + mlp
    if final:
        h = _rms(h, gf_ref[...])
    out_ref[...] = h


def _post(x, o, wo, g, w1, w2, gf=None):
    t, d = x.shape
    tm = TOKEN_TILE
    final = gf is not None
    row = pl.BlockSpec((tm, d), lambda i: (i, 0))
    in_specs = [row, row, _resident(wo.shape), _resident(g.shape), _resident(w1.shape), _resident(w2.shape)]
    args = [x, o, wo, g, w1, w2]
    if final:
        in_specs.append(_resident(gf.shape))
        args.append(gf)
    return _Call(functools.partial(_post_kernel, final=final), (t // tm,), in_specs, args,
                 [row], [jax.ShapeDtypeStruct((t, d), F32)], "post_final" if final else "post")


def _proj_b_kernel(x_ref, gkv_ref, wkv_ref, gq_ref, wq_ref,
                   q_ref, kb_ref, vb_ref, k_ref, v_ref, *, period, v_transposed):
    d = x_ref.shape[1]
    x = x_ref[...]
    xs = x * lax.rsqrt(jnp.mean(x * x, axis=-1, keepdims=True) + EPS)
    xkv = (xs * gkv_ref[...]).astype(BF16)
    xq = (xs * gq_ref[...]).astype(BF16)
    q_ref[...] = (jnp.dot(xq, wq_ref[...], preferred_element_type=F32) * QSCALE).astype(BF16)
    k = jnp.dot(xkv, wkv_ref[:, 0:d], preferred_element_type=F32)
    v = jnp.dot(xkv, wkv_ref[:, d:2 * d], preferred_element_type=F32)
    kb_ref[...] = k.astype(BF16)
    vb_ref[...] = _bf16_values(v, v_transposed)

    @pl.when(pl.program_id(0) % period == period - 1)
    def _():
        k_ref[...] = k
        v_ref[...] = v


def _proj_b(x, gkv, wkv, gq, wq, period, v_transposed):
    t, d = x.shape
    tm = TOKEN_TILE
    row = pl.BlockSpec((tm, d), lambda i: (i, 0))
    keep = pl.BlockSpec((tm, d), lambda i: (i // period, 0))
    vb_spec, vb_shape = _values_out(t, d, tm, v_transposed)
    return pl.pallas_call(
        functools.partial(_proj_b_kernel, period=period, v_transposed=v_transposed),
        grid=(t // tm,),
        in_specs=[row, _resident(gkv.shape), _resident(wkv.shape), _resident(gq.shape), _resident(wq.shape)],
        out_specs=[row, row, vb_spec, keep, keep],
        out_shape=[jax.ShapeDtypeStruct((t, d), BF16), jax.ShapeDtypeStruct((t, d), BF16),
                   vb_shape,
                   jax.ShapeDtypeStruct((t // period, d), F32), jax.ShapeDtypeStruct((t // period, d), F32)],
        compiler_params=_params("arbitrary"),
        name="proj_b",
    )(x, gkv, wkv, gq, wq)


def _toeplitz_bias(r_row, rows, width):
    rb = jnp.broadcast_to(r_row * LOG2E, (rows, r_row.shape[1]))
    return pltpu.roll(rb, 0, 1, stride=1, stride_axis=0)


def _band_prompt_kernel(r_ref, q_ref, k_ref, v_ref, o_ref, vt_ref, bias):
    tq = BAND_TILE
    s_len = q_ref.shape[1]
    win = PAD + tq
    vt_ref[0:LANES, :] = v_ref[...]
    vt_ref[LANES:, :] = jnp.ones((vt_ref.shape[0] - LANES, s_len), BF16)

    @pl.when(pl.program_id(1) == 0)
    def _():
        key = lax.broadcasted_iota(jnp.int32, (win, tq), 0)
        qry = lax.broadcasted_iota(jnp.int32, (win, tq), 1)
        dist = qry // CHUNK + LEFT_CHUNKS - key // CHUNK
        in_band = (dist >= 0) & (dist <= LEFT_CHUNKS)
        for hh in range(2):
            rb = jnp.broadcast_to(r_ref[0, hh:hh + 1, :] * LOG2E, (win, win + tq))
            shifted = pltpu.roll(rb, 0, 1, stride=1, stride_axis=0)
            bias[:, hh * tq:(hh + 1) * tq] = jnp.where(in_band, shifted[:, win:win + tq], NEG_INF)

    lane = lax.broadcasted_iota(jnp.int32, (tq, LANES), 1)

    def window(qi):
        q0 = qi * tq
        return max(0, PAD - q0), max(0, q0 - PAD), q0 + tq

    def scores(qi):
        lo, k_lo, k_hi = window(qi)
        q = q_ref[0, qi * tq:(qi + 1) * tq, :]
        zero = jnp.zeros_like(q)
        qz = jnp.concatenate([jnp.where(lane < HEAD_DIM, q, zero),
                              jnp.where(lane >= HEAD_DIM, q, zero)], axis=0)
        return lax.dot_general(k_ref[0, k_lo:k_hi, :], qz, NT_DIMS,
                               preferred_element_type=F32) + bias[lo:win, :]

    n_tiles = s_len // tq
    s_next = scores(0)
    for qi in range(n_tiles):
        s = s_next
        if qi + 1 < n_tiles:
            s_next = scores(qi + 1)
        _, k_lo, k_hi = window(qi)
        m = jnp.max(s, axis=0, keepdims=True)
        e = jnp.exp2(s - m)
        o = jnp.dot(vt_ref[:, k_lo:k_hi], e.astype(BF16), preferred_element_type=F32)
        o = o[0:LANES] / o[LANES:LANES + 1]
        o = jnp.concatenate([o[0:HEAD_DIM, 0:tq], o[HEAD_DIM:LANES, tq:2 * tq]], axis=0)
        o_ref[0, qi * tq:(qi + 1) * tq, :] = o.T.astype(BF16)


def _band_prompt(r, q, k, v):
    b, s, d = q.shape
    blk = pl.BlockSpec((1, s, LANES), lambda pi, bi: (bi, 0, pi))
    vt_blk = pl.BlockSpec((LANES, s), lambda pi, bi: (pi, bi))
    rblk = pl.BlockSpec((1, 2, r.shape[2]), lambda pi, bi: (pi, 0, 0))
    win = PAD + BAND_TILE
    return pl.pallas_call(
        _band_prompt_kernel,
        grid=(d // LANES, b),
        in_specs=[rblk, blk, blk, vt_blk],
        out_specs=blk,
        out_shape=jax.ShapeDtypeStruct((b, s, d), BF16),
        scratch_shapes=[pltpu.VMEM((LANES + ONES_ROWS, s), BF16), pltpu.VMEM((win, 2 * BAND_TILE), F32)],
        compiler_params=_params("arbitrary", "arbitrary"),
        name="band_prompt",
    )(r, q, k, v)


def _band_sample_kernel(r_ref, q_ref, kn_ref, vn_ref, ck_ref, cv_ref, o_ref):
    nt, d = q_ref.shape[1], q_ref.shape[2]
    lb = ck_ref.shape[1]
    off = BAND_Q_TILE
    lane = lax.broadcasted_iota(jnp.int32, (nt, LANES), 1)
    pairs = d // LANES

    def scores(p):
        sl = slice(p * LANES, (p + 1) * LANES)
        q = q_ref[0, :, sl]
        zero = jnp.zeros_like(q)
        qz = jnp.concatenate([jnp.where(lane < HEAD_DIM, q, zero),
                              jnp.where(lane >= HEAD_DIM, q, zero)], axis=0)
        shifted = jnp.concatenate([_toeplitz_bias(r_ref[p, hh:hh + 1, :], nt, lb + nt)
                                   for hh in range(2)], axis=0)
        sc = (lax.dot_general(qz, ck_ref[0, :, sl].astype(BF16), NT_DIMS, preferred_element_type=F32)
              + shifted[:, off:off + lb])
        sn = (lax.dot_general(qz, kn_ref[0, :, sl], NT_DIMS, preferred_element_type=F32)
              + shifted[:, off + lb:off + lb + nt])
        return sc, sn

    def weights(sc, sn):
        m = jnp.maximum(jnp.max(sc, axis=-1, keepdims=True), jnp.max(sn, axis=-1, keepdims=True))
        ec = jnp.exp2(sc - m)
        en = jnp.exp2(sn - m)
        inv = 1.0 / (jnp.sum(ec, axis=-1, keepdims=True) + jnp.sum(en, axis=-1, keepdims=True))
        return (ec * inv).astype(BF16), (en * inv).astype(BF16)

    def output(p, pc, pn):
        sl = slice(p * LANES, (p + 1) * LANES)
        o = (jnp.dot(pc, cv_ref[0, :, sl].astype(BF16), preferred_element_type=F32)
             + jnp.dot(pn, vn_ref[0, :, sl], preferred_element_type=F32))
        o_ref[0, :, sl] = jnp.where(lane < HEAD_DIM, o[0:nt], o[nt:2 * nt]).astype(BF16)

    s_next = scores(0)
    w_prev = None
    for p in range(pairs):
        s = s_next
        if p + 1 < pairs:
            s_next = scores(p + 1)
        w = weights(*s)
        if w_prev is not None:
            output(p - 1, *w_prev)
        w_prev = w
        yield
    output(pairs - 1, *w_prev)


def _band_sample(r, q, kn, vn, ck, cv):
    b, nt, d = q.shape
    lb = ck.shape[1]
    new = pl.BlockSpec((1, nt, d), lambda bi: (bi, 0, 0))
    old = pl.BlockSpec((1, lb, d), lambda bi: (bi, 0, 0))
    return _Call(_band_sample_kernel, (b,), [_resident(r.shape), new, new, new, old, old],
                 [r, q, kn, vn, ck, cv], [new], [jax.ShapeDtypeStruct((b, nt, d), BF16)], "band_sample")


def _rope_tables(pos):
    half = HEAD_DIM // 2
    inv = 1.0 / (ROPE_THETA ** (jnp.arange(half, dtype=F32) / half))
    ang = pos.astype(F32)[:, None] * inv[None, :]
    cos, sin = jnp.cos(ang), jnp.sin(ang)
    reps = LANES // HEAD_DIM
    return (jnp.tile(jnp.concatenate([cos, cos], axis=-1), (1, reps)),
            jnp.tile(jnp.concatenate([-sin, sin], axis=-1), (1, reps)))


def _bias_rows(table):
    far = PAD + BAND_Q_TILE - REL_CLIP
    near = table[:, 2 * REL_CLIP:0:-1]
    rows = jnp.concatenate([jnp.broadcast_to(table[:, 2 * REL_CLIP:], (table.shape[0], far)), near], axis=1)
    return rows.reshape(table.shape[0] // 2, 2, rows.shape[1])


def _bias_rows_t(table):
    heads = table.shape[0]
    lo = BAND_TILE - REL_CLIP
    hi = PAD + 2 * BAND_TILE - lo - (2 * REL_CLIP + 1)
    rows = jnp.concatenate([jnp.broadcast_to(table[:, :1], (heads, lo)), table,
                            jnp.broadcast_to(table[:, 2 * REL_CLIP:], (heads, hi))], axis=1)
    return rows.reshape(heads // 2, 2, rows.shape[1])


def _lambda_init(layer):
    return 0.8 - 0.6 * math.exp(-0.3 * layer)


def kernel(x_prompt, x_sample, cache_a_k, cache_a_v, cache_b_k, cache_b_v, g_attn, w_a_qkv, a_lambda, a_subln, w_a_o, g_kv, w_kv, w_b_q, b_rel, w_b_o, g_mlp, w_ff1, w_ff2, g_final):
    nb, seq, d = x_prompt.shape
    db, nt, _ = x_sample.shape
    past = cache_a_k.shape[2]
    lb = cache_b_k.shape[1]
    a_heads = d // (2 * HEAD_DIM)
    b_heads = d // HEAD_DIM
    assert w_a_qkv.shape[0] == 1 and w_b_q.shape[0] == 1, "one differential and one band layer"
    assert (db * nt) % TOKEN_TILE == 0 and TOKEN_TILE % nt == 0 and seq % TOKEN_TILE == 0
    keep = min(PAD, seq)
    assert keep == TOKEN_TILE and lb == PAD

    lam0 = _lambda_init(0)
    w_qkv = w_a_qkv[0].astype(BF16)
    w_ao = w_a_o[0].astype(BF16)
    w_kvb = w_kv.astype(BF16)
    w_bq = w_b_q[0].astype(BF16)
    w_bo = w_b_o[0].astype(BF16)
    w1 = [w_ff1[l].astype(BF16) for l in range(w_ff1.shape[0])]
    w2 = [w_ff2[l].astype(BF16) for l in range(w_ff2.shape[0])]
    g_a = g_attn[0][None]
    g_b = g_attn[1][None]
    g_k = g_kv[None]
    g_m0 = g_mlp[0][None]
    g_m1 = g_mlp[1][None]
    g_f = g_final[None]
    lp = a_lambda[0]
    sg = a_subln[0][None]
    r = _bias_rows(b_rel[0])
    r_t = _bias_rows_t(b_rel[0])

    cos_p, sin_p = _rope_tables(jnp.arange(seq))
    cos_s, sin_s = _rope_tables(jnp.arange(past, past + nt))
    reps = TOKEN_TILE // nt
    cos_s, sin_s = jnp.tile(cos_s, (reps, 1)), jnp.tile(sin_s, (reps, 1))

    ck_a = cache_a_k.reshape(db, past * a_heads, 2 * HEAD_DIM)
    cv_a = cache_a_v.reshape(db, past * a_heads, 2 * HEAD_DIM)
    ck_b = cache_b_k.reshape(db, lb, d)
    cv_b = cache_b_v.reshape(db, lb, d)
    xp = x_prompt.reshape(nb * seq, d)
    xs = x_sample.reshape(db * nt, d)
    psh, ssh = (nb, seq, d), (db, nt, d)

    q_p, ak_p, kb_p, av_p, vt_p = _proj_a(xp, g_a, w_qkv, cos_p, sin_p, True)
    q_s, ak_s, kb_s, av_s, vb_s = _proj_a(xs, g_a, w_qkv, cos_s, sin_s, False)
    o_p = _attn_a_prompt(lp, sg, q_p.reshape(psh), kb_p.reshape(psh), vt_p, lam0).reshape(nb * seq, d)
    (h_p,), (o_s,) = _launch_pair(
        _post(xp, o_p, w_ao, g_m0, w1[0], w2[0]),
        _attn_a_sample(lp, sg, q_s.reshape(ssh), kb_s.reshape(ssh), vb_s.reshape(ssh), ck_a, cv_a, lam0))
    (h_s,) = _launch(_post(xs, o_s.reshape(db * nt, d), w_ao, g_m0, w1[0], w2[0]))

    qb_p, kbb_p, vtb_p, bk_p, bv_p = _proj_b(h_p, g_k, w_kvb, g_b, w_bq, seq // TOKEN_TILE, True)
    qb_s, kbb_s, vbb_s, bk_s, bv_s = _proj_b(h_s, g_k, w_kvb, g_b, w_bq, 1, False)
    ob_p = _band_prompt(r_t, qb_p.reshape(psh), kbb_p.reshape(psh), vtb_p).reshape(nb * seq, d)
    (y_p,), (ob_s,) = _launch_pair(
        _post(h_p, ob_p, w_bo, g_m1, w1[1], w2[1], g_f),
        _band_sample(r, qb_s.reshape(ssh), kbb_s.reshape(ssh), vbb_s.reshape(ssh), ck_b, cv_b))
    (y_s,) = _launch(_post(h_s, ob_s.reshape(db * nt, d), w_bo, g_m1, w1[1], w2[1], g_f))

    return (y_p.reshape(nb, seq, d), y_s.reshape(db, nt, d),
            ak_p.reshape(1, nb, seq, a_heads, 2 * HEAD_DIM), av_p.reshape(1, nb, seq, a_heads, 2 * HEAD_DIM),
            bk_p.reshape(nb, keep, b_heads, HEAD_DIM), bv_p.reshape(nb, keep, b_heads, HEAD_DIM),
            ak_s.reshape(1, db, nt, a_heads, 2 * HEAD_DIM), av_s.reshape(1, db, nt, a_heads, 2 * HEAD_DIM),
            bk_s.reshape(db, nt, b_heads, HEAD_DIM), bv_s.reshape(db, nt, b_heads, HEAD_DIM))
```

```python
import functools
import math
from typing import Callable, NamedTuple

import jax
import jax.numpy as jnp
from jax import lax
from jax.experimental import pallas as pl
from jax.experimental.pallas import tpu as pltpu

HEAD_DIM = 64
CHUNK = 64
LEFT_CHUNKS = 8
PAD = LEFT_CHUNKS * CHUNK
REL_CLIP = 128
ROPE_THETA = 10000.0
EPS = 1e-6
NEG_INF = -1e30
LANES = 128
LOG2E = math.log2(math.e)
QSCALE = HEAD_DIM ** -0.5 * LOG2E

TOKEN_TILE = 512
FF_CHUNK = 1024
ATTN_TILE = 512
ONES_ROWS = 16
BAND_TILE = 256
SAMPLE_BIAS_OFFSET = 128
VMEM_LIMIT = 56 * 1024 * 1024

F32 = jnp.float32
BF16 = jnp.bfloat16
NT_DIMS = (((1,), (1,)), ((), ()))


def _rms(x, g):
    ms = jnp.mean(x * x, axis=-1, keepdims=True)
    return x * lax.rsqrt(ms + EPS) * g


def _resident(shape):
    return pl.BlockSpec(shape, lambda *_: (0,) * len(shape), pipeline_mode=pl.Buffered(1))


def _resident_layer(stacked_shape, layer):
    rest = tuple(stacked_shape[1:])
    return pl.BlockSpec((None,) + rest, lambda *_: (layer,) + (0,) * len(rest), pipeline_mode=pl.Buffered(1))


def _params(*sem):
    return pltpu.CompilerParams(dimension_semantics=sem, vmem_limit_bytes=VMEM_LIMIT)


class _Call(NamedTuple):
    body: Callable
    grid: tuple
    in_specs: list
    args: list
    out_specs: list
    out_shape: list
    name: str


def _run(stages):
    for _ in stages:
        pass


def _launch(call):
    outs = pl.pallas_call(lambda *refs: _run(call.body(*refs)), grid=call.grid, in_specs=call.in_specs,
                          out_specs=call.out_specs, out_shape=call.out_shape,
                          compiler_params=_params(*["parallel"] * len(call.grid)),
                          name=call.name)(*call.args)
    return outs


def _launch_pair(heavy, light):
    if heavy.grid != light.grid:
        return _launch(heavy), _launch(light)
    n_in_h, n_in_l, n_out_h = len(heavy.args), len(light.args), len(heavy.out_shape)

    def body(*refs):
        ins, outs = refs[:n_in_h + n_in_l], refs[n_in_h + n_in_l:]
        a = light.body(*ins[n_in_h:], *outs[n_out_h:])
        b = heavy.body(*ins[:n_in_h], *outs[:n_out_h])
        live = [a, b]
        while live:
            for g in list(live):
                if next(g, StopIteration) is StopIteration:
                    live.remove(g)

    outs = pl.pallas_call(body, grid=heavy.grid, in_specs=heavy.in_specs + light.in_specs,
                          out_specs=heavy.out_specs + light.out_specs,
                          out_shape=heavy.out_shape + light.out_shape,
                          compiler_params=_params(*["parallel"] * len(heavy.grid)),
                          name=heavy.name + "_" + light.name)(*heavy.args, *light.args)
    return outs[:n_out_h], outs[n_out_h:]


def _diff_lambda(lp, lam0):
    a = jnp.sum(lp[0:1] * lp[1:2], axis=-1, keepdims=True)
    b = jnp.sum(lp[2:3] * lp[3:4], axis=-1, keepdims=True)
    return jnp.exp(a) - jnp.exp(b) + lam0


def _head_norm(o, g, lam0):
    return _rms(o, g) * (1.0 - lam0)


def _bf16_values(v, transposed):
    return (v.T if transposed else v).astype(BF16)


def _proj_a_kernel(x_ref, g_ref, w_ref, cos_ref, sin_ref,
                   q_ref, k_ref, kb_ref, v_ref, vb_ref, *, v_transposed):
    d = x_ref.shape[1]
    xn = _rms(x_ref[...], g_ref[...]).astype(BF16)
    cos = cos_ref[...]
    sin = sin_ref[...]
    lane = lax.broadcasted_iota(jnp.int32, cos.shape, 1)
    first_half = (lane % HEAD_DIM) < (HEAD_DIM // 2)

    def rope(a):
        partner = jnp.where(first_half, pltpu.roll(a, LANES - HEAD_DIM // 2, 1),
                            pltpu.roll(a, HEAD_DIM // 2, 1))
        return a * cos + partner * sin

    q = jnp.dot(xn, w_ref[:, 0:d], preferred_element_type=F32)
    for c in range(d // LANES):
        sl = slice(c * LANES, (c + 1) * LANES)
        q_ref[:, sl] = (rope(q[:, sl]) * QSCALE).astype(BF16)
    tm, heads = x_ref.shape[0], d // LANES
    k = jnp.dot(xn, w_ref[:, d:2 * d], preferred_element_type=F32)
    for c in range(heads):
        sl = slice(c * LANES, (c + 1) * LANES)
        kr = rope(k[:, sl])
        k_ref[pl.ds(c, tm, stride=heads), :] = kr
        kb_ref[:, sl] = kr.astype(BF16)
    v = jnp.dot(xn, w_ref[:, 2 * d:3 * d], preferred_element_type=F32)
    for c in range(heads):
        v_ref[pl.ds(c, tm, stride=heads), :] = v[:, c * LANES:(c + 1) * LANES]
    vb_ref[...] = _bf16_values(v, v_transposed)


def _values_out(t, d, tm, transposed):
    if transposed:
        return pl.BlockSpec((d, tm), lambda i: (0, i)), jax.ShapeDtypeStruct((d, t), BF16)
    return pl.BlockSpec((tm, d), lambda i: (i, 0)), jax.ShapeDtypeStruct((t, d), BF16)


def _proj_a(x, g, w, cos, sin, v_transposed):
    t, d = x.shape
    tm = TOKEN_TILE
    period = cos.shape[0] // tm
    heads = d // LANES
    row = pl.BlockSpec((tm, d), lambda i: (i, 0))
    native = pl.BlockSpec((tm * heads, LANES), lambda i: (i, 0))
    tab = pl.BlockSpec((tm, LANES), lambda i: (i % period, 0))
    vb_spec, vb_shape = _values_out(t, d, tm, v_transposed)
    return pl.pallas_call(
        functools.partial(_proj_a_kernel, v_transposed=v_transposed),
        grid=(t // tm,),
        in_specs=[row, _resident((1, d)), _resident((d, 3 * d)), tab, tab],
        out_specs=[row, native, row, native, vb_spec],
        out_shape=[jax.ShapeDtypeStruct((t, d), BF16), jax.ShapeDtypeStruct((t * heads, LANES), F32),
                   jax.ShapeDtypeStruct((t, d), BF16), jax.ShapeDtypeStruct((t * heads, LANES), F32),
                   vb_shape],
        compiler_params=_params("parallel"),
        name="proj_a",
    )(x, g, w, cos, sin)


def _attn_a_prompt_kernel(lp_ref, sg_ref, q_ref, k_ref, v_ref, o_ref, vt_ref, *, lam0):
    t = ATTN_TILE
    s_len = q_ref.shape[1]
    lam = _diff_lambda(lp_ref[...], lam0)
    sg = sg_ref[...]
    vt_ref[0:LANES, :] = v_ref[...]

    vt_ref[LANES:, :] = jnp.ones((vt_ref.shape[0] - LANES, s_len), BF16)

    th = t // 2
    lane = lax.broadcasted_iota(jnp.int32, (th, LANES), 1)
    key_chunk = lax.broadcasted_iota(jnp.int32, (th, t), 0) // CHUNK
    qry_chunk = (lax.broadcasted_iota(jnp.int32, (th, t), 1) % th) // CHUNK
    diag_half = qry_chunk >= key_chunk

    def query_rows(qi, halves):
        rows = []
        for u in halves:
            q = q_ref[0, qi * t + u * th:qi * t + (u + 1) * th, :]
            zero = jnp.zeros_like(q)
            rows += [jnp.where(lane < HEAD_DIM, q, zero), jnp.where(lane >= HEAD_DIM, q, zero)]
        return jnp.concatenate(rows, axis=0)

    def scores(qi, j):
        if j < qi:
            return lax.dot_general(k_ref[0, j * t:(j + 1) * t, :], query_rows(qi, (0, 1)), NT_DIMS,
                                   preferred_element_type=F32)
        top = lax.dot_general(k_ref[0, qi * t:qi * t + th, :], query_rows(qi, (0, 1)), NT_DIMS,
                              preferred_element_type=F32)
        bot = lax.dot_general(k_ref[0, qi * t + th:(qi + 1) * t, :], query_rows(qi, (1,)), NT_DIMS,
                              preferred_element_type=F32)
        first = jnp.where(diag_half, top[:, 0:t], NEG_INF)
        second = jnp.concatenate([top[:, t:2 * t], jnp.where(diag_half, bot, NEG_INF)], axis=0)
        return first, second

    def update(s, k_lo, k_hi, m, acc):
        m_new = jnp.maximum(m, jnp.max(s, axis=0, keepdims=True))
        p = jnp.exp2(s - m_new)
        acc = jnp.exp2(m - m_new) * acc + jnp.dot(vt_ref[:, k_lo:k_hi], p.astype(BF16),
                                                  preferred_element_type=F32)
        return m_new, acc

    pairs = [(qi, j) for qi in range(s_len // t) for j in range(qi + 1)]
    s_next = scores(*pairs[0])
    for n, (qi, j) in enumerate(pairs):
        s = s_next
        if n + 1 < len(pairs):
            s_next = scores(*pairs[n + 1])
        if j == 0:
            m = jnp.full((1, 2 * t), NEG_INF, F32)
            acc = jnp.zeros((vt_ref.shape[0], 2 * t), F32)
        if j < qi:
            m, acc = update(s, j * t, (j + 1) * t, m, acc)
        else:
            _, acc0 = update(s[0], qi * t, qi * t + th, m[:, 0:t], acc[:, 0:t])
            _, acc1 = update(s[1], qi * t, (qi + 1) * t, m[:, t:2 * t], acc[:, t:2 * t])
            halves = []
            for a in (acc0, acc1):
                a = a[0:LANES] / a[LANES:LANES + 1]
                halves.append(a[:, 0:th] - lam * a[:, th:t])
            o = jnp.concatenate(halves, axis=1)
            o = o * lax.rsqrt(jnp.mean(o * o, axis=0, keepdims=True) + EPS)
            o_ref[0, qi * t:(qi + 1) * t, :] = (o.T * sg * (1.0 - lam0)).astype(BF16)


def _attn_a_prompt(lp, sg, q, k, v, lam0):
    b, s, d = q.shape
    blk = pl.BlockSpec((1, s, LANES), lambda bi, hi: (bi, 0, hi))
    vt_blk = pl.BlockSpec((LANES, s), lambda bi, hi: (hi, bi))
    return pl.pallas_call(
        functools.partial(_attn_a_prompt_kernel, lam0=lam0),
        grid=(b, d // LANES),
        in_specs=[_resident(lp.shape), _resident(sg.shape), blk, blk, vt_blk],
        out_specs=blk,
        out_shape=jax.ShapeDtypeStruct((b, s, d), BF16),
        scratch_shapes=[pltpu.VMEM((LANES + ONES_ROWS, s), BF16)],
        compiler_params=_params("parallel", "parallel"),
        name="attn_a_prompt",
    )(lp, sg, q, k, v)


def _attn_a_sample_kernel(lp_ref, sg_ref, q_ref, kn_ref, vn_ref, ck_ref, cv_ref, o_ref, *, lam0):
    nt, d = q_ref.shape[1], q_ref.shape[2]
    heads = d // LANES
    past = ck_ref.shape[1] // heads
    lam = _diff_lambda(lp_ref[...], lam0)
    sg = sg_ref[...]
    lane = lax.broadcasted_iota(jnp.int32, (nt, LANES), 1)

    def scores(h):
        sl = slice(h * LANES, (h + 1) * LANES)
        q = q_ref[0, :, sl]
        zero = jnp.zeros_like(q)
        qz = jnp.concatenate([jnp.where(lane < HEAD_DIM, q, zero),
                              jnp.where(lane >= HEAD_DIM, q, zero)], axis=0)
        kc = ck_ref[0, pl.ds(h, past, stride=heads), :].astype(BF16)
        return (lax.dot_general(qz, kc, NT_DIMS, preferred_element_type=F32),
                lax.dot_general(qz, kn_ref[0, :, sl], NT_DIMS, preferred_element_type=F32))

    def weights(sc, sn):
        m = jnp.maximum(jnp.max(sc, axis=-1, keepdims=True), jnp.max(sn, axis=-1, keepdims=True))
        ec = jnp.exp2(sc - m)
        en = jnp.exp2(sn - m)
        inv = 1.0 / (jnp.sum(ec, axis=-1, keepdims=True) + jnp.sum(en, axis=-1, keepdims=True))
        pc = ec * inv
        pn = en * inv
        return ((pc[0:nt] - lam * pc[nt:2 * nt]).astype(BF16),
                (pn[0:nt] - lam * pn[nt:2 * nt]).astype(BF16))

    def output(h, pc, pn):
        sl = slice(h * LANES, (h + 1) * LANES)
        vc = cv_ref[0, pl.ds(h, past, stride=heads), :].astype(BF16)
        o = (jnp.dot(pc, vc, preferred_element_type=F32)
             + jnp.dot(pn, vn_ref[0, :, sl], preferred_element_type=F32))
        o_ref[0, :, sl] = _head_norm(o, sg, lam0).astype(BF16)

    s_next = scores(0)
    p_prev = None
    for h in range(heads):
        s = s_next
        if h + 1 < heads:
            s_next = scores(h + 1)
        p = weights(*s)
        if p_prev is not None:
            output(h - 1, *p_prev)
        p_prev = p
        yield
    output(heads - 1, *p_prev)


def _attn_a_sample(lp, sg, q, kn, vn, ck, cv, lam0):
    b, nt, d = q.shape
    new = pl.BlockSpec((1, nt, d), lambda bi: (bi, 0, 0))
    old = pl.BlockSpec((1,) + ck.shape[1:], lambda bi: (bi, 0, 0))
    return _Call(functools.partial(_attn_a_sample_kernel, lam0=lam0), (b,),
                 [_resident(lp.shape), _resident(sg.shape), new, new, new, old, old],
                 [lp, sg, q, kn, vn, ck, cv], [new], [jax.ShapeDtypeStruct((b, nt, d), BF16)],
                 "attn_a_sample")


def _post_kernel(*refs, final):
    if final:
        x_ref, o_ref, wo_ref, g_ref, w1_ref, w2_ref, gf_ref, out_ref = refs
    else:
        x_ref, o_ref, wo_ref, g_ref, w1_ref, w2_ref, out_ref = refs
    h = x_ref[...] + jnp.dot(o_ref[...], wo_ref[...], preferred_element_type=F32)
    hn = _rms(h, g_ref[...]).astype(BF16)
    yield
    d_ff = w1_ref.shape[1]
    mlp = None
    for c in range(d_ff // FF_CHUNK):
        sl = slice(c * FF_CHUNK, (c + 1) * FF_CHUNK)
        a = jnp.dot(hn, w1_ref[:, sl], preferred_element_type=F32)
        u = jnp.square(jnp.maximum(a, 0.0)).astype(BF16)
        yield
        part = jnp.dot(u, w2_ref[sl, :], preferred_element_type=F32)
        mlp = part if mlp is None else mlp + part
        yield
    h = h + mlp
    if final:
        h = _rms(h, gf_ref[...])
    out_ref[...] = h


def _post(x, o, wo, g, w1, w2, layer, gf=None):
    t, d = x.shape
    tm = TOKEN_TILE
    final = gf is not None
    row = pl.BlockSpec((tm, d), lambda i: (i, 0))
    in_specs = [row, row, _resident(wo.shape), _resident(g.shape),
                _resident_layer(w1.shape, layer), _resident_layer(w2.shape, layer)]
    args = [x, o, wo, g, w1, w2]
    if final:
        in_specs.append(_resident(gf.shape))
        args.append(gf)
    return _Call(functools.partial(_post_kernel, final=final), (t // tm,), in_specs, args,
                 [row], [jax.ShapeDtypeStruct((t, d), F32)], "post_final" if final else "post")


def _proj_b_kernel(x_ref, gkv_ref, wkv_ref, gq_ref, wq_ref,
                   q_ref, kb_ref, vb_ref, k_ref, v_ref, *, period, v_transposed):
    d = x_ref.shape[1]
    x = x_ref[...]
    xs = x * lax.rsqrt(jnp.mean(x * x, axis=-1, keepdims=True) + EPS)
    xkv = (xs * gkv_ref[...]).astype(BF16)
    xq = (xs * gq_ref[...]).astype(BF16)
    q_ref[...] = (jnp.dot(xq, wq_ref[...], preferred_element_type=F32) * QSCALE).astype(BF16)
    k = jnp.dot(xkv, wkv_ref[:, 0:d], preferred_element_type=F32)
    v = jnp.dot(xkv, wkv_ref[:, d:2 * d], preferred_element_type=F32)
    kb_ref[...] = k.astype(BF16)
    vb_ref[...] = _bf16_values(v, v_transposed)

    @pl.when(pl.program_id(0) % period == period - 1)
    def _():
        k_ref[...] = k
        v_ref[...] = v


def _proj_b(x, gkv, wkv, gq, wq, period, v_transposed):
    t, d = x.shape
    tm = TOKEN_TILE
    row = pl.BlockSpec((tm, d), lambda i: (i, 0))
    keep = pl.BlockSpec((tm, d), lambda i: (i // period, 0))
    vb_spec, vb_shape = _values_out(t, d, tm, v_transposed)
    return pl.pallas_call(
        functools.partial(_proj_b_kernel, period=period, v_transposed=v_transposed),
        grid=(t // tm,),
        in_specs=[row, _resident(gkv.shape), _resident(wkv.shape), _resident(gq.shape), _resident(wq.shape)],
        out_specs=[row, row, vb_spec, keep, keep],
        out_shape=[jax.ShapeDtypeStruct((t, d), BF16), jax.ShapeDtypeStruct((t, d), BF16),
                   vb_shape,
                   jax.ShapeDtypeStruct((t // period, d), F32), jax.ShapeDtypeStruct((t // period, d), F32)],
        compiler_params=_params("arbitrary"),
        name="proj_b",
    )(x, gkv, wkv, gq, wq)


def _toeplitz_bias(r_row, rows):
    rb = jnp.broadcast_to(r_row * LOG2E, (rows, r_row.shape[1]))
    return pltpu.roll(rb, 0, 1, stride=1, stride_axis=0)


def _band_prompt_kernel(r_ref, q_ref, k_ref, v_ref, o_ref, vt_ref, bias):
    tq = BAND_TILE
    s_len = q_ref.shape[1]
    win = PAD + tq
    vt_ref[0:LANES, :] = v_ref[...]
    vt_ref[LANES:, :] = jnp.ones((vt_ref.shape[0] - LANES, s_len), BF16)

    @pl.when(pl.program_id(1) == 0)
    def _():
        key = lax.broadcasted_iota(jnp.int32, (win, tq), 0)
        qry = lax.broadcasted_iota(jnp.int32, (win, tq), 1)
        dist = qry // CHUNK + LEFT_CHUNKS - key // CHUNK
        in_band = (dist >= 0) & (dist <= LEFT_CHUNKS)
        for hh in range(2):
            rb = jnp.broadcast_to(r_ref[0, hh:hh + 1, :] * LOG2E, (win, win + tq))
            shifted = pltpu.roll(rb, 0, 1, stride=1, stride_axis=0)
            bias[:, hh * tq:(hh + 1) * tq] = jnp.where(in_band, shifted[:, win:win + tq], NEG_INF)

    lane = lax.broadcasted_iota(jnp.int32, (tq, LANES), 1)

    def window(qi):
        q0 = qi * tq
        return max(0, PAD - q0), max(0, q0 - PAD), q0 + tq

    def scores(qi):
        lo, k_lo, k_hi = window(qi)
        q = q_ref[0, qi * tq:(qi + 1) * tq, :]
        zero = jnp.zeros_like(q)
        qz = jnp.concatenate([jnp.where(lane < HEAD_DIM, q, zero),
                              jnp.where(lane >= HEAD_DIM, q, zero)], axis=0)
        return lax.dot_general(k_ref[0, k_lo:k_hi, :], qz, NT_DIMS,
                               preferred_element_type=F32) + bias[lo:win, :]

    n_tiles = s_len // tq
    s_next = scores(0)
    for qi in range(n_tiles):
        s = s_next
        if qi + 1 < n_tiles:
            s_next = scores(qi + 1)
        _, k_lo, k_hi = window(qi)
        m = jnp.max(s, axis=0, keepdims=True)
        e = jnp.exp2(s - m)
        o = jnp.dot(vt_ref[:, k_lo:k_hi], e.astype(BF16), preferred_element_type=F32)
        o = o[0:LANES] / o[LANES:LANES + 1]
        o = jnp.concatenate([o[0:HEAD_DIM, 0:tq], o[HEAD_DIM:LANES, tq:2 * tq]], axis=0)
        o_ref[0, qi * tq:(qi + 1) * tq, :] = o.T.astype(BF16)


def _band_prompt(r, q, k, v):
    b, s, d = q.shape
    blk = pl.BlockSpec((1, s, LANES), lambda pi, bi: (bi, 0, pi))
    vt_blk = pl.BlockSpec((LANES, s), lambda pi, bi: (pi, bi))
    rblk = pl.BlockSpec((1, 2, r.shape[2]), lambda pi, bi: (pi, 0, 0))
    win = PAD + BAND_TILE
    return pl.pallas_call(
        _band_prompt_kernel,
        grid=(d // LANES, b),
        in_specs=[rblk, blk, blk, vt_blk],
        out_specs=blk,
        out_shape=jax.ShapeDtypeStruct((b, s, d), BF16),
        scratch_shapes=[pltpu.VMEM((LANES + ONES_ROWS, s), BF16), pltpu.VMEM((win, 2 * BAND_TILE), F32)],
        compiler_params=_params("arbitrary", "arbitrary"),
        name="band_prompt",
    )(r, q, k, v)


def _band_sample_kernel(r_ref, q_ref, kn_ref, vn_ref, ck_ref, cv_ref, o_ref):
    nt, d = q_ref.shape[1], q_ref.shape[2]
    lb = ck_ref.shape[1]
    off = SAMPLE_BIAS_OFFSET
    lane = lax.broadcasted_iota(jnp.int32, (nt, LANES), 1)
    pairs = d // LANES

    def scores(p):
        sl = slice(p * LANES, (p + 1) * LANES)
        q = q_ref[0, :, sl]
        zero = jnp.zeros_like(q)
        qz = jnp.concatenate([jnp.where(lane < HEAD_DIM, q, zero),
                              jnp.where(lane >= HEAD_DIM, q, zero)], axis=0)
        shifted = jnp.concatenate([_toeplitz_bias(r_ref[p, hh:hh + 1, :], nt)
                                   for hh in range(2)], axis=0)
        sc = (lax.dot_general(qz, ck_ref[0, :, sl].astype(BF16), NT_DIMS, preferred_element_type=F32)
              + shifted[:, off:off + lb])
        sn = (lax.dot_general(qz, kn_ref[0, :, sl], NT_DIMS, preferred_element_type=F32)
              + shifted[:, off + lb:off + lb + nt])
        return sc, sn

    def weights(sc, sn):
        m = jnp.maximum(jnp.max(sc, axis=-1, keepdims=True), jnp.max(sn, axis=-1, keepdims=True))
        ec = jnp.exp2(sc - m)
        en = jnp.exp2(sn - m)
        inv = 1.0 / (jnp.sum(ec, axis=-1, keepdims=True) + jnp.sum(en, axis=-1, keepdims=True))
        return (ec * inv).astype(BF16), (en * inv).astype(BF16)

    def output(p, pc, pn):
        sl = slice(p * LANES, (p + 1) * LANES)
        o = (jnp.dot(pc, cv_ref[0, :, sl].astype(BF16), preferred_element_type=F32)
             + jnp.dot(pn, vn_ref[0, :, sl], preferred_element_type=F32))
        o_ref[0, :, sl] = jnp.where(lane < HEAD_DIM, o[0:nt], o[nt:2 * nt]).astype(BF16)

    s_next = scores(0)
    w_prev = None
    for p in range(pairs):
        s = s_next
        if p + 1 < pairs:
            s_next = scores(p + 1)
        w = weights(*s)
        if w_prev is not None:
            output(p - 1, *w_prev)
        w_prev = w
        yield
    output(pairs - 1, *w_prev)


def _band_sample(r, q, kn, vn, ck, cv):
    b, nt, d = q.shape
    lb = ck.shape[1]
    new = pl.BlockSpec((1, nt, d), lambda bi: (bi, 0, 0))
    old = pl.BlockSpec((1, lb, d), lambda bi: (bi, 0, 0))
    return _Call(_band_sample_kernel, (b,), [_resident(r.shape), new, new, new, old, old],
                 [r, q, kn, vn, ck, cv], [new], [jax.ShapeDtypeStruct((b, nt, d), BF16)], "band_sample")


def _rope_tables(pos):
    half = HEAD_DIM // 2
    inv = 1.0 / (ROPE_THETA ** (jnp.arange(half, dtype=F32) / half))
    ang = pos.astype(F32)[:, None] * inv[None, :]
    cos, sin = jnp.cos(ang), jnp.sin(ang)
    reps = LANES // HEAD_DIM
    return (jnp.tile(jnp.concatenate([cos, cos], axis=-1), (1, reps)),
            jnp.tile(jnp.concatenate([-sin, sin], axis=-1), (1, reps)))


def _bias_rows(table):
    far = PAD + SAMPLE_BIAS_OFFSET - REL_CLIP
    near = table[:, 2 * REL_CLIP:0:-1]
    rows = jnp.concatenate([jnp.broadcast_to(table[:, 2 * REL_CLIP:], (table.shape[0], far)), near], axis=1)
    return rows.reshape(table.shape[0] // 2, 2, rows.shape[1])


def _bias_rows_t(table):
    heads = table.shape[0]
    lo = BAND_TILE - REL_CLIP
    hi = PAD + 2 * BAND_TILE - lo - (2 * REL_CLIP + 1)
    rows = jnp.concatenate([jnp.broadcast_to(table[:, :1], (heads, lo)), table,
                            jnp.broadcast_to(table[:, 2 * REL_CLIP:], (heads, hi))], axis=1)
    return rows.reshape(heads // 2, 2, rows.shape[1])


def _lambda_init(layer):
    return 0.8 - 0.6 * math.exp(-0.3 * layer)


def kernel(x_prompt, x_sample, cache_a_k, cache_a_v, cache_b_k, cache_b_v, g_attn, w_a_qkv, a_lambda, a_subln, w_a_o, g_kv, w_kv, w_b_q, b_rel, w_b_o, g_mlp, w_ff1, w_ff2, g_final):
    nb, seq, d = x_prompt.shape
    db, nt, _ = x_sample.shape
    past = cache_a_k.shape[2]
    lb = cache_b_k.shape[1]
    a_heads = d // (2 * HEAD_DIM)
    b_heads = d // HEAD_DIM
    assert w_a_qkv.shape[0] == 1 and w_b_q.shape[0] == 1, "one differential and one band layer"
    assert (db * nt) % TOKEN_TILE == 0 and TOKEN_TILE % nt == 0 and seq % TOKEN_TILE == 0
    keep = min(PAD, seq)
    assert keep == TOKEN_TILE and lb == PAD

    lam0 = _lambda_init(0)
    w_qkv = w_a_qkv[0].astype(BF16)
    w_ao = w_a_o[0].astype(BF16)
    w_kvb = w_kv.astype(BF16)
    w_bq = w_b_q[0].astype(BF16)
    w_bo = w_b_o[0].astype(BF16)
    w1 = w_ff1.astype(BF16)
    w2 = w_ff2.astype(BF16)
    g_a = g_attn[0][None]
    g_b = g_attn[1][None]
    g_k = g_kv[None]
    g_m0 = g_mlp[0][None]
    g_m1 = g_mlp[1][None]
    g_f = g_final[None]
    lp = a_lambda[0]
    sg = a_subln[0][None]
    r = _bias_rows(b_rel[0])
    r_t = _bias_rows_t(b_rel[0])

    cos_p, sin_p = _rope_tables(jnp.arange(seq))
    cos_s, sin_s = _rope_tables(jnp.arange(past, past + nt))
    reps = TOKEN_TILE // nt
    cos_s, sin_s = jnp.tile(cos_s, (reps, 1)), jnp.tile(sin_s, (reps, 1))

    ck_a = cache_a_k.reshape(db, past * a_heads, 2 * HEAD_DIM)
    cv_a = cache_a_v.reshape(db, past * a_heads, 2 * HEAD_DIM)
    ck_b = cache_b_k.reshape(db, lb, d)
    cv_b = cache_b_v.reshape(db, lb, d)
    xp = x_prompt.reshape(nb * seq, d)
    xs = x_sample.reshape(db * nt, d)
    psh, ssh = (nb, seq, d), (db, nt, d)

    q_p, ak_p, kb_p, av_p, vt_p = _proj_a(xp, g_a, w_qkv, cos_p, sin_p, True)
    q_s, ak_s, kb_s, av_s, vb_s = _proj_a(xs, g_a, w_qkv, cos_s, sin_s, False)
    o_p = _attn_a_prompt(lp, sg, q_p.reshape(psh), kb_p.reshape(psh), vt_p, lam0).reshape(nb * seq, d)
    (h_p,), (o_s,) = _launch_pair(
        _post(xp, o_p, w_ao, g_m0, w1, w2, 0),
        _attn_a_sample(lp, sg, q_s.reshape(ssh), kb_s.reshape(ssh), vb_s.reshape(ssh), ck_a, cv_a, lam0))
    (h_s,) = _launch(_post(xs, o_s.reshape(db * nt, d), w_ao, g_m0, w1, w2, 0))

    qb_p, kbb_p, vtb_p, bk_p, bv_p = _proj_b(h_p, g_k, w_kvb, g_b, w_bq, seq // TOKEN_TILE, True)
    qb_s, kbb_s, vbb_s, bk_s, bv_s = _proj_b(h_s, g_k, w_kvb, g_b, w_bq, 1, False)
    ob_p = _band_prompt(r_t, qb_p.reshape(psh), kbb_p.reshape(psh), vtb_p).reshape(nb * seq, d)
    (y_p,), (ob_s,) = _launch_pair(
        _post(h_p, ob_p, w_bo, g_m1, w1, w2, 1, g_f),
        _band_sample(r, qb_s.reshape(ssh), kbb_s.reshape(ssh), vbb_s.reshape(ssh), ck_b, cv_b))
    (y_s,) = _launch(_post(h_s, ob_s.reshape(db * nt, d), w_bo, g_m1, w1, w2, 1, g_f))

    return (y_p.reshape(nb, seq, d), y_s.reshape(db, nt, d),
            ak_p.reshape(1, nb, seq, a_heads, 2 * HEAD_DIM), av_p.reshape(1, nb, seq, a_heads, 2 * HEAD_DIM),
            bk_p.reshape(nb, keep, b_heads, HEAD_DIM), bv_p.reshape(nb, keep, b_heads, HEAD_DIM),
            ak_s.reshape(1, db, nt, a_heads, 2 * HEAD_DIM), av_s.reshape(1, db, nt, a_heads, 2 * HEAD_DIM),
            bk_s.reshape(db, nt, b_heads, HEAD_DIM), bv_s.reshape(db, nt, b_heads, HEAD_DIM))
```

```python
import functools
import math
from typing import Callable, NamedTuple

import jax
import jax.numpy as jnp
from jax import lax
from jax.experimental import pallas as pl
from jax.experimental.pallas import tpu as pltpu

HEAD_DIM = 64
CHUNK = 64
LEFT_CHUNKS = 8
PAD = LEFT_CHUNKS * CHUNK
REL_CLIP = 128
ROPE_THETA = 10000.0
EPS = 1e-6
NEG_INF = -1e30
LANES = 128
LOG2E = math.log2(math.e)
QSCALE = HEAD_DIM ** -0.5 * LOG2E

TOKEN_TILE = 512
FF_CHUNK = 1024
ATTN_TILE = 512
ATTN_HEADS_PER_STEP = 2
BAND_PAIRS_PER_STEP = 4
ONES_ROWS = 16
BAND_TILE = 256
SAMPLE_BIAS_OFFSET = 128
VMEM_LIMIT = 56 * 1024 * 1024

F32 = jnp.float32
BF16 = jnp.bfloat16
NT_DIMS = (((1,), (1,)), ((), ()))


def _rms(x, g):
    ms = jnp.mean(x * x, axis=-1, keepdims=True)
    return x * lax.rsqrt(ms + EPS) * g


def _resident(shape):
    return pl.BlockSpec(shape, lambda *_: (0,) * len(shape), pipeline_mode=pl.Buffered(1))


def _resident_layer(stacked_shape, layer):
    rest = tuple(stacked_shape[1:])
    return pl.BlockSpec((None,) + rest, lambda *_: (layer,) + (0,) * len(rest), pipeline_mode=pl.Buffered(1))


def _params(*sem):
    return pltpu.CompilerParams(dimension_semantics=sem, vmem_limit_bytes=VMEM_LIMIT)


class _Call(NamedTuple):
    body: Callable
    grid: tuple
    in_specs: list
    args: list
    out_specs: list
    out_shape: list
    name: str


def _run(stages):
    for _ in stages:
        pass


def _alternate(bodies):
    live = list(bodies)
    while live:
        for g in list(live):
            if next(g, StopIteration) is StopIteration:
                live.remove(g)


def _launch(call):
    outs = pl.pallas_call(lambda *refs: _run(call.body(*refs)), grid=call.grid, in_specs=call.in_specs,
                          out_specs=call.out_specs, out_shape=call.out_shape,
                          compiler_params=_params(*["parallel"] * len(call.grid)),
                          name=call.name)(*call.args)
    return outs


def _launch_pair(heavy, light):
    if heavy.grid != light.grid:
        return _launch(heavy), _launch(light)
    n_in_h, n_in_l, n_out_h = len(heavy.args), len(light.args), len(heavy.out_shape)

    def body(*refs):
        ins, outs = refs[:n_in_h + n_in_l], refs[n_in_h + n_in_l:]
        _alternate([light.body(*ins[n_in_h:], *outs[n_out_h:]), heavy.body(*ins[:n_in_h], *outs[:n_out_h])])

    outs = pl.pallas_call(body, grid=heavy.grid, in_specs=heavy.in_specs + light.in_specs,
                          out_specs=heavy.out_specs + light.out_specs,
                          out_shape=heavy.out_shape + light.out_shape,
                          compiler_params=_params(*["parallel"] * len(heavy.grid)),
                          name=heavy.name + "_" + light.name)(*heavy.args, *light.args)
    return outs[:n_out_h], outs[n_out_h:]


def _diff_lambda(lp, lam0):
    a = jnp.sum(lp[0:1] * lp[1:2], axis=-1, keepdims=True)
    b = jnp.sum(lp[2:3] * lp[3:4], axis=-1, keepdims=True)
    return jnp.exp(a) - jnp.exp(b) + lam0


def _head_norm(o, g, lam0):
    return _rms(o, g) * (1.0 - lam0)


def _bf16_values(v, transposed):
    return (v.T if transposed else v).astype(BF16)


def _proj_a_kernel(x_ref, g_ref, w_ref, cos_ref, sin_ref,
                   q_ref, k_ref, kb_ref, v_ref, vb_ref, *, v_transposed):
    d = x_ref.shape[1]
    xn = _rms(x_ref[...], g_ref[...]).astype(BF16)
    cos = cos_ref[...]
    sin = sin_ref[...]
    lane = lax.broadcasted_iota(jnp.int32, cos.shape, 1)
    first_half = (lane % HEAD_DIM) < (HEAD_DIM // 2)

    def rope(a):
        partner = jnp.where(first_half, pltpu.roll(a, LANES - HEAD_DIM // 2, 1),
                            pltpu.roll(a, HEAD_DIM // 2, 1))
        return a * cos + partner * sin

    q = jnp.dot(xn, w_ref[:, 0:d], preferred_element_type=F32)
    for c in range(d // LANES):
        sl = slice(c * LANES, (c + 1) * LANES)
        q_ref[:, sl] = (rope(q[:, sl]) * QSCALE).astype(BF16)
    tm, heads = x_ref.shape[0], d // LANES
    k = jnp.dot(xn, w_ref[:, d:2 * d], preferred_element_type=F32)
    for c in range(heads):
        sl = slice(c * LANES, (c + 1) * LANES)
        kr = rope(k[:, sl])
        k_ref[pl.ds(c, tm, stride=heads), :] = kr
        kb_ref[:, sl] = kr.astype(BF16)
    v = jnp.dot(xn, w_ref[:, 2 * d:3 * d], preferred_element_type=F32)
    for c in range(heads):
        v_ref[pl.ds(c, tm, stride=heads), :] = v[:, c * LANES:(c + 1) * LANES]
    vb_ref[...] = _bf16_values(v, v_transposed)


def _values_out(t, d, tm, transposed):
    if transposed:
        return pl.BlockSpec((d, tm), lambda i: (0, i)), jax.ShapeDtypeStruct((d, t), BF16)
    return pl.BlockSpec((tm, d), lambda i: (i, 0)), jax.ShapeDtypeStruct((t, d), BF16)


def _proj_a(x, g, w, cos, sin, v_transposed):
    t, d = x.shape
    tm = TOKEN_TILE
    period = cos.shape[0] // tm
    heads = d // LANES
    row = pl.BlockSpec((tm, d), lambda i: (i, 0))
    native = pl.BlockSpec((tm * heads, LANES), lambda i: (i, 0))
    tab = pl.BlockSpec((tm, LANES), lambda i: (i % period, 0))
    vb_spec, vb_shape = _values_out(t, d, tm, v_transposed)
    return pl.pallas_call(
        functools.partial(_proj_a_kernel, v_transposed=v_transposed),
        grid=(t // tm,),
        in_specs=[row, _resident((1, d)), _resident((d, 3 * d)), tab, tab],
        out_specs=[row, native, row, native, vb_spec],
        out_shape=[jax.ShapeDtypeStruct((t, d), BF16), jax.ShapeDtypeStruct((t * heads, LANES), F32),
                   jax.ShapeDtypeStruct((t, d), BF16), jax.ShapeDtypeStruct((t * heads, LANES), F32),
                   vb_shape],
        compiler_params=_params("parallel"),
        name="proj_a",
    )(x, g, w, cos, sin)


def _attn_a_prompt_kernel(lp_ref, sg_ref, q_ref, k_ref, v_ref, o_ref, vt_ref, *, lam0):
    t = ATTN_TILE
    s_len = q_ref.shape[1]
    lam = _diff_lambda(lp_ref[...], lam0)
    sg = sg_ref[...]

    th = t // 2
    lane = lax.broadcasted_iota(jnp.int32, (th, LANES), 1)
    key_chunk = lax.broadcasted_iota(jnp.int32, (th, t), 0) // CHUNK
    qry_chunk = (lax.broadcasted_iota(jnp.int32, (th, t), 1) % th) // CHUNK
    diag_half = qry_chunk >= key_chunk

    def sweep(hh):
        hs = slice(hh * LANES, (hh + 1) * LANES)
        vt_ref[hh, 0:LANES, :] = v_ref[hs, :]
        vt_ref[hh, LANES:, :] = jnp.ones((vt_ref.shape[1] - LANES, s_len), BF16)

        def query_rows(qi, halves):
            rows = []
            for u in halves:
                q = q_ref[0, qi * t + u * th:qi * t + (u + 1) * th, hs]
                zero = jnp.zeros_like(q)
                rows += [jnp.where(lane < HEAD_DIM, q, zero), jnp.where(lane >= HEAD_DIM, q, zero)]
            return jnp.concatenate(rows, axis=0)

        def scores(qi, j):
            if j < qi:
                return lax.dot_general(k_ref[0, j * t:(j + 1) * t, hs], query_rows(qi, (0, 1)), NT_DIMS,
                                       preferred_element_type=F32)
            top = lax.dot_general(k_ref[0, qi * t:qi * t + th, hs], query_rows(qi, (0, 1)), NT_DIMS,
                                  preferred_element_type=F32)
            bot = lax.dot_general(k_ref[0, qi * t + th:(qi + 1) * t, hs], query_rows(qi, (1,)), NT_DIMS,
                                  preferred_element_type=F32)
            first = jnp.where(diag_half, top[:, 0:t], NEG_INF)
            second = jnp.concatenate([top[:, t:2 * t], jnp.where(diag_half, bot, NEG_INF)], axis=0)
            return first, second

        def update(s, k_lo, k_hi, m, acc):
            m_new = jnp.maximum(m, jnp.max(s, axis=0, keepdims=True))
            p = jnp.exp2(s - m_new)
            acc = jnp.exp2(m - m_new) * acc + jnp.dot(vt_ref[hh, :, k_lo:k_hi], p.astype(BF16),
                                                      preferred_element_type=F32)
            return m_new, acc

        pairs = [(qi, j) for qi in range(s_len // t) for j in range(qi + 1)]
        s_next = scores(*pairs[0])
        yield
        for n, (qi, j) in enumerate(pairs):
            s = s_next
            if n + 1 < len(pairs):
                s_next = scores(*pairs[n + 1])
                yield
            if j == 0:
                m = jnp.full((1, 2 * t), NEG_INF, F32)
                acc = jnp.zeros((vt_ref.shape[1], 2 * t), F32)
            if j < qi:
                m, acc = update(s, j * t, (j + 1) * t, m, acc)
            else:
                _, acc0 = update(s[0], qi * t, qi * t + th, m[:, 0:t], acc[:, 0:t])
                _, acc1 = update(s[1], qi * t, (qi + 1) * t, m[:, t:2 * t], acc[:, t:2 * t])
                halves = []
                for a in (acc0, acc1):
                    a = a[0:LANES] / a[LANES:LANES + 1]
                    halves.append(a[:, 0:th] - lam * a[:, th:t])
                o = jnp.concatenate(halves, axis=1)
                o = o * lax.rsqrt(jnp.mean(o * o, axis=0, keepdims=True) + EPS)
                o_ref[0, qi * t:(qi + 1) * t, hs] = (o.T * sg * (1.0 - lam0)).astype(BF16)
            yield

    _alternate([sweep(hh) for hh in range(q_ref.shape[2] // LANES)])


def _attn_a_prompt(lp, sg, q, k, v, lam0):
    b, s, d = q.shape
    width = ATTN_HEADS_PER_STEP * LANES
    blk = pl.BlockSpec((1, s, width), lambda bi, hi: (bi, 0, hi))
    vt_blk = pl.BlockSpec((width, s), lambda bi, hi: (hi, bi))
    return pl.pallas_call(
        functools.partial(_attn_a_prompt_kernel, lam0=lam0),
        grid=(b, d // width),
        in_specs=[_resident(lp.shape), _resident(sg.shape), blk, blk, vt_blk],
        out_specs=blk,
        out_shape=jax.ShapeDtypeStruct((b, s, d), BF16),
        scratch_shapes=[pltpu.VMEM((ATTN_HEADS_PER_STEP, LANES + ONES_ROWS, s), BF16)],
        compiler_params=_params("parallel", "parallel"),
        name="attn_a_prompt",
    )(lp, sg, q, k, v)


def _attn_a_sample_kernel(lp_ref, sg_ref, q_ref, kn_ref, vn_ref, ck_ref, cv_ref, o_ref, *, lam0):
    nt, d = q_ref.shape[1], q_ref.shape[2]
    heads = d // LANES
    past = ck_ref.shape[1] // heads
    lam = _diff_lambda(lp_ref[...], lam0)
    sg = sg_ref[...]
    lane = lax.broadcasted_iota(jnp.int32, (nt, LANES), 1)

    def scores(h):
        sl = slice(h * LANES, (h + 1) * LANES)
        q = q_ref[0, :, sl]
        zero = jnp.zeros_like(q)
        qz = jnp.concatenate([jnp.where(lane < HEAD_DIM, q, zero),
                              jnp.where(lane >= HEAD_DIM, q, zero)], axis=0)
        kc = ck_ref[0, pl.ds(h, past, stride=heads), :].astype(BF16)
        return (lax.dot_general(qz, kc, NT_DIMS, preferred_element_type=F32),
                lax.dot_general(qz, kn_ref[0, :, sl], NT_DIMS, preferred_element_type=F32))

    def weights(sc, sn):
        m = jnp.maximum(jnp.max(sc, axis=-1, keepdims=True), jnp.max(sn, axis=-1, keepdims=True))
        ec = jnp.exp2(sc - m)
        en = jnp.exp2(sn - m)
        inv = 1.0 / (jnp.sum(ec, axis=-1, keepdims=True) + jnp.sum(en, axis=-1, keepdims=True))
        pc = ec * inv
        pn = en * inv
        return ((pc[0:nt] - lam * pc[nt:2 * nt]).astype(BF16),
                (pn[0:nt] - lam * pn[nt:2 * nt]).astype(BF16))

    def output(h, pc, pn):
        sl = slice(h * LANES, (h + 1) * LANES)
        vc = cv_ref[0, pl.ds(h, past, stride=heads), :].astype(BF16)
        o = (jnp.dot(pc, vc, preferred_element_type=F32)
             + jnp.dot(pn, vn_ref[0, :, sl], preferred_element_type=F32))
        o_ref[0, :, sl] = _head_norm(o, sg, lam0).astype(BF16)

    s_next = scores(0)
    p_prev = None
    for h in range(heads):
        s = s_next
        if h + 1 < heads:
            s_next = scores(h + 1)
        p = weights(*s)
        if p_prev is not None:
            output(h - 1, *p_prev)
        p_prev = p
        yield
    output(heads - 1, *p_prev)


def _attn_a_sample(lp, sg, q, kn, vn, ck, cv, lam0):
    b, nt, d = q.shape
    new = pl.BlockSpec((1, nt, d), lambda bi: (bi, 0, 0))
    old = pl.BlockSpec((1,) + ck.shape[1:], lambda bi: (bi, 0, 0))
    return _Call(functools.partial(_attn_a_sample_kernel, lam0=lam0), (b,),
                 [_resident(lp.shape), _resident(sg.shape), new, new, new, old, old],
                 [lp, sg, q, kn, vn, ck, cv], [new], [jax.ShapeDtypeStruct((b, nt, d), BF16)],
                 "attn_a_sample")


def _post_kernel(*refs, final):
    if final:
        x_ref, o_ref, wo_ref, g_ref, w1_ref, w2_ref, gf_ref, out_ref = refs
    else:
        x_ref, o_ref, wo_ref, g_ref, w1_ref, w2_ref, out_ref = refs
    h = x_ref[...] + jnp.dot(o_ref[...], wo_ref[...], preferred_element_type=F32)
    hn = _rms(h, g_ref[...]).astype(BF16)
    yield
    d_ff = w1_ref.shape[1]
    mlp = None
    for c in range(d_ff // FF_CHUNK):
        sl = slice(c * FF_CHUNK, (c + 1) * FF_CHUNK)
        a = jnp.dot(hn, w1_ref[:, sl], preferred_element_type=F32)
        u = jnp.square(jnp.maximum(a, 0.0)).astype(BF16)
        yield
        part = jnp.dot(u, w2_ref[sl, :], preferred_element_type=F32)
        mlp = part if mlp is None else mlp + part
        yield
    h = h + mlp
    if final:
        h = _rms(h, gf_ref[...])
    out_ref[...] = h


def _post(x, o, wo, g, w1, w2, layer, gf=None):
    t, d = x.shape
    tm = TOKEN_TILE
    final = gf is not None
    row = pl.BlockSpec((tm, d), lambda i: (i, 0))
    in_specs = [row, row, _resident(wo.shape), _resident(g.shape),
                _resident_layer(w1.shape, layer), _resident_layer(w2.shape, layer)]
    args = [x, o, wo, g, w1, w2]
    if final:
        in_specs.append(_resident(gf.shape))
        args.append(gf)
    return _Call(functools.partial(_post_kernel, final=final), (t // tm,), in_specs, args,
                 [row], [jax.ShapeDtypeStruct((t, d), F32)], "post_final" if final else "post")


def _proj_b_kernel(x_ref, gkv_ref, wkv_ref, gq_ref, wq_ref,
                   q_ref, kb_ref, vb_ref, k_ref, v_ref, *, period, v_transposed):
    d = x_ref.shape[1]
    x = x_ref[...]
    xs = x * lax.rsqrt(jnp.mean(x * x, axis=-1, keepdims=True) + EPS)
    xkv = (xs * gkv_ref[...]).astype(BF16)
    xq = (xs * gq_ref[...]).astype(BF16)
    q_ref[...] = (jnp.dot(xq, wq_ref[...], preferred_element_type=F32) * QSCALE).astype(BF16)
    k = jnp.dot(xkv, wkv_ref[:, 0:d], preferred_element_type=F32)
    v = jnp.dot(xkv, wkv_ref[:, d:2 * d], preferred_element_type=F32)
    kb_ref[...] = k.astype(BF16)
    vb_ref[...] = _bf16_values(v, v_transposed)

    @pl.when(pl.program_id(0) % period == period - 1)
    def _():
        k_ref[...] = k
        v_ref[...] = v


def _proj_b(x, gkv, wkv, gq, wq, period, v_transposed):
    t, d = x.shape
    tm = TOKEN_TILE
    row = pl.BlockSpec((tm, d), lambda i: (i, 0))
    keep = pl.BlockSpec((tm, d), lambda i: (i // period, 0))
    vb_spec, vb_shape = _values_out(t, d, tm, v_transposed)
    return pl.pallas_call(
        functools.partial(_proj_b_kernel, period=period, v_transposed=v_transposed),
        grid=(t // tm,),
        in_specs=[row, _resident(gkv.shape), _resident(wkv.shape), _resident(gq.shape), _resident(wq.shape)],
        out_specs=[row, row, vb_spec, keep, keep],
        out_shape=[jax.ShapeDtypeStruct((t, d), BF16), jax.ShapeDtypeStruct((t, d), BF16),
                   vb_shape,
                   jax.ShapeDtypeStruct((t // period, d), F32), jax.ShapeDtypeStruct((t // period, d), F32)],
        compiler_params=_params("arbitrary"),
        name="proj_b",
    )(x, gkv, wkv, gq, wq)


def _toeplitz_bias(r_row, rows):
    rb = jnp.broadcast_to(r_row * LOG2E, (rows, r_row.shape[1]))
    return pltpu.roll(rb, 0, 1, stride=1, stride_axis=0)


def _band_prompt_kernel(r_ref, q_ref, k_ref, v_ref, o_ref, vt_ref, bias):
    tq = BAND_TILE
    s_len = q_ref.shape[1]
    n_pairs = q_ref.shape[2] // LANES
    win = PAD + tq

    @pl.when(pl.program_id(1) == 0)
    def _():
        key = lax.broadcasted_iota(jnp.int32, (win, tq), 0)
        qry = lax.broadcasted_iota(jnp.int32, (win, tq), 1)
        dist = qry // CHUNK + LEFT_CHUNKS - key // CHUNK
        in_band = (dist >= 0) & (dist <= LEFT_CHUNKS)
        for pp in range(n_pairs):
            for hh in range(2):
                rb = jnp.broadcast_to(r_ref[pp, hh:hh + 1, :] * LOG2E, (win, win + tq))
                shifted = pltpu.roll(rb, 0, 1, stride=1, stride_axis=0)
                bias[pp, :, hh * tq:(hh + 1) * tq] = jnp.where(in_band, shifted[:, win:win + tq], NEG_INF)

    lane = lax.broadcasted_iota(jnp.int32, (tq, LANES), 1)

    def window(qi):
        q0 = qi * tq
        return max(0, PAD - q0), max(0, q0 - PAD), q0 + tq

    def sweep(pp):
        ps = slice(pp * LANES, (pp + 1) * LANES)
        vt_ref[pp, 0:LANES, :] = v_ref[ps, :]
        vt_ref[pp, LANES:, :] = jnp.ones((vt_ref.shape[1] - LANES, s_len), BF16)

        def scores(qi):
            lo, k_lo, k_hi = window(qi)
            q = q_ref[0, qi * tq:(qi + 1) * tq, ps]
            zero = jnp.zeros_like(q)
            qz = jnp.concatenate([jnp.where(lane < HEAD_DIM, q, zero),
                                  jnp.where(lane >= HEAD_DIM, q, zero)], axis=0)
            return lax.dot_general(k_ref[0, k_lo:k_hi, ps], qz, NT_DIMS,
                                   preferred_element_type=F32) + bias[pp, lo:win, :]

        n_tiles = s_len // tq
        s_next = scores(0)
        yield
        for qi in range(n_tiles):
            s = s_next
            if qi + 1 < n_tiles:
                s_next = scores(qi + 1)
                yield
            _, k_lo, k_hi = window(qi)
            m = jnp.max(s, axis=0, keepdims=True)
            e = jnp.exp2(s - m)
            o = jnp.dot(vt_ref[pp, :, k_lo:k_hi], e.astype(BF16), preferred_element_type=F32)
            o = o[0:LANES] / o[LANES:LANES + 1]
            o = jnp.concatenate([o[0:HEAD_DIM, 0:tq], o[HEAD_DIM:LANES, tq:2 * tq]], axis=0)
            o_ref[0, qi * tq:(qi + 1) * tq, ps] = o.T.astype(BF16)
            yield

    _alternate([sweep(pp) for pp in range(n_pairs)])


def _band_prompt(r, q, k, v):
    b, s, d = q.shape
    width = BAND_PAIRS_PER_STEP * LANES
    blk = pl.BlockSpec((1, s, width), lambda pi, bi: (bi, 0, pi))
    vt_blk = pl.BlockSpec((width, s), lambda pi, bi: (pi, bi))
    rblk = pl.BlockSpec((BAND_PAIRS_PER_STEP, 2, r.shape[2]), lambda pi, bi: (pi, 0, 0))
    win = PAD + BAND_TILE
    return pl.pallas_call(
        _band_prompt_kernel,
        grid=(d // width, b),
        in_specs=[rblk, blk, blk, vt_blk],
        out_specs=blk,
        out_shape=jax.ShapeDtypeStruct((b, s, d), BF16),
        scratch_shapes=[pltpu.VMEM((BAND_PAIRS_PER_STEP, LANES + ONES_ROWS, s), BF16),
                        pltpu.VMEM((BAND_PAIRS_PER_STEP, win, 2 * BAND_TILE), F32)],
        compiler_params=_params("arbitrary", "arbitrary"),
        name="band_prompt",
    )(r, q, k, v)


def _band_sample_kernel(r_ref, q_ref, kn_ref, vn_ref, ck_ref, cv_ref, o_ref):
    nt, d = q_ref.shape[1], q_ref.shape[2]
    lb = ck_ref.shape[1]
    off = SAMPLE_BIAS_OFFSET
    lane = lax.broadcasted_iota(jnp.int32, (nt, LANES), 1)
    pairs = d // LANES

    def scores(p):
        sl = slice(p * LANES, (p + 1) * LANES)
        q = q_ref[0, :, sl]
        zero = jnp.zeros_like(q)
        qz = jnp.concatenate([jnp.where(lane < HEAD_DIM, q, zero),
                              jnp.where(lane >= HEAD_DIM, q, zero)], axis=0)
        shifted = jnp.concatenate([_toeplitz_bias(r_ref[p, hh:hh + 1, :], nt)
                                   for hh in range(2)], axis=0)
        sc = (lax.dot_general(qz, ck_ref[0, :, sl].astype(BF16), NT_DIMS, preferred_element_type=F32)
              + shifted[:, off:off + lb])
        sn = (lax.dot_general(qz, kn_ref[0, :, sl], NT_DIMS, preferred_element_type=F32)
              + shifted[:, off + lb:off + lb + nt])
        return sc, sn

    def weights(sc, sn):
        m = jnp.maximum(jnp.max(sc, axis=-1, keepdims=True), jnp.max(sn, axis=-1, keepdims=True))
        ec = jnp.exp2(sc - m)
        en = jnp.exp2(sn - m)
        inv = 1.0 / (jnp.sum(ec, axis=-1, keepdims=True) + jnp.sum(en, axis=-1, keepdims=True))
        return (ec * inv).astype(BF16), (en * inv).astype(BF16)

    def output(p, pc, pn):
        sl = slice(p * LANES, (p + 1) * LANES)
        o = (jnp.dot(pc, cv_ref[0, :, sl].astype(BF16), preferred_element_type=F32)
             + jnp.dot(pn, vn_ref[0, :, sl], preferred_element_type=F32))
        o_ref[0, :, sl] = jnp.where(lane < HEAD_DIM, o[0:nt], o[nt:2 * nt]).astype(BF16)

    s_next = scores(0)
    w_prev = None
    for p in range(pairs):
        s = s_next
        if p + 1 < pairs:
            s_next = scores(p + 1)
        w = weights(*s)
        if w_prev is not None:
            output(p - 1, *w_prev)
        w_prev = w
        yield
    output(pairs - 1, *w_prev)


def _band_sample(r, q, kn, vn, ck, cv):
    b, nt, d = q.shape
    lb = ck.shape[1]
    new = pl.BlockSpec((1, nt, d), lambda bi: (bi, 0, 0))
    old = pl.BlockSpec((1, lb, d), lambda bi: (bi, 0, 0))
    return _Call(_band_sample_kernel, (b,), [_resident(r.shape), new, new, new, old, old],
                 [r, q, kn, vn, ck, cv], [new], [jax.ShapeDtypeStruct((b, nt, d), BF16)], "band_sample")


def _rope_tables(pos):
    half = HEAD_DIM // 2
    inv = 1.0 / (ROPE_THETA ** (jnp.arange(half, dtype=F32) / half))
    ang = pos.astype(F32)[:, None] * inv[None, :]
    cos, sin = jnp.cos(ang), jnp.sin(ang)
    reps = LANES // HEAD_DIM
    return (jnp.tile(jnp.concatenate([cos, cos], axis=-1), (1, reps)),
            jnp.tile(jnp.concatenate([-sin, sin], axis=-1), (1, reps)))


def _bias_rows(table):
    far = PAD + SAMPLE_BIAS_OFFSET - REL_CLIP
    near = table[:, 2 * REL_CLIP:0:-1]
    rows = jnp.concatenate([jnp.broadcast_to(table[:, 2 * REL_CLIP:], (table.shape[0], far)), near], axis=1)
    return rows.reshape(table.shape[0] // 2, 2, rows.shape[1])


def _bias_rows_t(table):
    heads = table.shape[0]
    lo = BAND_TILE - REL_CLIP
    hi = PAD + 2 * BAND_TILE - lo - (2 * REL_CLIP + 1)
    rows = jnp.concatenate([jnp.broadcast_to(table[:, :1], (heads, lo)), table,
                            jnp.broadcast_to(table[:, 2 * REL_CLIP:], (heads, hi))], axis=1)
    return rows.reshape(heads // 2, 2, rows.shape[1])


def _lambda_init(layer):
    return 0.8 - 0.6 * math.exp(-0.3 * layer)


def kernel(x_prompt, x_sample, cache_a_k, cache_a_v, cache_b_k, cache_b_v, g_attn, w_a_qkv, a_lambda, a_subln, w_a_o, g_kv, w_kv, w_b_q, b_rel, w_b_o, g_mlp, w_ff1, w_ff2, g_final):
    nb, seq, d = x_prompt.shape
    db, nt, _ = x_sample.shape
    past = cache_a_k.shape[2]
    lb = cache_b_k.shape[1]
    a_heads = d // (2 * HEAD_DIM)
    b_heads = d // HEAD_DIM
    assert w_a_qkv.shape[0] == 1 and w_b_q.shape[0] == 1, "one differential and one band layer"
    assert (db * nt) % TOKEN_TILE == 0 and TOKEN_TILE % nt == 0 and seq % TOKEN_TILE == 0
    keep = min(PAD, seq)
    assert keep == TOKEN_TILE and lb == PAD

    lam0 = _lambda_init(0)
    w_qkv = w_a_qkv[0].astype(BF16)
    w_ao = w_a_o[0].astype(BF16)
    w_kvb = w_kv.astype(BF16)
    w_bq = w_b_q[0].astype(BF16)
    w_bo = w_b_o[0].astype(BF16)
    w1 = w_ff1.astype(BF16)
    w2 = w_ff2.astype(BF16)
    g_a = g_attn[0][None]
    g_b = g_attn[1][None]
    g_k = g_kv[None]
    g_m0 = g_mlp[0][None]
    g_m1 = g_mlp[1][None]
    g_f = g_final[None]
    lp = a_lambda[0]
    sg = a_subln[0][None]
    r = _bias_rows(b_rel[0])
    r_t = _bias_rows_t(b_rel[0])

    cos_p, sin_p = _rope_tables(jnp.arange(seq))
    cos_s, sin_s = _rope_tables(jnp.arange(past, past + nt))
    reps = TOKEN_TILE // nt
    cos_s, sin_s = jnp.tile(cos_s, (reps, 1)), jnp.tile(sin_s, (reps, 1))

    ck_a = cache_a_k.reshape(db, past * a_heads, 2 * HEAD_DIM)
    cv_a = cache_a_v.reshape(db, past * a_heads, 2 * HEAD_DIM)
    ck_b = cache_b_k.reshape(db, lb, d)
    cv_b = cache_b_v.reshape(db, lb, d)
    xp = x_prompt.reshape(nb * seq, d)
    xs = x_sample.reshape(db * nt, d)
    psh, ssh = (nb, seq, d), (db, nt, d)

    q_p, ak_p, kb_p, av_p, vt_p = _proj_a(xp, g_a, w_qkv, cos_p, sin_p, True)
    q_s, ak_s, kb_s, av_s, vb_s = _proj_a(xs, g_a, w_qkv, cos_s, sin_s, False)
    o_p = _attn_a_prompt(lp, sg, q_p.reshape(psh), kb_p.reshape(psh), vt_p, lam0).reshape(nb * seq, d)
    (h_p,), (o_s,) = _launch_pair(
        _post(xp, o_p, w_ao, g_m0, w1, w2, 0),
        _attn_a_sample(lp, sg, q_s.reshape(ssh), kb_s.reshape(ssh), vb_s.reshape(ssh), ck_a, cv_a, lam0))
    (h_s,) = _launch(_post(xs, o_s.reshape(db * nt, d), w_ao, g_m0, w1, w2, 0))

    qb_p, kbb_p, vtb_p, bk_p, bv_p = _proj_b(h_p, g_k, w_kvb, g_b, w_bq, seq // TOKEN_TILE, True)
    qb_s, kbb_s, vbb_s, bk_s, bv_s = _proj_b(h_s, g_k, w_kvb, g_b, w_bq, 1, False)
    ob_p = _band_prompt(r_t, qb_p.reshape(psh), kbb_p.reshape(psh), vtb_p).reshape(nb * seq, d)
    (y_p,), (ob_s,) = _launch_pair(
        _post(h_p, ob_p, w_bo, g_m1, w1, w2, 1, g_f),
        _band_sample(r, qb_s.reshape(ssh), kbb_s.reshape(ssh), vbb_s.reshape(ssh), ck_b, cv_b))
    (y_s,) = _launch(_post(h_s, ob_s.reshape(db * nt, d), w_bo, g_m1, w1, w2, 1, g_f))

    return (y_p.reshape(nb, seq, d), y_s.reshape(db, nt, d),
            ak_p.reshape(1, nb, seq, a_heads, 2 * HEAD_DIM), av_p.reshape(1, nb, seq, a_heads, 2 * HEAD_DIM),
            bk_p.reshape(nb, keep, b_heads, HEAD_DIM), bv_p.reshape(nb, keep, b_heads, HEAD_DIM),
            ak_s.reshape(1, db, nt, a_heads, 2 * HEAD_DIM), av_s.reshape(1, db, nt, a_heads, 2 * HEAD_DIM),
            bk_s.reshape(db, nt, b_heads, HEAD_DIM), bv_s.reshape(db, nt, b_heads, HEAD_DIM))
```

```python
import functools
import math
from typing import Callable, NamedTuple

import jax
import jax.numpy as jnp
from jax import lax
from jax.experimental import pallas as pl
from jax.experimental.pallas import tpu as pltpu

HEAD_DIM = 64
CHUNK = 64
LEFT_CHUNKS = 8
PAD = LEFT_CHUNKS * CHUNK
REL_CLIP = 128
ROPE_THETA = 10000.0
EPS = 1e-6
NEG_INF = -1e30
LANES = 128
LOG2E = math.log2(math.e)
QSCALE = HEAD_DIM ** -0.5 * LOG2E

TOKEN_TILE = 512
FF_CHUNK = 1024
ATTN_TILE = 512
ATTN_HEADS_PER_STEP = 4
BAND_PAIRS_PER_STEP = 4
ONES_ROWS = 16
BAND_TILE = 256
SAMPLE_BIAS_OFFSET = 128
VMEM_LIMIT = 56 * 1024 * 1024

F32 = jnp.float32
BF16 = jnp.bfloat16
NT_DIMS = (((1,), (1,)), ((), ()))


def _rms(x, g):
    ms = jnp.mean(x * x, axis=-1, keepdims=True)
    return x * lax.rsqrt(ms + EPS) * g


def _resident(shape):
    return pl.BlockSpec(shape, lambda *_: (0,) * len(shape), pipeline_mode=pl.Buffered(1))


def _resident_layer(stacked_shape, layer):
    rest = tuple(stacked_shape[1:])
    return pl.BlockSpec((None,) + rest, lambda *_: (layer,) + (0,) * len(rest), pipeline_mode=pl.Buffered(1))


def _params(*sem):
    return pltpu.CompilerParams(dimension_semantics=sem, vmem_limit_bytes=VMEM_LIMIT)


class _Call(NamedTuple):
    body: Callable
    grid: tuple
    in_specs: list
    args: list
    out_specs: list
    out_shape: list
    name: str


def _run(stages):
    for _ in stages:
        pass


def _alternate(bodies):
    live = list(bodies)
    while live:
        for g in list(live):
            if next(g, StopIteration) is StopIteration:
                live.remove(g)


def _launch(call):
    outs = pl.pallas_call(lambda *refs: _run(call.body(*refs)), grid=call.grid, in_specs=call.in_specs,
                          out_specs=call.out_specs, out_shape=call.out_shape,
                          compiler_params=_params(*["parallel"] * len(call.grid)),
                          name=call.name)(*call.args)
    return outs


def _launch_pair(heavy, light):
    if heavy.grid != light.grid:
        return _launch(heavy), _launch(light)
    n_in_h, n_in_l, n_out_h = len(heavy.args), len(light.args), len(heavy.out_shape)

    def body(*refs):
        ins, outs = refs[:n_in_h + n_in_l], refs[n_in_h + n_in_l:]
        _alternate([light.body(*ins[n_in_h:], *outs[n_out_h:]), heavy.body(*ins[:n_in_h], *outs[:n_out_h])])

    outs = pl.pallas_call(body, grid=heavy.grid, in_specs=heavy.in_specs + light.in_specs,
                          out_specs=heavy.out_specs + light.out_specs,
                          out_shape=heavy.out_shape + light.out_shape,
                          compiler_params=_params(*["parallel"] * len(heavy.grid)),
                          name=heavy.name + "_" + light.name)(*heavy.args, *light.args)
    return outs[:n_out_h], outs[n_out_h:]


def _diff_lambda(lp, lam0):
    a = jnp.sum(lp[0:1] * lp[1:2], axis=-1, keepdims=True)
    b = jnp.sum(lp[2:3] * lp[3:4], axis=-1, keepdims=True)
    return jnp.exp(a) - jnp.exp(b) + lam0


def _head_norm(o, g, lam0):
    return _rms(o, g) * (1.0 - lam0)


def _bf16_values(v, transposed):
    return (v.T if transposed else v).astype(BF16)


def _proj_a_kernel(x_ref, g_ref, w_ref, cos_ref, sin_ref,
                   q_ref, k_ref, kb_ref, v_ref, vb_ref, *, v_transposed):
    d = x_ref.shape[1]
    xn = _rms(x_ref[...], g_ref[...]).astype(BF16)
    cos = cos_ref[...]
    sin = sin_ref[...]
    lane = lax.broadcasted_iota(jnp.int32, cos.shape, 1)
    first_half = (lane % HEAD_DIM) < (HEAD_DIM // 2)

    def rope(a):
        partner = jnp.where(first_half, pltpu.roll(a, LANES - HEAD_DIM // 2, 1),
                            pltpu.roll(a, HEAD_DIM // 2, 1))
        return a * cos + partner * sin

    q = jnp.dot(xn, w_ref[:, 0:d], preferred_element_type=F32)
    for c in range(d // LANES):
        sl = slice(c * LANES, (c + 1) * LANES)
        q_ref[:, sl] = (rope(q[:, sl]) * QSCALE).astype(BF16)
    tm, heads = x_ref.shape[0], d // LANES
    k = jnp.dot(xn, w_ref[:, d:2 * d], preferred_element_type=F32)
    for c in range(heads):
        sl = slice(c * LANES, (c + 1) * LANES)
        kr = rope(k[:, sl])
        k_ref[pl.ds(c, tm, stride=heads), :] = kr
        kb_ref[:, sl] = kr.astype(BF16)
    v = jnp.dot(xn, w_ref[:, 2 * d:3 * d], preferred_element_type=F32)
    for c in range(heads):
        v_ref[pl.ds(c, tm, stride=heads), :] = v[:, c * LANES:(c + 1) * LANES]
    vb_ref[...] = _bf16_values(v, v_transposed)


def _values_out(t, d, tm, transposed):
    if transposed:
        return pl.BlockSpec((d, tm), lambda i: (0, i)), jax.ShapeDtypeStruct((d, t), BF16)
    return pl.BlockSpec((tm, d), lambda i: (i, 0)), jax.ShapeDtypeStruct((t, d), BF16)


def _proj_a(x, g, w, cos, sin, v_transposed):
    t, d = x.shape
    tm = TOKEN_TILE
    period = cos.shape[0] // tm
    heads = d // LANES
    row = pl.BlockSpec((tm, d), lambda i: (i, 0))
    native = pl.BlockSpec((tm * heads, LANES), lambda i: (i, 0))
    tab = pl.BlockSpec((tm, LANES), lambda i: (i % period, 0))
    vb_spec, vb_shape = _values_out(t, d, tm, v_transposed)
    return pl.pallas_call(
        functools.partial(_proj_a_kernel, v_transposed=v_transposed),
        grid=(t // tm,),
        in_specs=[row, _resident((1, d)), _resident((d, 3 * d)), tab, tab],
        out_specs=[row, native, row, native, vb_spec],
        out_shape=[jax.ShapeDtypeStruct((t, d), BF16), jax.ShapeDtypeStruct((t * heads, LANES), F32),
                   jax.ShapeDtypeStruct((t, d), BF16), jax.ShapeDtypeStruct((t * heads, LANES), F32),
                   vb_shape],
        compiler_params=_params("parallel"),
        name="proj_a",
    )(x, g, w, cos, sin)


def _attn_a_prompt_kernel(lp_ref, sg_ref, q_ref, k_ref, v_ref, o_ref, vt_ref, *, lam0):
    t = ATTN_TILE
    s_len = q_ref.shape[1]
    lam = _diff_lambda(lp_ref[...], lam0)
    sg = sg_ref[...]

    th = t // 2
    lane = lax.broadcasted_iota(jnp.int32, (th, LANES), 1)
    key_chunk = lax.broadcasted_iota(jnp.int32, (th, t), 0) // CHUNK
    qry_chunk = (lax.broadcasted_iota(jnp.int32, (th, t), 1) % th) // CHUNK
    diag_half = qry_chunk >= key_chunk

    def sweep(hh):
        hs = slice(hh * LANES, (hh + 1) * LANES)
        vt_ref[hh, 0:LANES, :] = v_ref[hs, :]
        vt_ref[hh, LANES:, :] = jnp.ones((vt_ref.shape[1] - LANES, s_len), BF16)

        def query_rows(qi, halves):
            rows = []
            for u in halves:
                q = q_ref[0, qi * t + u * th:qi * t + (u + 1) * th, hs]
                zero = jnp.zeros_like(q)
                rows += [jnp.where(lane < HEAD_DIM, q, zero), jnp.where(lane >= HEAD_DIM, q, zero)]
            return jnp.concatenate(rows, axis=0)

        def scores(qi, j):
            if j < qi:
                return lax.dot_general(k_ref[0, j * t:(j + 1) * t, hs], query_rows(qi, (0, 1)), NT_DIMS,
                                       preferred_element_type=F32)
            top = lax.dot_general(k_ref[0, qi * t:qi * t + th, hs], query_rows(qi, (0, 1)), NT_DIMS,
                                  preferred_element_type=F32)
            bot = lax.dot_general(k_ref[0, qi * t + th:(qi + 1) * t, hs], query_rows(qi, (1,)), NT_DIMS,
                                  preferred_element_type=F32)
            first = jnp.where(diag_half, top[:, 0:t], NEG_INF)
            second = jnp.concatenate([top[:, t:2 * t], jnp.where(diag_half, bot, NEG_INF)], axis=0)
            return first, second

        def update(s, k_lo, k_hi, m, acc):
            m_new = jnp.maximum(m, jnp.max(s, axis=0, keepdims=True))
            p = jnp.exp2(s - m_new)
            acc = jnp.exp2(m - m_new) * acc + jnp.dot(vt_ref[hh, :, k_lo:k_hi], p.astype(BF16),
                                                      preferred_element_type=F32)
            return m_new, acc

        pairs = [(qi, j) for qi in range(s_len // t) for j in range(qi + 1)]
        s_next = scores(*pairs[0])
        yield
        for n, (qi, j) in enumerate(pairs):
            s = s_next
            if n + 1 < len(pairs):
                s_next = scores(*pairs[n + 1])
                yield
            if j == 0:
                m = jnp.full((1, 2 * t), NEG_INF, F32)
                acc = jnp.zeros((vt_ref.shape[1], 2 * t), F32)
            if j < qi:
                m, acc = update(s, j * t, (j + 1) * t, m, acc)
            else:
                _, acc0 = update(s[0], qi * t, qi * t + th, m[:, 0:t], acc[:, 0:t])
                _, acc1 = update(s[1], qi * t, (qi + 1) * t, m[:, t:2 * t], acc[:, t:2 * t])
                halves = []
                for a in (acc0, acc1):
                    a = a[0:LANES] / a[LANES:LANES + 1]
                    halves.append(a[:, 0:th] - lam * a[:, th:t])
                o = jnp.concatenate(halves, axis=1)
                o = o * lax.rsqrt(jnp.mean(o * o, axis=0, keepdims=True) + EPS)
                o_ref[0, qi * t:(qi + 1) * t, hs] = (o.T * sg * (1.0 - lam0)).astype(BF16)
            yield

    _alternate([sweep(hh) for hh in range(q_ref.shape[2] // LANES)])


def _attn_a_prompt(lp, sg, q, k, v, lam0):
    b, s, d = q.shape
    width = ATTN_HEADS_PER_STEP * LANES
    blk = pl.BlockSpec((1, s, width), lambda bi, hi: (bi, 0, hi))
    vt_blk = pl.BlockSpec((width, s), lambda bi, hi: (hi, bi))
    return pl.pallas_call(
        functools.partial(_attn_a_prompt_kernel, lam0=lam0),
        grid=(b, d // width),
        in_specs=[_resident(lp.shape), _resident(sg.shape), blk, blk, vt_blk],
        out_specs=blk,
        out_shape=jax.ShapeDtypeStruct((b, s, d), BF16),
        scratch_shapes=[pltpu.VMEM((ATTN_HEADS_PER_STEP, LANES + ONES_ROWS, s), BF16)],
        compiler_params=_params("parallel", "parallel"),
        name="attn_a_prompt",
    )(lp, sg, q, k, v)


def _attn_a_sample_kernel(lp_ref, sg_ref, q_ref, kn_ref, vn_ref, ck_ref, cv_ref, o_ref, *, lam0):
    nt, d = q_ref.shape[1], q_ref.shape[2]
    heads = d // LANES
    past = ck_ref.shape[1] // heads
    lam = _diff_lambda(lp_ref[...], lam0)
    sg = sg_ref[...]
    lane = lax.broadcasted_iota(jnp.int32, (nt, LANES), 1)

    def scores(h):
        sl = slice(h * LANES, (h + 1) * LANES)
        q = q_ref[0, :, sl]
        zero = jnp.zeros_like(q)
        qz = jnp.concatenate([jnp.where(lane < HEAD_DIM, q, zero),
                              jnp.where(lane >= HEAD_DIM, q, zero)], axis=0)
        kc = ck_ref[0, pl.ds(h, past, stride=heads), :].astype(BF16)
        return (lax.dot_general(qz, kc, NT_DIMS, preferred_element_type=F32),
                lax.dot_general(qz, kn_ref[0, :, sl], NT_DIMS, preferred_element_type=F32))

    def weights(sc, sn):
        m = jnp.maximum(jnp.max(sc, axis=-1, keepdims=True), jnp.max(sn, axis=-1, keepdims=True))
        ec = jnp.exp2(sc - m)
        en = jnp.exp2(sn - m)
        inv = 1.0 / (jnp.sum(ec, axis=-1, keepdims=True) + jnp.sum(en, axis=-1, keepdims=True))
        pc = ec * inv
        pn = en * inv
        return ((pc[0:nt] - lam * pc[nt:2 * nt]).astype(BF16),
                (pn[0:nt] - lam * pn[nt:2 * nt]).astype(BF16))

    def output(h, pc, pn):
        sl = slice(h * LANES, (h + 1) * LANES)
        vc = cv_ref[0, pl.ds(h, past, stride=heads), :].astype(BF16)
        o = (jnp.dot(pc, vc, preferred_element_type=F32)
             + jnp.dot(pn, vn_ref[0, :, sl], preferred_element_type=F32))
        o_ref[0, :, sl] = _head_norm(o, sg, lam0).astype(BF16)

    s_next = scores(0)
    p_prev = None
    for h in range(heads):
        s = s_next
        if h + 1 < heads:
            s_next = scores(h + 1)
        p = weights(*s)
        if p_prev is not None:
            output(h - 1, *p_prev)
        p_prev = p
        yield
    output(heads - 1, *p_prev)


def _attn_a_sample(lp, sg, q, kn, vn, ck, cv, lam0):
    b, nt, d = q.shape
    new = pl.BlockSpec((1, nt, d), lambda bi: (bi, 0, 0))
    old = pl.BlockSpec((1,) + ck.shape[1:], lambda bi: (bi, 0, 0))
    return _Call(functools.partial(_attn_a_sample_kernel, lam0=lam0), (b,),
                 [_resident(lp.shape), _resident(sg.shape), new, new, new, old, old],
                 [lp, sg, q, kn, vn, ck, cv], [new], [jax.ShapeDtypeStruct((b, nt, d), BF16)],
                 "attn_a_sample")


def _post_kernel(*refs, final):
    if final:
        x_ref, o_ref, wo_ref, g_ref, w1_ref, w2_ref, gf_ref, out_ref = refs
    else:
        x_ref, o_ref, wo_ref, g_ref, w1_ref, w2_ref, out_ref = refs
    h = x_ref[...] + jnp.dot(o_ref[...], wo_ref[...], preferred_element_type=F32)
    hn = _rms(h, g_ref[...]).astype(BF16)
    yield
    d_ff = w1_ref.shape[1]
    mlp = None
    for c in range(d_ff // FF_CHUNK):
        sl = slice(c * FF_CHUNK, (c + 1) * FF_CHUNK)
        a = jnp.dot(hn, w1_ref[:, sl], preferred_element_type=F32)
        u = jnp.square(jnp.maximum(a, 0.0)).astype(BF16)
        yield
        part = jnp.dot(u, w2_ref[sl, :], preferred_element_type=F32)
        mlp = part if mlp is None else mlp + part
        yield
    h = h + mlp
    if final:
        h = _rms(h, gf_ref[...])
    out_ref[...] = h


def _post(x, o, wo, g, w1, w2, layer, gf=None):
    t, d = x.shape
    tm = TOKEN_TILE
    final = gf is not None
    row = pl.BlockSpec((tm, d), lambda i: (i, 0))
    in_specs = [row, row, _resident(wo.shape), _resident(g.shape),
                _resident_layer(w1.shape, layer), _resident_layer(w2.shape, layer)]
    args = [x, o, wo, g, w1, w2]
    if final:
        in_specs.append(_resident(gf.shape))
        args.append(gf)
    return _Call(functools.partial(_post_kernel, final=final), (t // tm,), in_specs, args,
                 [row], [jax.ShapeDtypeStruct((t, d), F32)], "post_final" if final else "post")


def _proj_b_kernel(x_ref, gkv_ref, wkv_ref, gq_ref, wq_ref,
                   q_ref, kb_ref, vb_ref, k_ref, v_ref, *, period, v_transposed):
    d = x_ref.shape[1]
    x = x_ref[...]
    xs = x * lax.rsqrt(jnp.mean(x * x, axis=-1, keepdims=True) + EPS)
    xkv = (xs * gkv_ref[...]).astype(BF16)
    xq = (xs * gq_ref[...]).astype(BF16)
    q_ref[...] = (jnp.dot(xq, wq_ref[...], preferred_element_type=F32) * QSCALE).astype(BF16)
    k = jnp.dot(xkv, wkv_ref[:, 0:d], preferred_element_type=F32)
    v = jnp.dot(xkv, wkv_ref[:, d:2 * d], preferred_element_type=F32)
    kb_ref[...] = k.astype(BF16)
    vb_ref[...] = _bf16_values(v, v_transposed)

    @pl.when(pl.program_id(0) % period == period - 1)
    def _():
        k_ref[...] = k
        v_ref[...] = v


def _proj_b(x, gkv, wkv, gq, wq, period, v_transposed):
    t, d = x.shape
    tm = TOKEN_TILE
    row = pl.BlockSpec((tm, d), lambda i: (i, 0))
    keep = pl.BlockSpec((tm, d), lambda i: (i // period, 0))
    vb_spec, vb_shape = _values_out(t, d, tm, v_transposed)
    return pl.pallas_call(
        functools.partial(_proj_b_kernel, period=period, v_transposed=v_transposed),
        grid=(t // tm,),
        in_specs=[row, _resident(gkv.shape), _resident(wkv.shape), _resident(gq.shape), _resident(wq.shape)],
        out_specs=[row, row, vb_spec, keep, keep],
        out_shape=[jax.ShapeDtypeStruct((t, d), BF16), jax.ShapeDtypeStruct((t, d), BF16),
                   vb_shape,
                   jax.ShapeDtypeStruct((t // period, d), F32), jax.ShapeDtypeStruct((t // period, d), F32)],
        compiler_params=_params("arbitrary"),
        name="proj_b",
    )(x, gkv, wkv, gq, wq)


def _toeplitz_bias(r_row, rows):
    rb = jnp.broadcast_to(r_row * LOG2E, (rows, r_row.shape[1]))
    return pltpu.roll(rb, 0, 1, stride=1, stride_axis=0)


def _band_prompt_kernel(r_ref, q_ref, k_ref, v_ref, o_ref, vt_ref, bias):
    tq = BAND_TILE
    s_len = q_ref.shape[1]
    n_pairs = q_ref.shape[2] // LANES
    win = PAD + tq

    @pl.when(pl.program_id(1) == 0)
    def _():
        key = lax.broadcasted_iota(jnp.int32, (win, tq), 0)
        qry = lax.broadcasted_iota(jnp.int32, (win, tq), 1)
        dist = qry // CHUNK + LEFT_CHUNKS - key // CHUNK
        in_band = (dist >= 0) & (dist <= LEFT_CHUNKS)
        for pp in range(n_pairs):
            for hh in range(2):
                rb = jnp.broadcast_to(r_ref[pp, hh:hh + 1, :] * LOG2E, (win, win + tq))
                shifted = pltpu.roll(rb, 0, 1, stride=1, stride_axis=0)
                bias[pp, :, hh * tq:(hh + 1) * tq] = jnp.where(in_band, shifted[:, win:win + tq], NEG_INF)

    lane = lax.broadcasted_iota(jnp.int32, (tq, LANES), 1)

    def window(qi):
        q0 = qi * tq
        return max(0, PAD - q0), max(0, q0 - PAD), q0 + tq

    def sweep(pp):
        ps = slice(pp * LANES, (pp + 1) * LANES)
        vt_ref[pp, 0:LANES, :] = v_ref[ps, :]
        vt_ref[pp, LANES:, :] = jnp.ones((vt_ref.shape[1] - LANES, s_len), BF16)

        def scores(qi):
            lo, k_lo, k_hi = window(qi)
            q = q_ref[0, qi * tq:(qi + 1) * tq, ps]
            zero = jnp.zeros_like(q)
            qz = jnp.concatenate([jnp.where(lane < HEAD_DIM, q, zero),
                                  jnp.where(lane >= HEAD_DIM, q, zero)], axis=0)
            return lax.dot_general(k_ref[0, k_lo:k_hi, ps], qz, NT_DIMS,
                                   preferred_element_type=F32) + bias[pp, lo:win, :]

        n_tiles = s_len // tq
        s_next = scores(0)
        yield
        for qi in range(n_tiles):
            s = s_next
            if qi + 1 < n_tiles:
                s_next = scores(qi + 1)
                yield
            _, k_lo, k_hi = window(qi)
            m = jnp.max(s, axis=0, keepdims=True)
            e = jnp.exp2(s - m)
            o = jnp.dot(vt_ref[pp, :, k_lo:k_hi], e.astype(BF16), preferred_element_type=F32)
            o = o[0:LANES] / o[LANES:LANES + 1]
            o = jnp.concatenate([o[0:HEAD_DIM, 0:tq], o[HEAD_DIM:LANES, tq:2 * tq]], axis=0)
            o_ref[0, qi * tq:(qi + 1) * tq, ps] = o.T.astype(BF16)
            yield

    _alternate([sweep(pp) for pp in range(n_pairs)])


def _band_prompt(r, q, k, v):
    b, s, d = q.shape
    width = BAND_PAIRS_PER_STEP * LANES
    blk = pl.BlockSpec((1, s, width), lambda pi, bi: (bi, 0, pi))
    vt_blk = pl.BlockSpec((width, s), lambda pi, bi: (pi, bi))
    rblk = pl.BlockSpec((BAND_PAIRS_PER_STEP, 2, r.shape[2]), lambda pi, bi: (pi, 0, 0))
    win = PAD + BAND_TILE
    return pl.pallas_call(
        _band_prompt_kernel,
        grid=(d // width, b),
        in_specs=[rblk, blk, blk, vt_blk],
        out_specs=blk,
        out_shape=jax.ShapeDtypeStruct((b, s, d), BF16),
        scratch_shapes=[pltpu.VMEM((BAND_PAIRS_PER_STEP, LANES + ONES_ROWS, s), BF16),
                        pltpu.VMEM((BAND_PAIRS_PER_STEP, win, 2 * BAND_TILE), F32)],
        compiler_params=_params("arbitrary", "arbitrary"),
        name="band_prompt",
    )(r, q, k, v)


def _band_sample_kernel(r_ref, q_ref, kn_ref, vn_ref, ck_ref, cv_ref, o_ref):
    nt, d = q_ref.shape[1], q_ref.shape[2]
    lb = ck_ref.shape[1]
    off = SAMPLE_BIAS_OFFSET
    lane = lax.broadcasted_iota(jnp.int32, (nt, LANES), 1)
    pairs = d // LANES

    def scores(p):
        sl = slice(p * LANES, (p + 1) * LANES)
        q = q_ref[0, :, sl]
        zero = jnp.zeros_like(q)
        qz = jnp.concatenate([jnp.where(lane < HEAD_DIM, q, zero),
                              jnp.where(lane >= HEAD_DIM, q, zero)], axis=0)
        shifted = jnp.concatenate([_toeplitz_bias(r_ref[p, hh:hh + 1, :], nt)
                                   for hh in range(2)], axis=0)
        sc = (lax.dot_general(qz, ck_ref[0, :, sl].astype(BF16), NT_DIMS, preferred_element_type=F32)
              + shifted[:, off:off + lb])
        sn = (lax.dot_general(qz, kn_ref[0, :, sl], NT_DIMS, preferred_element_type=F32)
              + shifted[:, off + lb:off + lb + nt])
        return sc, sn

    def weights(sc, sn):
        m = jnp.maximum(jnp.max(sc, axis=-1, keepdims=True), jnp.max(sn, axis=-1, keepdims=True))
        ec = jnp.exp2(sc - m)
        en = jnp.exp2(sn - m)
        inv = 1.0 / (jnp.sum(ec, axis=-1, keepdims=True) + jnp.sum(en, axis=-1, keepdims=True))
        return (ec * inv).astype(BF16), (en * inv).astype(BF16)

    def output(p, pc, pn):
        sl = slice(p * LANES, (p + 1) * LANES)
        o = (jnp.dot(pc, cv_ref[0, :, sl].astype(BF16), preferred_element_type=F32)
             + jnp.dot(pn, vn_ref[0, :, sl], preferred_element_type=F32))
        o_ref[0, :, sl] = jnp.where(lane < HEAD_DIM, o[0:nt], o[nt:2 * nt]).astype(BF16)

    s_next = scores(0)
    w_prev = None
    for p in range(pairs):
        s = s_next
        if p + 1 < pairs:
            s_next = scores(p + 1)
        w = weights(*s)
        if w_prev is not None:
            output(p - 1, *w_prev)
        w_prev = w
        yield
    output(pairs - 1, *w_prev)


def _band_sample(r, q, kn, vn, ck, cv):
    b, nt, d = q.shape
    lb = ck.shape[1]
    new = pl.BlockSpec((1, nt, d), lambda bi: (bi, 0, 0))
    old = pl.BlockSpec((1, lb, d), lambda bi: (bi, 0, 0))
    return _Call(_band_sample_kernel, (b,), [_resident(r.shape), new, new, new, old, old],
                 [r, q, kn, vn, ck, cv], [new], [jax.ShapeDtypeStruct((b, nt, d), BF16)], "band_sample")


def _rope_tables(pos):
    half = HEAD_DIM // 2
    inv = 1.0 / (ROPE_THETA ** (jnp.arange(half, dtype=F32) / half))
    ang = pos.astype(F32)[:, None] * inv[None, :]
    cos, sin = jnp.cos(ang), jnp.sin(ang)
    reps = LANES // HEAD_DIM
    return (jnp.tile(jnp.concatenate([cos, cos], axis=-1), (1, reps)),
            jnp.tile(jnp.concatenate([-sin, sin], axis=-1), (1, reps)))


def _bias_rows(table):
    far = PAD + SAMPLE_BIAS_OFFSET - REL_CLIP
    near = table[:, 2 * REL_CLIP:0:-1]
    rows = jnp.concatenate([jnp.broadcast_to(table[:, 2 * REL_CLIP:], (table.shape[0], far)), near], axis=1)
    return rows.reshape(table.shape[0] // 2, 2, rows.shape[1])


def _bias_rows_t(table):
    heads = table.shape[0]
    lo = BAND_TILE - REL_CLIP
    hi = PAD + 2 * BAND_TILE - lo - (2 * REL_CLIP + 1)
    rows = jnp.concatenate([jnp.broadcast_to(table[:, :1], (heads, lo)), table,
                            jnp.broadcast_to(table[:, 2 * REL_CLIP:], (heads, hi))], axis=1)
    return rows.reshape(heads // 2, 2, rows.shape[1])


def _lambda_init(layer):
    return 0.8 - 0.6 * math.exp(-0.3 * layer)


def kernel(x_prompt, x_sample, cache_a_k, cache_a_v, cache_b_k, cache_b_v, g_attn, w_a_qkv, a_lambda, a_subln, w_a_o, g_kv, w_kv, w_b_q, b_rel, w_b_o, g_mlp, w_ff1, w_ff2, g_final):
    nb, seq, d = x_prompt.shape
    db, nt, _ = x_sample.shape
    past = cache_a_k.shape[2]
    lb = cache_b_k.shape[1]
    a_heads = d // (2 * HEAD_DIM)
    b_heads = d // HEAD_DIM
    assert w_a_qkv.shape[0] == 1 and w_b_q.shape[0] == 1, "one differential and one band layer"
    assert (db * nt) % TOKEN_TILE == 0 and TOKEN_TILE % nt == 0 and seq % TOKEN_TILE == 0
    keep = min(PAD, seq)
    assert keep == TOKEN_TILE and lb == PAD

    lam0 = _lambda_init(0)
    w_qkv = w_a_qkv[0].astype(BF16)
    w_ao = w_a_o[0].astype(BF16)
    w_kvb = w_kv.astype(BF16)
    w_bq = w_b_q[0].astype(BF16)
    w_bo = w_b_o[0].astype(BF16)
    w1 = w_ff1.astype(BF16)
    w2 = w_ff2.astype(BF16)
    g_a = g_attn[0][None]
    g_b = g_attn[1][None]
    g_k = g_kv[None]
    g_m0 = g_mlp[0][None]
    g_m1 = g_mlp[1][None]
    g_f = g_final[None]
    lp = a_lambda[0]
    sg = a_subln[0][None]
    r = _bias_rows(b_rel[0])
    r_t = _bias_rows_t(b_rel[0])

    cos_p, sin_p = _rope_tables(jnp.arange(seq))
    cos_s, sin_s = _rope_tables(jnp.arange(past, past + nt))
    reps = TOKEN_TILE // nt
    cos_s, sin_s = jnp.tile(cos_s, (reps, 1)), jnp.tile(sin_s, (reps, 1))

    ck_a = cache_a_k.reshape(db, past * a_heads, 2 * HEAD_DIM)
    cv_a = cache_a_v.reshape(db, past * a_heads, 2 * HEAD_DIM)
    ck_b = cache_b_k.reshape(db, lb, d)
    cv_b = cache_b_v.reshape(db, lb, d)
    xp = x_prompt.reshape(nb * seq, d)
    xs = x_sample.reshape(db * nt, d)
    psh, ssh = (nb, seq, d), (db, nt, d)

    q_p, ak_p, kb_p, av_p, vt_p = _proj_a(xp, g_a, w_qkv, cos_p, sin_p, True)
    q_s, ak_s, kb_s, av_s, vb_s = _proj_a(xs, g_a, w_qkv, cos_s, sin_s, False)
    o_p = _attn_a_prompt(lp, sg, q_p.reshape(psh), kb_p.reshape(psh), vt_p, lam0).reshape(nb * seq, d)
    (h_p,), (o_s,) = _launch_pair(
        _post(xp, o_p, w_ao, g_m0, w1, w2, 0),
        _attn_a_sample(lp, sg, q_s.reshape(ssh), kb_s.reshape(ssh), vb_s.reshape(ssh), ck_a, cv_a, lam0))
    (h_s,) = _launch(_post(xs, o_s.reshape(db * nt, d), w_ao, g_m0, w1, w2, 0))

    qb_p, kbb_p, vtb_p, bk_p, bv_p = _proj_b(h_p, g_k, w_kvb, g_b, w_bq, seq // TOKEN_TILE, True)
    qb_s, kbb_s, vbb_s, bk_s, bv_s = _proj_b(h_s, g_k, w_kvb, g_b, w_bq, 1, False)
    ob_p = _band_prompt(r_t, qb_p.reshape(psh), kbb_p.reshape(psh), vtb_p).reshape(nb * seq, d)
    (y_p,), (ob_s,) = _launch_pair(
        _post(h_p, ob_p, w_bo, g_m1, w1, w2, 1, g_f),
        _band_sample(r, qb_s.reshape(ssh), kbb_s.reshape(ssh), vbb_s.reshape(ssh), ck_b, cv_b))
    (y_s,) = _launch(_post(h_s, ob_s.reshape(db * nt, d), w_bo, g_m1, w1, w2, 1, g_f))

    return (y_p.reshape(nb, seq, d), y_s.reshape(db, nt, d),
            ak_p.reshape(1, nb, seq, a_heads, 2 * HEAD_DIM), av_p.reshape(1, nb, seq, a_heads, 2 * HEAD_DIM),
            bk_p.reshape(nb, keep, b_heads, HEAD_DIM), bv_p.reshape(nb, keep, b_heads, HEAD_DIM),
            ak_s.reshape(1, db, nt, a_heads, 2 * HEAD_DIM), av_s.reshape(1, db, nt, a_heads, 2 * HEAD_DIM),
            bk_s.reshape(db, nt, b_heads, HEAD_DIM), bv_s.reshape(db, nt, b_heads, HEAD_DIM))
```

```python
import functools
import math
from typing import Callable, NamedTuple

import jax
import jax.numpy as jnp
from jax import lax
from jax.experimental import pallas as pl
from jax.experimental.pallas import tpu as pltpu

HEAD_DIM = 64
CHUNK = 64
LEFT_CHUNKS = 8
PAD = LEFT_CHUNKS * CHUNK
REL_CLIP = 128
ROPE_THETA = 10000.0
EPS = 1e-6
NEG_INF = -1e30
LANES = 128
LOG2E = math.log2(math.e)
QSCALE = HEAD_DIM ** -0.5 * LOG2E

TOKEN_TILE = 512
FF_CHUNK = 1024
ATTN_TILE = 256
ATTN_HEADS_PER_STEP = 2
BAND_PAIRS_PER_STEP = 4
ONES_ROWS = 16
BAND_TILE = 128
SAMPLE_BIAS_OFFSET = 128
VMEM_LIMIT = 56 * 1024 * 1024

F32 = jnp.float32
BF16 = jnp.bfloat16
NT_DIMS = (((1,), (1,)), ((), ()))


def _rms(x, g):
    ms = jnp.mean(x * x, axis=-1, keepdims=True)
    return x * lax.rsqrt(ms + EPS) * g


def _resident(shape):
    return pl.BlockSpec(shape, lambda *_: (0,) * len(shape), pipeline_mode=pl.Buffered(1))


def _resident_layer(stacked_shape, layer):
    rest = tuple(stacked_shape[1:])
    return pl.BlockSpec((None,) + rest, lambda *_: (layer,) + (0,) * len(rest), pipeline_mode=pl.Buffered(1))


def _params(*sem):
    return pltpu.CompilerParams(dimension_semantics=sem, vmem_limit_bytes=VMEM_LIMIT)


class _Call(NamedTuple):
    body: Callable
    grid: tuple
    in_specs: list
    args: list
    out_specs: list
    out_shape: list
    name: str


def _run(stages):
    for _ in stages:
        pass


def _alternate(bodies):
    live = list(bodies)
    while live:
        for g in list(live):
            if next(g, StopIteration) is StopIteration:
                live.remove(g)


def _launch(call):
    outs = pl.pallas_call(lambda *refs: _run(call.body(*refs)), grid=call.grid, in_specs=call.in_specs,
                          out_specs=call.out_specs, out_shape=call.out_shape,
                          compiler_params=_params(*["parallel"] * len(call.grid)),
                          name=call.name)(*call.args)
    return outs


def _launch_pair(heavy, light):
    if heavy.grid != light.grid:
        return _launch(heavy), _launch(light)
    n_in_h, n_in_l, n_out_h = len(heavy.args), len(light.args), len(heavy.out_shape)

    def body(*refs):
        ins, outs = refs[:n_in_h + n_in_l], refs[n_in_h + n_in_l:]
        _alternate([light.body(*ins[n_in_h:], *outs[n_out_h:]), heavy.body(*ins[:n_in_h], *outs[:n_out_h])])

    outs = pl.pallas_call(body, grid=heavy.grid, in_specs=heavy.in_specs + light.in_specs,
                          out_specs=heavy.out_specs + light.out_specs,
                          out_shape=heavy.out_shape + light.out_shape,
                          compiler_params=_params(*["parallel"] * len(heavy.grid)),
                          name=heavy.name + "_" + light.name)(*heavy.args, *light.args)
    return outs[:n_out_h], outs[n_out_h:]


def _diff_lambda(lp, lam0):
    a = jnp.sum(lp[0:1] * lp[1:2], axis=-1, keepdims=True)
    b = jnp.sum(lp[2:3] * lp[3:4], axis=-1, keepdims=True)
    return jnp.exp(a) - jnp.exp(b) + lam0


def _head_norm(o, g, lam0):
    return _rms(o, g) * (1.0 - lam0)


def _bf16_values(v, transposed):
    return (v.T if transposed else v).astype(BF16)


def _proj_a_kernel(x_ref, g_ref, w_ref, cos_ref, sin_ref,
                   q_ref, k_ref, kb_ref, v_ref, vb_ref, *, v_transposed):
    d = x_ref.shape[1]
    xn = _rms(x_ref[...], g_ref[...]).astype(BF16)
    cos = cos_ref[...]
    sin = sin_ref[...]
    lane = lax.broadcasted_iota(jnp.int32, cos.shape, 1)
    first_half = (lane % HEAD_DIM) < (HEAD_DIM // 2)

    def rope(a):
        partner = jnp.where(first_half, pltpu.roll(a, LANES - HEAD_DIM // 2, 1),
                            pltpu.roll(a, HEAD_DIM // 2, 1))
        return a * cos + partner * sin

    q = jnp.dot(xn, w_ref[:, 0:d], preferred_element_type=F32)
    for c in range(d // LANES):
        sl = slice(c * LANES, (c + 1) * LANES)
        q_ref[:, sl] = (rope(q[:, sl]) * QSCALE).astype(BF16)
    tm, heads = x_ref.shape[0], d // LANES
    k = jnp.dot(xn, w_ref[:, d:2 * d], preferred_element_type=F32)
    for c in range(heads):
        sl = slice(c * LANES, (c + 1) * LANES)
        kr = rope(k[:, sl])
        k_ref[pl.ds(c, tm, stride=heads), :] = kr
        kb_ref[:, sl] = kr.astype(BF16)
    v = jnp.dot(xn, w_ref[:, 2 * d:3 * d], preferred_element_type=F32)
    for c in range(heads):
        v_ref[pl.ds(c, tm, stride=heads), :] = v[:, c * LANES:(c + 1) * LANES]
    vb_ref[...] = _bf16_values(v, v_transposed)


def _values_out(t, d, tm, transposed):
    if transposed:
        return pl.BlockSpec((d, tm), lambda i: (0, i)), jax.ShapeDtypeStruct((d, t), BF16)
    return pl.BlockSpec((tm, d), lambda i: (i, 0)), jax.ShapeDtypeStruct((t, d), BF16)


def _proj_a(x, g, w, cos, sin, v_transposed):
    t, d = x.shape
    tm = TOKEN_TILE
    period = cos.shape[0] // tm
    heads = d // LANES
    row = pl.BlockSpec((tm, d), lambda i: (i, 0))
    native = pl.BlockSpec((tm * heads, LANES), lambda i: (i, 0))
    tab = pl.BlockSpec((tm, LANES), lambda i: (i % period, 0))
    vb_spec, vb_shape = _values_out(t, d, tm, v_transposed)
    return pl.pallas_call(
        functools.partial(_proj_a_kernel, v_transposed=v_transposed),
        grid=(t // tm,),
        in_specs=[row, _resident((1, d)), _resident((d, 3 * d)), tab, tab],
        out_specs=[row, native, row, native, vb_spec],
        out_shape=[jax.ShapeDtypeStruct((t, d), BF16), jax.ShapeDtypeStruct((t * heads, LANES), F32),
                   jax.ShapeDtypeStruct((t, d), BF16), jax.ShapeDtypeStruct((t * heads, LANES), F32),
                   vb_shape],
        compiler_params=_params("parallel"),
        name="proj_a",
    )(x, g, w, cos, sin)


def _attn_a_prompt_kernel(lp_ref, sg_ref, q_ref, k_ref, v_ref, o_ref, vt_ref, *, lam0):
    t = ATTN_TILE
    s_len = q_ref.shape[1]
    lam = _diff_lambda(lp_ref[...], lam0)
    sg = sg_ref[...]

    th = t // 2
    lane = lax.broadcasted_iota(jnp.int32, (th, LANES), 1)
    key_chunk = lax.broadcasted_iota(jnp.int32, (th, t), 0) // CHUNK
    qry_chunk = (lax.broadcasted_iota(jnp.int32, (th, t), 1) % th) // CHUNK
    diag_half = qry_chunk >= key_chunk

    def sweep(hh):
        hs = slice(hh * LANES, (hh + 1) * LANES)
        vt_ref[hh, 0:LANES, :] = v_ref[hs, :]
        vt_ref[hh, LANES:, :] = jnp.ones((vt_ref.shape[1] - LANES, s_len), BF16)

        def query_rows(qi, halves):
            rows = []
            for u in halves:
                q = q_ref[0, qi * t + u * th:qi * t + (u + 1) * th, hs]
                zero = jnp.zeros_like(q)
                rows += [jnp.where(lane < HEAD_DIM, q, zero), jnp.where(lane >= HEAD_DIM, q, zero)]
            return jnp.concatenate(rows, axis=0)

        def scores(qi, j):
            if j < qi:
                return lax.dot_general(k_ref[0, j * t:(j + 1) * t, hs], query_rows(qi, (0, 1)), NT_DIMS,
                                       preferred_element_type=F32)
            top = lax.dot_general(k_ref[0, qi * t:qi * t + th, hs], query_rows(qi, (0, 1)), NT_DIMS,
                                  preferred_element_type=F32)
            bot = lax.dot_general(k_ref[0, qi * t + th:(qi + 1) * t, hs], query_rows(qi, (1,)), NT_DIMS,
                                  preferred_element_type=F32)
            first = jnp.where(diag_half, top[:, 0:t], NEG_INF)
            second = jnp.concatenate([top[:, t:2 * t], jnp.where(diag_half, bot, NEG_INF)], axis=0)
            return first, second

        def update(s, k_lo, k_hi, m, acc):
            m_new = jnp.maximum(m, jnp.max(s, axis=0, keepdims=True))
            p = jnp.exp2(s - m_new)
            acc = jnp.exp2(m - m_new) * acc + jnp.dot(vt_ref[hh, :, k_lo:k_hi], p.astype(BF16),
                                                      preferred_element_type=F32)
            return m_new, acc

        pairs = [(qi, j) for qi in range(s_len // t) for j in range(qi + 1)]
        s_next = scores(*pairs[0])
        yield
        for n, (qi, j) in enumerate(pairs):
            s = s_next
            if n + 1 < len(pairs):
                s_next = scores(*pairs[n + 1])
                yield
            if j == 0:
                m = jnp.full((1, 2 * t), NEG_INF, F32)
                acc = jnp.zeros((vt_ref.shape[1], 2 * t), F32)
            if j < qi:
                m, acc = update(s, j * t, (j + 1) * t, m, acc)
            else:
                _, acc0 = update(s[0], qi * t, qi * t + th, m[:, 0:t], acc[:, 0:t])
                _, acc1 = update(s[1], qi * t, (qi + 1) * t, m[:, t:2 * t], acc[:, t:2 * t])
                halves = []
                for a in (acc0, acc1):
                    a = a[0:LANES] / a[LANES:LANES + 1]
                    halves.append(a[:, 0:th] - lam * a[:, th:t])
                o = jnp.concatenate(halves, axis=1)
                o = o * lax.rsqrt(jnp.mean(o * o, axis=0, keepdims=True) + EPS)
                o_ref[0, qi * t:(qi + 1) * t, hs] = (o.T * sg * (1.0 - lam0)).astype(BF16)
            yield

    _alternate([sweep(hh) for hh in range(q_ref.shape[2] // LANES)])


def _attn_a_prompt(lp, sg, q, k, v, lam0):
    b, s, d = q.shape
    width = ATTN_HEADS_PER_STEP * LANES
    blk = pl.BlockSpec((1, s, width), lambda bi, hi: (bi, 0, hi))
    vt_blk = pl.BlockSpec((width, s), lambda bi, hi: (hi, bi))
    return pl.pallas_call(
        functools.partial(_attn_a_prompt_kernel, lam0=lam0),
        grid=(b, d // width),
        in_specs=[_resident(lp.shape), _resident(sg.shape), blk, blk, vt_blk],
        out_specs=blk,
        out_shape=jax.ShapeDtypeStruct((b, s, d), BF16),
        scratch_shapes=[pltpu.VMEM((ATTN_HEADS_PER_STEP, LANES + ONES_ROWS, s), BF16)],
        compiler_params=_params("parallel", "parallel"),
        name="attn_a_prompt",
    )(lp, sg, q, k, v)


def _attn_a_sample_kernel(lp_ref, sg_ref, q_ref, kn_ref, vn_ref, ck_ref, cv_ref, o_ref, *, lam0):
    nt, d = q_ref.shape[1], q_ref.shape[2]
    heads = d // LANES
    past = ck_ref.shape[1] // heads
    lam = _diff_lambda(lp_ref[...], lam0)
    sg = sg_ref[...]
    lane = lax.broadcasted_iota(jnp.int32, (nt, LANES), 1)

    def scores(h):
        sl = slice(h * LANES, (h + 1) * LANES)
        q = q_ref[0, :, sl]
        zero = jnp.zeros_like(q)
        qz = jnp.concatenate([jnp.where(lane < HEAD_DIM, q, zero),
                              jnp.where(lane >= HEAD_DIM, q, zero)], axis=0)
        kc = ck_ref[0, pl.ds(h, past, stride=heads), :].astype(BF16)
        return (lax.dot_general(qz, kc, NT_DIMS, preferred_element_type=F32),
                lax.dot_general(qz, kn_ref[0, :, sl], NT_DIMS, preferred_element_type=F32))

    def weights(sc, sn):
        m = jnp.maximum(jnp.max(sc, axis=-1, keepdims=True), jnp.max(sn, axis=-1, keepdims=True))
        ec = jnp.exp2(sc - m)
        en = jnp.exp2(sn - m)
        inv = 1.0 / (jnp.sum(ec, axis=-1, keepdims=True) + jnp.sum(en, axis=-1, keepdims=True))
        pc = ec * inv
        pn = en * inv
        return ((pc[0:nt] - lam * pc[nt:2 * nt]).astype(BF16),
                (pn[0:nt] - lam * pn[nt:2 * nt]).astype(BF16))

    def output(h, pc, pn):
        sl = slice(h * LANES, (h + 1) * LANES)
        vc = cv_ref[0, pl.ds(h, past, stride=heads), :].astype(BF16)
        o = (jnp.dot(pc, vc, preferred_element_type=F32)
             + jnp.dot(pn, vn_ref[0, :, sl], preferred_element_type=F32))
        o_ref[0, :, sl] = _head_norm(o, sg, lam0).astype(BF16)

    s_next = scores(0)
    p_prev = None
    for h in range(heads):
        s = s_next
        if h + 1 < heads:
            s_next = scores(h + 1)
        p = weights(*s)
        if p_prev is not None:
            output(h - 1, *p_prev)
        p_prev = p
        yield
    output(heads - 1, *p_prev)


def _attn_a_sample(lp, sg, q, kn, vn, ck, cv, lam0):
    b, nt, d = q.shape
    new = pl.BlockSpec((1, nt, d), lambda bi: (bi, 0, 0))
    old = pl.BlockSpec((1,) + ck.shape[1:], lambda bi: (bi, 0, 0))
    return _Call(functools.partial(_attn_a_sample_kernel, lam0=lam0), (b,),
                 [_resident(lp.shape), _resident(sg.shape), new, new, new, old, old],
                 [lp, sg, q, kn, vn, ck, cv], [new], [jax.ShapeDtypeStruct((b, nt, d), BF16)],
                 "attn_a_sample")


def _post_kernel(*refs, final):
    if final:
        x_ref, o_ref, wo_ref, g_ref, w1_ref, w2_ref, gf_ref, out_ref = refs
    else:
        x_ref, o_ref, wo_ref, g_ref, w1_ref, w2_ref, out_ref = refs
    h = x_ref[...] + jnp.dot(o_ref[...], wo_ref[...], preferred_element_type=F32)
    hn = _rms(h, g_ref[...]).astype(BF16)
    yield
    d_ff = w1_ref.shape[1]
    mlp = None
    for c in range(d_ff // FF_CHUNK):
        sl = slice(c * FF_CHUNK, (c + 1) * FF_CHUNK)
        a = jnp.dot(hn, w1_ref[:, sl], preferred_element_type=F32)
        u = jnp.square(jnp.maximum(a, 0.0)).astype(BF16)
        yield
        part = jnp.dot(u, w2_ref[sl, :], preferred_element_type=F32)
        mlp = part if mlp is None else mlp + part
        yield
    h = h + mlp
    if final:
        h = _rms(h, gf_ref[...])
    out_ref[...] = h


def _post(x, o, wo, g, w1, w2, layer, gf=None):
    t, d = x.shape
    tm = TOKEN_TILE
    final = gf is not None
    row = pl.BlockSpec((tm, d), lambda i: (i, 0))
    in_specs = [row, row, _resident(wo.shape), _resident(g.shape),
                _resident_layer(w1.shape, layer), _resident_layer(w2.shape, layer)]
    args = [x, o, wo, g, w1, w2]
    if final:
        in_specs.append(_resident(gf.shape))
        args.append(gf)
    return _Call(functools.partial(_post_kernel, final=final), (t // tm,), in_specs, args,
                 [row], [jax.ShapeDtypeStruct((t, d), F32)], "post_final" if final else "post")


def _proj_b_kernel(x_ref, gkv_ref, wkv_ref, gq_ref, wq_ref,
                   q_ref, kb_ref, vb_ref, k_ref, v_ref, *, period, v_transposed):
    d = x_ref.shape[1]
    x = x_ref[...]
    xs = x * lax.rsqrt(jnp.mean(x * x, axis=-1, keepdims=True) + EPS)
    xkv = (xs * gkv_ref[...]).astype(BF16)
    xq = (xs * gq_ref[...]).astype(BF16)
    q_ref[...] = (jnp.dot(xq, wq_ref[...], preferred_element_type=F32) * QSCALE).astype(BF16)
    k = jnp.dot(xkv, wkv_ref[:, 0:d], preferred_element_type=F32)
    v = jnp.dot(xkv, wkv_ref[:, d:2 * d], preferred_element_type=F32)
    kb_ref[...] = k.astype(BF16)
    vb_ref[...] = _bf16_values(v, v_transposed)

    @pl.when(pl.program_id(0) % period == period - 1)
    def _():
        k_ref[...] = k
        v_ref[...] = v


def _proj_b(x, gkv, wkv, gq, wq, period, v_transposed):
    t, d = x.shape
    tm = TOKEN_TILE
    row = pl.BlockSpec((tm, d), lambda i: (i, 0))
    keep = pl.BlockSpec((tm, d), lambda i: (i // period, 0))
    vb_spec, vb_shape = _values_out(t, d, tm, v_transposed)
    return pl.pallas_call(
        functools.partial(_proj_b_kernel, period=period, v_transposed=v_transposed),
        grid=(t // tm,),
        in_specs=[row, _resident(gkv.shape), _resident(wkv.shape), _resident(gq.shape), _resident(wq.shape)],
        out_specs=[row, row, vb_spec, keep, keep],
        out_shape=[jax.ShapeDtypeStruct((t, d), BF16), jax.ShapeDtypeStruct((t, d), BF16),
                   vb_shape,
                   jax.ShapeDtypeStruct((t // period, d), F32), jax.ShapeDtypeStruct((t // period, d), F32)],
        compiler_params=_params("arbitrary"),
        name="proj_b",
    )(x, gkv, wkv, gq, wq)


def _toeplitz_bias(r_row, rows):
    rb = jnp.broadcast_to(r_row * LOG2E, (rows, r_row.shape[1]))
    return pltpu.roll(rb, 0, 1, stride=1, stride_axis=0)


def _band_prompt_kernel(r_ref, q_ref, k_ref, v_ref, o_ref, vt_ref, bias):
    tq = BAND_TILE
    s_len = q_ref.shape[1]
    n_pairs = q_ref.shape[2] // LANES
    win = PAD + tq

    @pl.when(pl.program_id(1) == 0)
    def _():
        key = lax.broadcasted_iota(jnp.int32, (win, tq), 0)
        qry = lax.broadcasted_iota(jnp.int32, (win, tq), 1)
        dist = qry // CHUNK + LEFT_CHUNKS - key // CHUNK
        in_band = (dist >= 0) & (dist <= LEFT_CHUNKS)
        for pp in range(n_pairs):
            for hh in range(2):
                rb = jnp.broadcast_to(r_ref[pp, hh:hh + 1, :] * LOG2E, (win, win + tq))
                shifted = pltpu.roll(rb, 0, 1, stride=1, stride_axis=0)
                bias[pp, :, hh * tq:(hh + 1) * tq] = jnp.where(in_band, shifted[:, win:win + tq], NEG_INF)

    lane = lax.broadcasted_iota(jnp.int32, (tq, LANES), 1)

    def window(qi):
        q0 = qi * tq
        return max(0, PAD - q0), max(0, q0 - PAD), q0 + tq

    def sweep(pp):
        ps = slice(pp * LANES, (pp + 1) * LANES)
        vt_ref[pp, 0:LANES, :] = v_ref[ps, :]
        vt_ref[pp, LANES:, :] = jnp.ones((vt_ref.shape[1] - LANES, s_len), BF16)

        def scores(qi):
            lo, k_lo, k_hi = window(qi)
            q = q_ref[0, qi * tq:(qi + 1) * tq, ps]
            zero = jnp.zeros_like(q)
            qz = jnp.concatenate([jnp.where(lane < HEAD_DIM, q, zero),
                                  jnp.where(lane >= HEAD_DIM, q, zero)], axis=0)
            return lax.dot_general(k_ref[0, k_lo:k_hi, ps], qz, NT_DIMS,
                                   preferred_element_type=F32) + bias[pp, lo:win, :]

        n_tiles = s_len // tq
        s_next = scores(0)
        yield
        for qi in range(n_tiles):
            s = s_next
            if qi + 1 < n_tiles:
                s_next = scores(qi + 1)
                yield
            _, k_lo, k_hi = window(qi)
            m = jnp.max(s, axis=0, keepdims=True)
            e = jnp.exp2(s - m)
            o = jnp.dot(vt_ref[pp, :, k_lo:k_hi], e.astype(BF16), preferred_element_type=F32)
            o = o[0:LANES] / o[LANES:LANES + 1]
            o = jnp.concatenate([o[0:HEAD_DIM, 0:tq], o[HEAD_DIM:LANES, tq:2 * tq]], axis=0)
            o_ref[0, qi * tq:(qi + 1) * tq, ps] = o.T.astype(BF16)
            yield

    _alternate([sweep(pp) for pp in range(n_pairs)])


def _band_prompt(r, q, k, v):
    b, s, d = q.shape
    width = BAND_PAIRS_PER_STEP * LANES
    blk = pl.BlockSpec((1, s, width), lambda pi, bi: (bi, 0, pi))
    vt_blk = pl.BlockSpec((width, s), lambda pi, bi: (pi, bi))
    rblk = pl.BlockSpec((BAND_PAIRS_PER_STEP, 2, r.shape[2]), lambda pi, bi: (pi, 0, 0))
    win = PAD + BAND_TILE
    return pl.pallas_call(
        _band_prompt_kernel,
        grid=(d // width, b),
        in_specs=[rblk, blk, blk, vt_blk],
        out_specs=blk,
        out_shape=jax.ShapeDtypeStruct((b, s, d), BF16),
        scratch_shapes=[pltpu.VMEM((BAND_PAIRS_PER_STEP, LANES + ONES_ROWS, s), BF16),
                        pltpu.VMEM((BAND_PAIRS_PER_STEP, win, 2 * BAND_TILE), F32)],
        compiler_params=_params("arbitrary", "arbitrary"),
        name="band_prompt",
    )(r, q, k, v)


def _band_sample_kernel(r_ref, q_ref, kn_ref, vn_ref, ck_ref, cv_ref, o_ref):
    nt, d = q_ref.shape[1], q_ref.shape[2]
    lb = ck_ref.shape[1]
    off = SAMPLE_BIAS_OFFSET
    lane = lax.broadcasted_iota(jnp.int32, (nt, LANES), 1)
    pairs = d // LANES

    def scores(p):
        sl = slice(p * LANES, (p + 1) * LANES)
        q = q_ref[0, :, sl]
        zero = jnp.zeros_like(q)
        qz = jnp.concatenate([jnp.where(lane < HEAD_DIM, q, zero),
                              jnp.where(lane >= HEAD_DIM, q, zero)], axis=0)
        shifted = jnp.concatenate([_toeplitz_bias(r_ref[p, hh:hh + 1, :], nt)
                                   for hh in range(2)], axis=0)
        sc = (lax.dot_general(qz, ck_ref[0, :, sl].astype(BF16), NT_DIMS, preferred_element_type=F32)
              + shifted[:, off:off + lb])
        sn = (lax.dot_general(qz, kn_ref[0, :, sl], NT_DIMS, preferred_element_type=F32)
              + shifted[:, off + lb:off + lb + nt])
        return sc, sn

    def weights(sc, sn):
        m = jnp.maximum(jnp.max(sc, axis=-1, keepdims=True), jnp.max(sn, axis=-1, keepdims=True))
        ec = jnp.exp2(sc - m)
        en = jnp.exp2(sn - m)
        inv = 1.0 / (jnp.sum(ec, axis=-1, keepdims=True) + jnp.sum(en, axis=-1, keepdims=True))
        return (ec * inv).astype(BF16), (en * inv).astype(BF16)

    def output(p, pc, pn):
        sl = slice(p * LANES, (p + 1) * LANES)
        o = (jnp.dot(pc, cv_ref[0, :, sl].astype(BF16), preferred_element_type=F32)
             + jnp.dot(pn, vn_ref[0, :, sl], preferred_element_type=F32))
        o_ref[0, :, sl] = jnp.where(lane < HEAD_DIM, o[0:nt], o[nt:2 * nt]).astype(BF16)

    s_next = scores(0)
    w_prev = None
    for p in range(pairs):
        s = s_next
        if p + 1 < pairs:
            s_next = scores(p + 1)
        w = weights(*s)
        if w_prev is not None:
            output(p - 1, *w_prev)
        w_prev = w
        yield
    output(pairs - 1, *w_prev)


def _band_sample(r, q, kn, vn, ck, cv):
    b, nt, d = q.shape
    lb = ck.shape[1]
    new = pl.BlockSpec((1, nt, d), lambda bi: (bi, 0, 0))
    old = pl.BlockSpec((1, lb, d), lambda bi: (bi, 0, 0))
    return _Call(_band_sample_kernel, (b,), [_resident(r.shape), new, new, new, old, old],
                 [r, q, kn, vn, ck, cv], [new], [jax.ShapeDtypeStruct((b, nt, d), BF16)], "band_sample")


def _rope_tables(pos):
    half = HEAD_DIM // 2
    inv = 1.0 / (ROPE_THETA ** (jnp.arange(half, dtype=F32) / half))
    ang = pos.astype(F32)[:, None] * inv[None, :]
    cos, sin = jnp.cos(ang), jnp.sin(ang)
    reps = LANES // HEAD_DIM
    return (jnp.tile(jnp.concatenate([cos, cos], axis=-1), (1, reps)),
            jnp.tile(jnp.concatenate([-sin, sin], axis=-1), (1, reps)))


def _bias_rows(table):
    far = PAD + SAMPLE_BIAS_OFFSET - REL_CLIP
    near = table[:, 2 * REL_CLIP:0:-1]
    rows = jnp.concatenate([jnp.broadcast_to(table[:, 2 * REL_CLIP:], (table.shape[0], far)), near], axis=1)
    return rows.reshape(table.shape[0] // 2, 2, rows.shape[1])


def _bias_rows_t(table):
    heads = table.shape[0]
    lo = BAND_TILE - REL_CLIP
    hi = PAD + 2 * BAND_TILE - lo - (2 * REL_CLIP + 1)
    rows = jnp.concatenate([jnp.broadcast_to(table[:, :1], (heads, lo)), table,
                            jnp.broadcast_to(table[:, 2 * REL_CLIP:], (heads, hi))], axis=1)
    return rows.reshape(heads // 2, 2, rows.shape[1])


def _lambda_init(layer):
    return 0.8 - 0.6 * math.exp(-0.3 * layer)


def kernel(x_prompt, x_sample, cache_a_k, cache_a_v, cache_b_k, cache_b_v, g_attn, w_a_qkv, a_lambda, a_subln, w_a_o, g_kv, w_kv, w_b_q, b_rel, w_b_o, g_mlp, w_ff1, w_ff2, g_final):
    nb, seq, d = x_prompt.shape
    db, nt, _ = x_sample.shape
    past = cache_a_k.shape[2]
    lb = cache_b_k.shape[1]
    a_heads = d // (2 * HEAD_DIM)
    b_heads = d // HEAD_DIM
    assert w_a_qkv.shape[0] == 1 and w_b_q.shape[0] == 1, "one differential and one band layer"
    assert (db * nt) % TOKEN_TILE == 0 and TOKEN_TILE % nt == 0 and seq % TOKEN_TILE == 0
    keep = min(PAD, seq)
    assert keep == TOKEN_TILE and lb == PAD

    lam0 = _lambda_init(0)
    w_qkv = w_a_qkv[0].astype(BF16)
    w_ao = w_a_o[0].astype(BF16)
    w_kvb = w_kv.astype(BF16)
    w_bq = w_b_q[0].astype(BF16)
    w_bo = w_b_o[0].astype(BF16)
    w1 = w_ff1.astype(BF16)
    w2 = w_ff2.astype(BF16)
    g_a = g_attn[0][None]
    g_b = g_attn[1][None]
    g_k = g_kv[None]
    g_m0 = g_mlp[0][None]
    g_m1 = g_mlp[1][None]
    g_f = g_final[None]
    lp = a_lambda[0]
    sg = a_subln[0][None]
    r = _bias_rows(b_rel[0])
    r_t = _bias_rows_t(b_rel[0])

    cos_p, sin_p = _rope_tables(jnp.arange(seq))
    cos_s, sin_s = _rope_tables(jnp.arange(past, past + nt))
    reps = TOKEN_TILE // nt
    cos_s, sin_s = jnp.tile(cos_s, (reps, 1)), jnp.tile(sin_s, (reps, 1))

    ck_a = cache_a_k.reshape(db, past * a_heads, 2 * HEAD_DIM)
    cv_a = cache_a_v.reshape(db, past * a_heads, 2 * HEAD_DIM)
    ck_b = cache_b_k.reshape(db, lb, d)
    cv_b = cache_b_v.reshape(db, lb, d)
    xp = x_prompt.reshape(nb * seq, d)
    xs = x_sample.reshape(db * nt, d)
    psh, ssh = (nb, seq, d), (db, nt, d)

    q_p, ak_p, kb_p, av_p, vt_p = _proj_a(xp, g_a, w_qkv, cos_p, sin_p, True)
    q_s, ak_s, kb_s, av_s, vb_s = _proj_a(xs, g_a, w_qkv, cos_s, sin_s, False)
    o_p = _attn_a_prompt(lp, sg, q_p.reshape(psh), kb_p.reshape(psh), vt_p, lam0).reshape(nb * seq, d)
    (h_p,), (o_s,) = _launch_pair(
        _post(xp, o_p, w_ao, g_m0, w1, w2, 0),
        _attn_a_sample(lp, sg, q_s.reshape(ssh), kb_s.reshape(ssh), vb_s.reshape(ssh), ck_a, cv_a, lam0))
    (h_s,) = _launch(_post(xs, o_s.reshape(db * nt, d), w_ao, g_m0, w1, w2, 0))

    qb_p, kbb_p, vtb_p, bk_p, bv_p = _proj_b(h_p, g_k, w_kvb, g_b, w_bq, seq // TOKEN_TILE, True)
    qb_s, kbb_s, vbb_s, bk_s, bv_s = _proj_b(h_s, g_k, w_kvb, g_b, w_bq, 1, False)
    ob_p = _band_prompt(r_t, qb_p.reshape(psh), kbb_p.reshape(psh), vtb_p).reshape(nb * seq, d)
    (y_p,), (ob_s,) = _launch_pair(
        _post(h_p, ob_p, w_bo, g_m1, w1, w2, 1, g_f),
        _band_sample(r, qb_s.reshape(ssh), kbb_s.reshape(ssh), vbb_s.reshape(ssh), ck_b, cv_b))
    (y_s,) = _launch(_post(h_s, ob_s.reshape(db * nt, d), w_bo, g_m1, w1, w2, 1, g_f))

    return (y_p.reshape(nb, seq, d), y_s.reshape(db, nt, d),
            ak_p.reshape(1, nb, seq, a_heads, 2 * HEAD_DIM), av_p.reshape(1, nb, seq, a_heads, 2 * HEAD_DIM),
            bk_p.reshape(nb, keep, b_heads, HEAD_DIM), bv_p.reshape(nb, keep, b_heads, HEAD_DIM),
            ak_s.reshape(1, db, nt, a_heads, 2 * HEAD_DIM), av_s.reshape(1, db, nt, a_heads, 2 * HEAD_DIM),
            bk_s.reshape(db, nt, b_heads, HEAD_DIM), bv_s.reshape(db, nt, b_heads, HEAD_DIM))
```

```python
import functools
import math
from typing import Callable, NamedTuple

import jax
import jax.numpy as jnp
from jax import lax
from jax.experimental import pallas as pl
from jax.experimental.pallas import tpu as pltpu

HEAD_DIM = 64
CHUNK = 64
LEFT_CHUNKS = 8
PAD = LEFT_CHUNKS * CHUNK
REL_CLIP = 128
ROPE_THETA = 10000.0
EPS = 1e-6
NEG_INF = -1e30
LANES = 128
LOG2E = math.log2(math.e)
QSCALE = HEAD_DIM ** -0.5 * LOG2E

TOKEN_TILE = 512
FF_CHUNK = 1024
ATTN_TILE = 512
ATTN_HEADS_PER_STEP = 2
BAND_PAIRS_PER_STEP = 4
ONES_ROWS = 16
BAND_TILE = 128
SAMPLE_BIAS_OFFSET = 128
VMEM_MIB = {"proj_a": 40, "proj_b": 36, "attn_a_prompt": 36, "band_prompt": 40, "single": 44, "paired": 56}

F32 = jnp.float32
BF16 = jnp.bfloat16
NT_DIMS = (((1,), (1,)), ((), ()))


def _rms(x, g):
    ms = jnp.mean(x * x, axis=-1, keepdims=True)
    return x * lax.rsqrt(ms + EPS) * g


def _resident(shape):
    return pl.BlockSpec(shape, lambda *_: (0,) * len(shape), pipeline_mode=pl.Buffered(1))


def _resident_layer(stacked_shape, layer):
    rest = tuple(stacked_shape[1:])
    return pl.BlockSpec((None,) + rest, lambda *_: (layer,) + (0,) * len(rest), pipeline_mode=pl.Buffered(1))


def _params(launch, *sem):
    return pltpu.CompilerParams(dimension_semantics=sem, vmem_limit_bytes=VMEM_MIB[launch] * 1024 * 1024)


class _Call(NamedTuple):
    body: Callable
    grid: tuple
    in_specs: list
    args: list
    out_specs: list
    out_shape: list
    name: str


def _run(stages):
    for _ in stages:
        pass


def _alternate(bodies):
    live = list(bodies)
    while live:
        for g in list(live):
            if next(g, StopIteration) is StopIteration:
                live.remove(g)


def _launch(call):
    outs = pl.pallas_call(lambda *refs: _run(call.body(*refs)), grid=call.grid, in_specs=call.in_specs,
                          out_specs=call.out_specs, out_shape=call.out_shape,
                          compiler_params=_params("single", *["parallel"] * len(call.grid)),
                          name=call.name)(*call.args)
    return outs


def _launch_pair(heavy, light):
    if heavy.grid != light.grid:
        return _launch(heavy), _launch(light)
    n_in_h, n_in_l, n_out_h = len(heavy.args), len(light.args), len(heavy.out_shape)

    def body(*refs):
        ins, outs = refs[:n_in_h + n_in_l], refs[n_in_h + n_in_l:]
        _alternate([light.body(*ins[n_in_h:], *outs[n_out_h:]), heavy.body(*ins[:n_in_h], *outs[:n_out_h])])

    outs = pl.pallas_call(body, grid=heavy.grid, in_specs=heavy.in_specs + light.in_specs,
                          out_specs=heavy.out_specs + light.out_specs,
                          out_shape=heavy.out_shape + light.out_shape,
                          compiler_params=_params("paired", *["parallel"] * len(heavy.grid)),
                          name=heavy.name + "_" + light.name)(*heavy.args, *light.args)
    return outs[:n_out_h], outs[n_out_h:]


def _diff_lambda(lp, lam0):
    a = jnp.sum(lp[0:1] * lp[1:2], axis=-1, keepdims=True)
    b = jnp.sum(lp[2:3] * lp[3:4], axis=-1, keepdims=True)
    return jnp.exp(a) - jnp.exp(b) + lam0


def _head_norm(o, g, lam0):
    return _rms(o, g) * (1.0 - lam0)


def _bf16_values(v, transposed):
    return (v.T if transposed else v).astype(BF16)


def _proj_a_kernel(x_ref, g_ref, w_ref, cos_ref, sin_ref,
                   q_ref, k_ref, kb_ref, v_ref, vb_ref, *, v_transposed):
    d = x_ref.shape[1]
    xn = _rms(x_ref[...], g_ref[...]).astype(BF16)
    cos = cos_ref[...]
    sin = sin_ref[...]
    lane = lax.broadcasted_iota(jnp.int32, cos.shape, 1)
    first_half = (lane % HEAD_DIM) < (HEAD_DIM // 2)

    def rope(a):
        partner = jnp.where(first_half, pltpu.roll(a, LANES - HEAD_DIM // 2, 1),
                            pltpu.roll(a, HEAD_DIM // 2, 1))
        return a * cos + partner * sin

    q = jnp.dot(xn, w_ref[:, 0:d], preferred_element_type=F32)
    for c in range(d // LANES):
        sl = slice(c * LANES, (c + 1) * LANES)
        q_ref[:, sl] = (rope(q[:, sl]) * QSCALE).astype(BF16)
    tm, heads = x_ref.shape[0], d // LANES
    k = jnp.dot(xn, w_ref[:, d:2 * d], preferred_element_type=F32)
    for c in range(heads):
        sl = slice(c * LANES, (c + 1) * LANES)
        kr = rope(k[:, sl])
        k_ref[pl.ds(c, tm, stride=heads), :] = kr
        kb_ref[:, sl] = kr.astype(BF16)
    v = jnp.dot(xn, w_ref[:, 2 * d:3 * d], preferred_element_type=F32)
    for c in range(heads):
        v_ref[pl.ds(c, tm, stride=heads), :] = v[:, c * LANES:(c + 1) * LANES]
    vb_ref[...] = _bf16_values(v, v_transposed)


def _values_out(t, d, tm, transposed):
    if transposed:
        return pl.BlockSpec((d, tm), lambda i: (0, i)), jax.ShapeDtypeStruct((d, t), BF16)
    return pl.BlockSpec((tm, d), lambda i: (i, 0)), jax.ShapeDtypeStruct((t, d), BF16)


def _proj_a(x, g, w, cos, sin, v_transposed):
    t, d = x.shape
    tm = TOKEN_TILE
    period = cos.shape[0] // tm
    heads = d // LANES
    row = pl.BlockSpec((tm, d), lambda i: (i, 0))
    native = pl.BlockSpec((tm * heads, LANES), lambda i: (i, 0))
    tab = pl.BlockSpec((tm, LANES), lambda i: (i % period, 0))
    vb_spec, vb_shape = _values_out(t, d, tm, v_transposed)
    return pl.pallas_call(
        functools.partial(_proj_a_kernel, v_transposed=v_transposed),
        grid=(t // tm,),
        in_specs=[row, _resident((1, d)), _resident((d, 3 * d)), tab, tab],
        out_specs=[row, native, row, native, vb_spec],
        out_shape=[jax.ShapeDtypeStruct((t, d), BF16), jax.ShapeDtypeStruct((t * heads, LANES), F32),
                   jax.ShapeDtypeStruct((t, d), BF16), jax.ShapeDtypeStruct((t * heads, LANES), F32),
                   vb_shape],
        compiler_params=_params("proj_a", "parallel"),
        name="proj_a",
    )(x, g, w, cos, sin)


def _attn_a_prompt_kernel(lp_ref, sg_ref, q_ref, k_ref, v_ref, o_ref, vt_ref, *, lam0):
    t = ATTN_TILE
    s_len = q_ref.shape[1]
    lam = _diff_lambda(lp_ref[...], lam0)
    sg = sg_ref[...]

    th = t // 2
    lane = lax.broadcasted_iota(jnp.int32, (th, LANES), 1)
    key_chunk = lax.broadcasted_iota(jnp.int32, (th, t), 0) // CHUNK
    qry_chunk = (lax.broadcasted_iota(jnp.int32, (th, t), 1) % th) // CHUNK
    diag_half = qry_chunk >= key_chunk

    def sweep(hh):
        hs = slice(hh * LANES, (hh + 1) * LANES)
        vt_ref[hh, 0:LANES, :] = v_ref[hs, :]
        vt_ref[hh, LANES:, :] = jnp.ones((vt_ref.shape[1] - LANES, s_len), BF16)

        def query_rows(qi, halves):
            rows = []
            for u in halves:
                q = q_ref[0, qi * t + u * th:qi * t + (u + 1) * th, hs]
                zero = jnp.zeros_like(q)
                rows += [jnp.where(lane < HEAD_DIM, q, zero), jnp.where(lane >= HEAD_DIM, q, zero)]
            return jnp.concatenate(rows, axis=0)

        def scores(qi, j):
            if j < qi:
                return lax.dot_general(k_ref[0, j * t:(j + 1) * t, hs], query_rows(qi, (0, 1)), NT_DIMS,
                                       preferred_element_type=F32)
            top = lax.dot_general(k_ref[0, qi * t:qi * t + th, hs], query_rows(qi, (0, 1)), NT_DIMS,
                                  preferred_element_type=F32)
            bot = lax.dot_general(k_ref[0, qi * t + th:(qi + 1) * t, hs], query_rows(qi, (1,)), NT_DIMS,
                                  preferred_element_type=F32)
            first = jnp.where(diag_half, top[:, 0:t], NEG_INF)
            second = jnp.concatenate([top[:, t:2 * t], jnp.where(diag_half, bot, NEG_INF)], axis=0)
            return first, second

        def update(s, k_lo, k_hi, m, acc):
            m_new = jnp.maximum(m, jnp.max(s, axis=0, keepdims=True))
            p = jnp.exp2(s - m_new)
            acc = jnp.exp2(m - m_new) * acc + jnp.dot(vt_ref[hh, :, k_lo:k_hi], p.astype(BF16),
                                                      preferred_element_type=F32)
            return m_new, acc

        pairs = [(qi, j) for qi in range(s_len // t) for j in range(qi + 1)]
        s_next = scores(*pairs[0])
        yield
        for n, (qi, j) in enumerate(pairs):
            s = s_next
            if n + 1 < len(pairs):
                s_next = scores(*pairs[n + 1])
                yield
            if j == 0:
                m = jnp.full((1, 2 * t), NEG_INF, F32)
                acc = jnp.zeros((vt_ref.shape[1], 2 * t), F32)
            if j < qi:
                m, acc = update(s, j * t, (j + 1) * t, m, acc)
            else:
                _, acc0 = update(s[0], qi * t, qi * t + th, m[:, 0:t], acc[:, 0:t])
                _, acc1 = update(s[1], qi * t, (qi + 1) * t, m[:, t:2 * t], acc[:, t:2 * t])
                halves = []
                for a in (acc0, acc1):
                    a = a[0:LANES] / a[LANES:LANES + 1]
                    halves.append(a[:, 0:th] - lam * a[:, th:t])
                o = jnp.concatenate(halves, axis=1)
                o = o * lax.rsqrt(jnp.mean(o * o, axis=0, keepdims=True) + EPS)
                o_ref[0, qi * t:(qi + 1) * t, hs] = (o.T * sg * (1.0 - lam0)).astype(BF16)
            yield

    _alternate([sweep(hh) for hh in range(q_ref.shape[2] // LANES)])


def _attn_a_prompt(lp, sg, q, k, v, lam0):
    b, s, d = q.shape
    width = ATTN_HEADS_PER_STEP * LANES
    blk = pl.BlockSpec((1, s, width), lambda bi, hi: (bi, 0, hi))
    vt_blk = pl.BlockSpec((width, s), lambda bi, hi: (hi, bi))
    return pl.pallas_call(
        functools.partial(_attn_a_prompt_kernel, lam0=lam0),
        grid=(b, d // width),
        in_specs=[_resident(lp.shape), _resident(sg.shape), blk, blk, vt_blk],
        out_specs=blk,
        out_shape=jax.ShapeDtypeStruct((b, s, d), BF16),
        scratch_shapes=[pltpu.VMEM((ATTN_HEADS_PER_STEP, LANES + ONES_ROWS, s), BF16)],
        compiler_params=_params("attn_a_prompt", "parallel", "parallel"),
        name="attn_a_prompt",
    )(lp, sg, q, k, v)


def _attn_a_sample_kernel(lp_ref, sg_ref, q_ref, kn_ref, vn_ref, ck_ref, cv_ref, o_ref, *, lam0):
    nt, d = q_ref.shape[1], q_ref.shape[2]
    heads = d // LANES
    past = ck_ref.shape[1] // heads
    lam = _diff_lambda(lp_ref[...], lam0)
    sg = sg_ref[...]
    lane = lax.broadcasted_iota(jnp.int32, (nt, LANES), 1)

    def scores(h):
        sl = slice(h * LANES, (h + 1) * LANES)
        q = q_ref[0, :, sl]
        zero = jnp.zeros_like(q)
        qz = jnp.concatenate([jnp.where(lane < HEAD_DIM, q, zero),
                              jnp.where(lane >= HEAD_DIM, q, zero)], axis=0)
        kc = ck_ref[0, pl.ds(h, past, stride=heads), :].astype(BF16)
        return (lax.dot_general(qz, kc, NT_DIMS, preferred_element_type=F32),
                lax.dot_general(qz, kn_ref[0, :, sl], NT_DIMS, preferred_element_type=F32))

    def weights(sc, sn):
        m = jnp.maximum(jnp.max(sc, axis=-1, keepdims=True), jnp.max(sn, axis=-1, keepdims=True))
        ec = jnp.exp2(sc - m)
        en = jnp.exp2(sn - m)
        inv = 1.0 / (jnp.sum(ec, axis=-1, keepdims=True) + jnp.sum(en, axis=-1, keepdims=True))
        pc = ec * inv
        pn = en * inv
        return ((pc[0:nt] - lam * pc[nt:2 * nt]).astype(BF16),
                (pn[0:nt] - lam * pn[nt:2 * nt]).astype(BF16))

    def output(h, pc, pn):
        sl = slice(h * LANES, (h + 1) * LANES)
        vc = cv_ref[0, pl.ds(h, past, stride=heads), :].astype(BF16)
        o = (jnp.dot(pc, vc, preferred_element_type=F32)
             + jnp.dot(pn, vn_ref[0, :, sl], preferred_element_type=F32))
        o_ref[0, :, sl] = _head_norm(o, sg, lam0).astype(BF16)

    s_next = scores(0)
    p_prev = None
    for h in range(heads):
        s = s_next
        if h + 1 < heads:
            s_next = scores(h + 1)
        p = weights(*s)
        if p_prev is not None:
            output(h - 1, *p_prev)
        p_prev = p
        yield
    output(heads - 1, *p_prev)


def _attn_a_sample(lp, sg, q, kn, vn, ck, cv, lam0):
    b, nt, d = q.shape
    new = pl.BlockSpec((1, nt, d), lambda bi: (bi, 0, 0))
    old = pl.BlockSpec((1,) + ck.shape[1:], lambda bi: (bi, 0, 0))
    return _Call(functools.partial(_attn_a_sample_kernel, lam0=lam0), (b,),
                 [_resident(lp.shape), _resident(sg.shape), new, new, new, old, old],
                 [lp, sg, q, kn, vn, ck, cv], [new], [jax.ShapeDtypeStruct((b, nt, d), BF16)],
                 "attn_a_sample")


def _post_kernel(*refs, final):
    if final:
        x_ref, o_ref, wo_ref, g_ref, w1_ref, w2_ref, gf_ref, out_ref = refs
    else:
        x_ref, o_ref, wo_ref, g_ref, w1_ref, w2_ref, out_ref = refs
    h = x_ref[...] + jnp.dot(o_ref[...], wo_ref[...], preferred_element_type=F32)
    hn = _rms(h, g_ref[...]).astype(BF16)
    yield
    d_ff = w1_ref.shape[1]
    mlp = None
    for c in range(d_ff // FF_CHUNK):
        sl = slice(c * FF_CHUNK, (c + 1) * FF_CHUNK)
        a = jnp.dot(hn, w1_ref[:, sl], preferred_element_type=F32)
        u = jnp.square(jnp.maximum(a, 0.0)).astype(BF16)
        yield
        part = jnp.dot(u, w2_ref[sl, :], preferred_element_type=F32)
        mlp = part if mlp is None else mlp + part
        yield
    h = h + mlp
    if final:
        h = _rms(h, gf_ref[...])
    out_ref[...] = h


def _post(x, o, wo, g, w1, w2, layer, gf=None):
    t, d = x.shape
    tm = TOKEN_TILE
    final = gf is not None
    row = pl.BlockSpec((tm, d), lambda i: (i, 0))
    in_specs = [row, row, _resident(wo.shape), _resident(g.shape),
                _resident_layer(w1.shape, layer), _resident_layer(w2.shape, layer)]
    args = [x, o, wo, g, w1, w2]
    if final:
        in_specs.append(_resident(gf.shape))
        args.append(gf)
    return _Call(functools.partial(_post_kernel, final=final), (t // tm,), in_specs, args,
                 [row], [jax.ShapeDtypeStruct((t, d), F32)], "post_final" if final else "post")


def _proj_b_kernel(x_ref, gkv_ref, wkv_ref, gq_ref, wq_ref,
                   q_ref, kb_ref, vb_ref, k_ref, v_ref, *, period, v_transposed):
    d = x_ref.shape[1]
    x = x_ref[...]
    xs = x * lax.rsqrt(jnp.mean(x * x, axis=-1, keepdims=True) + EPS)
    xkv = (xs * gkv_ref[...]).astype(BF16)
    xq = (xs * gq_ref[...]).astype(BF16)
    q_ref[...] = (jnp.dot(xq, wq_ref[...], preferred_element_type=F32) * QSCALE).astype(BF16)
    k = jnp.dot(xkv, wkv_ref[:, 0:d], preferred_element_type=F32)
    v = jnp.dot(xkv, wkv_ref[:, d:2 * d], preferred_element_type=F32)
    kb_ref[...] = k.astype(BF16)
    vb_ref[...] = _bf16_values(v, v_transposed)

    @pl.when(pl.program_id(0) % period == period - 1)
    def _():
        k_ref[...] = k
        v_ref[...] = v


def _proj_b(x, gkv, wkv, gq, wq, period, v_transposed):
    t, d = x.shape
    tm = TOKEN_TILE
    row = pl.BlockSpec((tm, d), lambda i: (i, 0))
    keep = pl.BlockSpec((tm, d), lambda i: (i // period, 0))
    vb_spec, vb_shape = _values_out(t, d, tm, v_transposed)
    return pl.pallas_call(
        functools.partial(_proj_b_kernel, period=period, v_transposed=v_transposed),
        grid=(t // tm,),
        in_specs=[row, _resident(gkv.shape), _resident(wkv.shape), _resident(gq.shape), _resident(wq.shape)],
        out_specs=[row, row, vb_spec, keep, keep],
        out_shape=[jax.ShapeDtypeStruct((t, d), BF16), jax.ShapeDtypeStruct((t, d), BF16),
                   vb_shape,
                   jax.ShapeDtypeStruct((t // period, d), F32), jax.ShapeDtypeStruct((t // period, d), F32)],
        compiler_params=_params("proj_b", "arbitrary"),
        name="proj_b",
    )(x, gkv, wkv, gq, wq)


def _toeplitz_bias(r_row, rows):
    rb = jnp.broadcast_to(r_row * LOG2E, (rows, r_row.shape[1]))
    return pltpu.roll(rb, 0, 1, stride=1, stride_axis=0)


def _band_prompt_kernel(r_ref, q_ref, k_ref, v_ref, o_ref, vt_ref, bias):
    tq = BAND_TILE
    s_len = q_ref.shape[1]
    n_pairs = q_ref.shape[2] // LANES
    win = PAD + tq

    @pl.when(pl.program_id(1) == 0)
    def _():
        key = lax.broadcasted_iota(jnp.int32, (win, tq), 0)
        qry = lax.broadcasted_iota(jnp.int32, (win, tq), 1)
        dist = qry // CHUNK + LEFT_CHUNKS - key // CHUNK
        in_band = (dist >= 0) & (dist <= LEFT_CHUNKS)
        for pp in range(n_pairs):
            for hh in range(2):
                rb = jnp.broadcast_to(r_ref[pp, hh:hh + 1, :] * LOG2E, (win, win + tq))
                shifted = pltpu.roll(rb, 0, 1, stride=1, stride_axis=0)
                bias[pp, :, hh * tq:(hh + 1) * tq] = jnp.where(in_band, shifted[:, win:win + tq], NEG_INF)

    lane = lax.broadcasted_iota(jnp.int32, (tq, LANES), 1)

    def window(qi):
        q0 = qi * tq
        return max(0, PAD - q0), max(0, q0 - PAD), q0 + tq

    def sweep(pp):
        ps = slice(pp * LANES, (pp + 1) * LANES)
        vt_ref[pp, 0:LANES, :] = v_ref[ps, :]
        vt_ref[pp, LANES:, :] = jnp.ones((vt_ref.shape[1] - LANES, s_len), BF16)

        def scores(qi):
            lo, k_lo, k_hi = window(qi)
            q = q_ref[0, qi * tq:(qi + 1) * tq, ps]
            zero = jnp.zeros_like(q)
            qz = jnp.concatenate([jnp.where(lane < HEAD_DIM, q, zero),
                                  jnp.where(lane >= HEAD_DIM, q, zero)], axis=0)
            return lax.dot_general(k_ref[0, k_lo:k_hi, ps], qz, NT_DIMS,
                                   preferred_element_type=F32) + bias[pp, lo:win, :]

        n_tiles = s_len // tq
        s_next = scores(0)
        yield
        for qi in range(n_tiles):
            s = s_next
            if qi + 1 < n_tiles:
                s_next = scores(qi + 1)
                yield
            _, k_lo, k_hi = window(qi)
            m = jnp.max(s, axis=0, keepdims=True)
            e = jnp.exp2(s - m)
            o = jnp.dot(vt_ref[pp, :, k_lo:k_hi], e.astype(BF16), preferred_element_type=F32)
            o = o[0:LANES] / o[LANES:LANES + 1]
            o = jnp.concatenate([o[0:HEAD_DIM, 0:tq], o[HEAD_DIM:LANES, tq:2 * tq]], axis=0)
            o_ref[0, qi * tq:(qi + 1) * tq, ps] = o.T.astype(BF16)
            yield

    _alternate([sweep(pp) for pp in range(n_pairs)])


def _band_prompt(r, q, k, v):
    b, s, d = q.shape
    width = BAND_PAIRS_PER_STEP * LANES
    blk = pl.BlockSpec((1, s, width), lambda pi, bi: (bi, 0, pi))
    vt_blk = pl.BlockSpec((width, s), lambda pi, bi: (pi, bi))
    rblk = pl.BlockSpec((BAND_PAIRS_PER_STEP, 2, r.shape[2]), lambda pi, bi: (pi, 0, 0))
    win = PAD + BAND_TILE
    return pl.pallas_call(
        _band_prompt_kernel,
        grid=(d // width, b),
        in_specs=[rblk, blk, blk, vt_blk],
        out_specs=blk,
        out_shape=jax.ShapeDtypeStruct((b, s, d), BF16),
        scratch_shapes=[pltpu.VMEM((BAND_PAIRS_PER_STEP, LANES + ONES_ROWS, s), BF16),
                        pltpu.VMEM((BAND_PAIRS_PER_STEP, win, 2 * BAND_TILE), F32)],
        compiler_params=_params("band_prompt", "arbitrary", "arbitrary"),
        name="band_prompt",
    )(r, q, k, v)


def _band_sample_kernel(r_ref, q_ref, kn_ref, vn_ref, ck_ref, cv_ref, o_ref):
    nt, d = q_ref.shape[1], q_ref.shape[2]
    lb = ck_ref.shape[1]
    off = SAMPLE_BIAS_OFFSET
    lane = lax.broadcasted_iota(jnp.int32, (nt, LANES), 1)
    pairs = d // LANES

    def scores(p):
        sl = slice(p * LANES, (p + 1) * LANES)
        q = q_ref[0, :, sl]
        zero = jnp.zeros_like(q)
        qz = jnp.concatenate([jnp.where(lane < HEAD_DIM, q, zero),
                              jnp.where(lane >= HEAD_DIM, q, zero)], axis=0)
        shifted = jnp.concatenate([_toeplitz_bias(r_ref[p, hh:hh + 1, :], nt)
                                   for hh in range(2)], axis=0)
        sc = (lax.dot_general(qz, ck_ref[0, :, sl].astype(BF16), NT_DIMS, preferred_element_type=F32)
              + shifted[:, off:off + lb])
        sn = (lax.dot_general(qz, kn_ref[0, :, sl], NT_DIMS, preferred_element_type=F32)
              + shifted[:, off + lb:off + lb + nt])
        return sc, sn

    def weights(sc, sn):
        m = jnp.maximum(jnp.max(sc, axis=-1, keepdims=True), jnp.max(sn, axis=-1, keepdims=True))
        ec = jnp.exp2(sc - m)
        en = jnp.exp2(sn - m)
        inv = 1.0 / (jnp.sum(ec, axis=-1, keepdims=True) + jnp.sum(en, axis=-1, keepdims=True))
        return (ec * inv).astype(BF16), (en * inv).astype(BF16)

    def output(p, pc, pn):
        sl = slice(p * LANES, (p + 1) * LANES)
        o = (jnp.dot(pc, cv_ref[0, :, sl].astype(BF16), preferred_element_type=F32)
             + jnp.dot(pn, vn_ref[0, :, sl], preferred_element_type=F32))
        o_ref[0, :, sl] = jnp.where(lane < HEAD_DIM, o[0:nt], o[nt:2 * nt]).astype(BF16)

    s_next = scores(0)
    w_prev = None
    for p in range(pairs):
        s = s_next
        if p + 1 < pairs:
            s_next = scores(p + 1)
        w = weights(*s)
        if w_prev is not None:
            output(p - 1, *w_prev)
        w_prev = w
        yield
    output(pairs - 1, *w_prev)


def _band_sample(r, q, kn, vn, ck, cv):
    b, nt, d = q.shape
    lb = ck.shape[1]
    new = pl.BlockSpec((1, nt, d), lambda bi: (bi, 0, 0))
    old = pl.BlockSpec((1, lb, d), lambda bi: (bi, 0, 0))
    return _Call(_band_sample_kernel, (b,), [_resident(r.shape), new, new, new, old, old],
                 [r, q, kn, vn, ck, cv], [new], [jax.ShapeDtypeStruct((b, nt, d), BF16)], "band_sample")


def _rope_tables(pos):
    half = HEAD_DIM // 2
    inv = 1.0 / (ROPE_THETA ** (jnp.arange(half, dtype=F32) / half))
    ang = pos.astype(F32)[:, None] * inv[None, :]
    cos, sin = jnp.cos(ang), jnp.sin(ang)
    reps = LANES // HEAD_DIM
    return (jnp.tile(jnp.concatenate([cos, cos], axis=-1), (1, reps)),
            jnp.tile(jnp.concatenate([-sin, sin], axis=-1), (1, reps)))


def _bias_rows(table):
    far = PAD + SAMPLE_BIAS_OFFSET - REL_CLIP
    near = table[:, 2 * REL_CLIP:0:-1]
    rows = jnp.concatenate([jnp.broadcast_to(table[:, 2 * REL_CLIP:], (table.shape[0], far)), near], axis=1)
    return rows.reshape(table.shape[0] // 2, 2, rows.shape[1])


def _bias_rows_t(table):
    heads = table.shape[0]
    lo = BAND_TILE - REL_CLIP
    hi = PAD + 2 * BAND_TILE - lo - (2 * REL_CLIP + 1)
    rows = jnp.concatenate([jnp.broadcast_to(table[:, :1], (heads, lo)), table,
                            jnp.broadcast_to(table[:, 2 * REL_CLIP:], (heads, hi))], axis=1)
    return rows.reshape(heads // 2, 2, rows.shape[1])


def _lambda_init(layer):
    return 0.8 - 0.6 * math.exp(-0.3 * layer)


def kernel(x_prompt, x_sample, cache_a_k, cache_a_v, cache_b_k, cache_b_v, g_attn, w_a_qkv, a_lambda, a_subln, w_a_o, g_kv, w_kv, w_b_q, b_rel, w_b_o, g_mlp, w_ff1, w_ff2, g_final):
    nb, seq, d = x_prompt.shape
    db, nt, _ = x_sample.shape
    past = cache_a_k.shape[2]
    lb = cache_b_k.shape[1]
    a_heads = d // (2 * HEAD_DIM)
    b_heads = d // HEAD_DIM
    assert w_a_qkv.shape[0] == 1 and w_b_q.shape[0] == 1, "one differential and one band layer"
    assert (db * nt) % TOKEN_TILE == 0 and TOKEN_TILE % nt == 0 and seq % TOKEN_TILE == 0
    keep = min(PAD, seq)
    assert keep == TOKEN_TILE and lb == PAD

    lam0 = _lambda_init(0)
    w_qkv = w_a_qkv[0].astype(BF16)
    w_ao = w_a_o[0].astype(BF16)
    w_kvb = w_kv.astype(BF16)
    w_bq = w_b_q[0].astype(BF16)
    w_bo = w_b_o[0].astype(BF16)
    w1 = w_ff1.astype(BF16)
    w2 = w_ff2.astype(BF16)
    g_a = g_attn[0][None]
    g_b = g_attn[1][None]
    g_k = g_kv[None]
    g_m0 = g_mlp[0][None]
    g_m1 = g_mlp[1][None]
    g_f = g_final[None]
    lp = a_lambda[0]
    sg = a_subln[0][None]
    r = _bias_rows(b_rel[0])
    r_t = _bias_rows_t(b_rel[0])

    cos_p, sin_p = _rope_tables(jnp.arange(seq))
    cos_s, sin_s = _rope_tables(jnp.arange(past, past + nt))
    reps = TOKEN_TILE // nt
    cos_s, sin_s = jnp.tile(cos_s, (reps, 1)), jnp.tile(sin_s, (reps, 1))

    ck_a = cache_a_k.reshape(db, past * a_heads, 2 * HEAD_DIM)
    cv_a = cache_a_v.reshape(db, past * a_heads, 2 * HEAD_DIM)
    ck_b = cache_b_k.reshape(db, lb, d)
    cv_b = cache_b_v.reshape(db, lb, d)
    xp = x_prompt.reshape(nb * seq, d)
    xs = x_sample.reshape(db * nt, d)
    psh, ssh = (nb, seq, d), (db, nt, d)

    q_p, ak_p, kb_p, av_p, vt_p = _proj_a(xp, g_a, w_qkv, cos_p, sin_p, True)
    q_s, ak_s, kb_s, av_s, vb_s = _proj_a(xs, g_a, w_qkv, cos_s, sin_s, False)
    o_p = _attn_a_prompt(lp, sg, q_p.reshape(psh), kb_p.reshape(psh), vt_p, lam0).reshape(nb * seq, d)
    (h_p,), (o_s,) = _launch_pair(
        _post(xp, o_p, w_ao, g_m0, w1, w2, 0),
        _attn_a_sample(lp, sg, q_s.reshape(ssh), kb_s.reshape(ssh), vb_s.reshape(ssh), ck_a, cv_a, lam0))
    (h_s,) = _launch(_post(xs, o_s.reshape(db * nt, d), w_ao, g_m0, w1, w2, 0))

    qb_p, kbb_p, vtb_p, bk_p, bv_p = _proj_b(h_p, g_k, w_kvb, g_b, w_bq, seq // TOKEN_TILE, True)
    qb_s, kbb_s, vbb_s, bk_s, bv_s = _proj_b(h_s, g_k, w_kvb, g_b, w_bq, 1, False)
    ob_p = _band_prompt(r_t, qb_p.reshape(psh), kbb_p.reshape(psh), vtb_p).reshape(nb * seq, d)
    (y_p,), (ob_s,) = _launch_pair(
        _post(h_p, ob_p, w_bo, g_m1, w1, w2, 1, g_f),
        _band_sample(r, qb_s.reshape(ssh), kbb_s.reshape(ssh), vbb_s.reshape(ssh), ck_b, cv_b))
    (y_s,) = _launch(_post(h_s, ob_s.reshape(db * nt, d), w_bo, g_m1, w1, w2, 1, g_f))

    return (y_p.reshape(nb, seq, d), y_s.reshape(db, nt, d),
            ak_p.reshape(1, nb, seq, a_heads, 2 * HEAD_DIM), av_p.reshape(1, nb, seq, a_heads, 2 * HEAD_DIM),
            bk_p.reshape(nb, keep, b_heads, HEAD_DIM), bv_p.reshape(nb, keep, b_heads, HEAD_DIM),
            ak_s.reshape(1, db, nt, a_heads, 2 * HEAD_DIM), av_s.reshape(1, db, nt, a_heads, 2 * HEAD_DIM),
            bk_s.reshape(db, nt, b_heads, HEAD_DIM), bv_s.reshape(db, nt, b_heads, HEAD_DIM))
```

```python
import functools
import math
from typing import Callable, NamedTuple

import jax
import jax.numpy as jnp
from jax import lax
from jax.experimental import pallas as pl
from jax.experimental.pallas import tpu as pltpu

HEAD_DIM = 64
CHUNK = 64
LEFT_CHUNKS = 8
PAD = LEFT_CHUNKS * CHUNK
REL_CLIP = 128
ROPE_THETA = 10000.0
EPS = 1e-6
NEG_INF = -1e30
LANES = 128
LOG2E = math.log2(math.e)
QSCALE = HEAD_DIM ** -0.5 * LOG2E

TOKEN_TILE = 512
FF_CHUNK = 1024
ATTN_TILE = 512
ATTN_HEADS_PER_STEP = 2
BAND_PAIRS_PER_STEP = 4
ONES_ROWS = 16
BAND_TILE = 128
SAMPLE_BIAS_OFFSET = 128
VMEM_MIB = {"proj_a": 36, "proj_b": 32, "attn_a_prompt": 30, "band_prompt": 34, "single": 40, "paired": 54}

F32 = jnp.float32
BF16 = jnp.bfloat16
NT_DIMS = (((1,), (1,)), ((), ()))


def _rms(x, g):
    ms = jnp.mean(x * x, axis=-1, keepdims=True)
    return x * lax.rsqrt(ms + EPS) * g


def _resident(shape):
    return pl.BlockSpec(shape, lambda *_: (0,) * len(shape), pipeline_mode=pl.Buffered(1))


def _resident_layer(stacked_shape, layer):
    rest = tuple(stacked_shape[1:])
    return pl.BlockSpec((None,) + rest, lambda *_: (layer,) + (0,) * len(rest), pipeline_mode=pl.Buffered(1))


def _params(launch, *sem):
    return pltpu.CompilerParams(dimension_semantics=sem, vmem_limit_bytes=VMEM_MIB[launch] * 1024 * 1024)


class _Call(NamedTuple):
    body: Callable
    grid: tuple
    in_specs: list
    args: list
    out_specs: list
    out_shape: list
    name: str


def _run(stages):
    for _ in stages:
        pass


def _alternate(bodies):
    live = list(bodies)
    while live:
        for g in list(live):
            if next(g, StopIteration) is StopIteration:
                live.remove(g)


def _launch(call):
    outs = pl.pallas_call(lambda *refs: _run(call.body(*refs)), grid=call.grid, in_specs=call.in_specs,
                          out_specs=call.out_specs, out_shape=call.out_shape,
                          compiler_params=_params("single", *["parallel"] * len(call.grid)),
                          name=call.name)(*call.args)
    return outs


def _launch_pair(heavy, light):
    if heavy.grid != light.grid:
        return _launch(heavy), _launch(light)
    n_in_h, n_in_l, n_out_h = len(heavy.args), len(light.args), len(heavy.out_shape)

    def body(*refs):
        ins, outs = refs[:n_in_h + n_in_l], refs[n_in_h + n_in_l:]
        _alternate([light.body(*ins[n_in_h:], *outs[n_out_h:]), heavy.body(*ins[:n_in_h], *outs[:n_out_h])])

    outs = pl.pallas_call(body, grid=heavy.grid, in_specs=heavy.in_specs + light.in_specs,
                          out_specs=heavy.out_specs + light.out_specs,
                          out_shape=heavy.out_shape + light.out_shape,
                          compiler_params=_params("paired", *["parallel"] * len(heavy.grid)),
                          name=heavy.name + "_" + light.name)(*heavy.args, *light.args)
    return outs[:n_out_h], outs[n_out_h:]


def _diff_lambda(lp, lam0):
    a = jnp.sum(lp[0:1] * lp[1:2], axis=-1, keepdims=True)
    b = jnp.sum(lp[2:3] * lp[3:4], axis=-1, keepdims=True)
    return jnp.exp(a) - jnp.exp(b) + lam0


def _head_norm(o, g, lam0):
    return _rms(o, g) * (1.0 - lam0)


def _bf16_values(v, transposed):
    return (v.T if transposed else v).astype(BF16)


def _proj_a_kernel(x_ref, g_ref, w_ref, cos_ref, sin_ref,
                   q_ref, k_ref, kb_ref, v_ref, vb_ref, *, v_transposed):
    d = x_ref.shape[1]
    xn = _rms(x_ref[...], g_ref[...]).astype(BF16)
    cos = cos_ref[...]
    sin = sin_ref[...]
    lane = lax.broadcasted_iota(jnp.int32, cos.shape, 1)
    first_half = (lane % HEAD_DIM) < (HEAD_DIM // 2)

    def rope(a):
        partner = jnp.where(first_half, pltpu.roll(a, LANES - HEAD_DIM // 2, 1),
                            pltpu.roll(a, HEAD_DIM // 2, 1))
        return a * cos + partner * sin

    q = jnp.dot(xn, w_ref[:, 0:d], preferred_element_type=F32)
    for c in range(d // LANES):
        sl = slice(c * LANES, (c + 1) * LANES)
        q_ref[:, sl] = (rope(q[:, sl]) * QSCALE).astype(BF16)
    tm, heads = x_ref.shape[0], d // LANES
    k = jnp.dot(xn, w_ref[:, d:2 * d], preferred_element_type=F32)
    for c in range(heads):
        sl = slice(c * LANES, (c + 1) * LANES)
        kr = rope(k[:, sl])
        k_ref[pl.ds(c, tm, stride=heads), :] = kr
        kb_ref[:, sl] = kr.astype(BF16)
    v = jnp.dot(xn, w_ref[:, 2 * d:3 * d], preferred_element_type=F32)
    for c in range(heads):
        v_ref[pl.ds(c, tm, stride=heads), :] = v[:, c * LANES:(c + 1) * LANES]
    vb_ref[...] = _bf16_values(v, v_transposed)


def _values_out(t, d, tm, transposed):
    if transposed:
        return pl.BlockSpec((d, tm), lambda i: (0, i)), jax.ShapeDtypeStruct((d, t), BF16)
    return pl.BlockSpec((tm, d), lambda i: (i, 0)), jax.ShapeDtypeStruct((t, d), BF16)


def _proj_a(x, g, w, cos, sin, v_transposed):
    t, d = x.shape
    tm = TOKEN_TILE
    period = cos.shape[0] // tm
    heads = d // LANES
    row = pl.BlockSpec((tm, d), lambda i: (i, 0))
    native = pl.BlockSpec((tm * heads, LANES), lambda i: (i, 0))
    tab = pl.BlockSpec((tm, LANES), lambda i: (i % period, 0))
    vb_spec, vb_shape = _values_out(t, d, tm, v_transposed)
    return pl.pallas_call(
        functools.partial(_proj_a_kernel, v_transposed=v_transposed),
        grid=(t // tm,),
        in_specs=[row, _resident((1, d)), _resident((d, 3 * d)), tab, tab],
        out_specs=[row, native, row, native, vb_spec],
        out_shape=[jax.ShapeDtypeStruct((t, d), BF16), jax.ShapeDtypeStruct((t * heads, LANES), F32),
                   jax.ShapeDtypeStruct((t, d), BF16), jax.ShapeDtypeStruct((t * heads, LANES), F32),
                   vb_shape],
        compiler_params=_params("proj_a", "parallel"),
        name="proj_a",
    )(x, g, w, cos, sin)


def _attn_a_prompt_kernel(lp_ref, sg_ref, q_ref, k_ref, v_ref, o_ref, vt_ref, *, lam0):
    t = ATTN_TILE
    s_len = q_ref.shape[1]
    lam = _diff_lambda(lp_ref[...], lam0)
    sg = sg_ref[...]

    th = t // 2
    lane = lax.broadcasted_iota(jnp.int32, (th, LANES), 1)
    key_chunk = lax.broadcasted_iota(jnp.int32, (th, t), 0) // CHUNK
    qry_chunk = (lax.broadcasted_iota(jnp.int32, (th, t), 1) % th) // CHUNK
    diag_half = qry_chunk >= key_chunk

    def sweep(hh):
        hs = slice(hh * LANES, (hh + 1) * LANES)
        vt_ref[hh, 0:LANES, :] = v_ref[hs, :]
        vt_ref[hh, LANES:, :] = jnp.ones((vt_ref.shape[1] - LANES, s_len), BF16)

        def query_rows(qi, halves):
            rows = []
            for u in halves:
                q = q_ref[0, qi * t + u * th:qi * t + (u + 1) * th, hs]
                zero = jnp.zeros_like(q)
                rows += [jnp.where(lane < HEAD_DIM, q, zero), jnp.where(lane >= HEAD_DIM, q, zero)]
            return jnp.concatenate(rows, axis=0)

        def scores(qi, j):
            if j < qi:
                return lax.dot_general(k_ref[0, j * t:(j + 1) * t, hs], query_rows(qi, (0, 1)), NT_DIMS,
                                       preferred_element_type=F32)
            top = lax.dot_general(k_ref[0, qi * t:qi * t + th, hs], query_rows(qi, (0, 1)), NT_DIMS,
                                  preferred_element_type=F32)
            bot = lax.dot_general(k_ref[0, qi * t + th:(qi + 1) * t, hs], query_rows(qi, (1,)), NT_DIMS,
                                  preferred_element_type=F32)
            first = jnp.where(diag_half, top[:, 0:t], NEG_INF)
            second = jnp.concatenate([top[:, t:2 * t], jnp.where(diag_half, bot, NEG_INF)], axis=0)
            return first, second

        def update(s, k_lo, k_hi, m, acc):
            m_new = jnp.maximum(m, jnp.max(s, axis=0, keepdims=True))
            p = jnp.exp2(s - m_new)
            acc = jnp.exp2(m - m_new) * acc + jnp.dot(vt_ref[hh, :, k_lo:k_hi], p.astype(BF16),
                                                      preferred_element_type=F32)
            return m_new, acc

        pairs = [(qi, j) for qi in range(s_len // t) for j in range(qi + 1)]
        s_next = scores(*pairs[0])
        yield
        for n, (qi, j) in enumerate(pairs):
            s = s_next
            if n + 1 < len(pairs):
                s_next = scores(*pairs[n + 1])
                yield
            if j == 0:
                m = jnp.full((1, 2 * t), NEG_INF, F32)
                acc = jnp.zeros((vt_ref.shape[1], 2 * t), F32)
            if j < qi:
                m, acc = update(s, j * t, (j + 1) * t, m, acc)
            else:
                _, acc0 = update(s[0], qi * t, qi * t + th, m[:, 0:t], acc[:, 0:t])
                _, acc1 = update(s[1], qi * t, (qi + 1) * t, m[:, t:2 * t], acc[:, t:2 * t])
                halves = []
                for a in (acc0, acc1):
                    a = a[0:LANES] / a[LANES:LANES + 1]
                    halves.append(a[:, 0:th] - lam * a[:, th:t])
                o = jnp.concatenate(halves, axis=1)
                o = o * lax.rsqrt(jnp.mean(o * o, axis=0, keepdims=True) + EPS)
                o_ref[0, qi * t:(qi + 1) * t, hs] = (o.T * sg * (1.0 - lam0)).astype(BF16)
            yield

    _alternate([sweep(hh) for hh in range(q_ref.shape[2] // LANES)])


def _attn_a_prompt(lp, sg, q, k, v, lam0):
    b, s, d = q.shape
    width = ATTN_HEADS_PER_STEP * LANES
    blk = pl.BlockSpec((1, s, width), lambda bi, hi: (bi, 0, hi))
    vt_blk = pl.BlockSpec((width, s), lambda bi, hi: (hi, bi))
    return pl.pallas_call(
        functools.partial(_attn_a_prompt_kernel, lam0=lam0),
        grid=(b, d // width),
        in_specs=[_resident(lp.shape), _resident(sg.shape), blk, blk, vt_blk],
        out_specs=blk,
        out_shape=jax.ShapeDtypeStruct((b, s, d), BF16),
        scratch_shapes=[pltpu.VMEM((ATTN_HEADS_PER_STEP, LANES + ONES_ROWS, s), BF16)],
        compiler_params=_params("attn_a_prompt", "parallel", "parallel"),
        name="attn_a_prompt",
    )(lp, sg, q, k, v)


def _attn_a_sample_kernel(lp_ref, sg_ref, q_ref, kn_ref, vn_ref, ck_ref, cv_ref, o_ref, *, lam0):
    nt, d = q_ref.shape[1], q_ref.shape[2]
    heads = d // LANES
    past = ck_ref.shape[1] // heads
    lam = _diff_lambda(lp_ref[...], lam0)
    sg = sg_ref[...]
    lane = lax.broadcasted_iota(jnp.int32, (nt, LANES), 1)

    def scores(h):
        sl = slice(h * LANES, (h + 1) * LANES)
        q = q_ref[0, :, sl]
        zero = jnp.zeros_like(q)
        qz = jnp.concatenate([jnp.where(lane < HEAD_DIM, q, zero),
                              jnp.where(lane >= HEAD_DIM, q, zero)], axis=0)
        kc = ck_ref[0, pl.ds(h, past, stride=heads), :].astype(BF16)
        return (lax.dot_general(qz, kc, NT_DIMS, preferred_element_type=F32),
                lax.dot_general(qz, kn_ref[0, :, sl], NT_DIMS, preferred_element_type=F32))

    def weights(sc, sn):
        m = jnp.maximum(jnp.max(sc, axis=-1, keepdims=True), jnp.max(sn, axis=-1, keepdims=True))
        ec = jnp.exp2(sc - m)
        en = jnp.exp2(sn - m)
        inv = 1.0 / (jnp.sum(ec, axis=-1, keepdims=True) + jnp.sum(en, axis=-1, keepdims=True))
        pc = ec * inv
        pn = en * inv
        return ((pc[0:nt] - lam * pc[nt:2 * nt]).astype(BF16),
                (pn[0:nt] - lam * pn[nt:2 * nt]).astype(BF16))

    def output(h, pc, pn):
        sl = slice(h * LANES, (h + 1) * LANES)
        vc = cv_ref[0, pl.ds(h, past, stride=heads), :].astype(BF16)
        o = (jnp.dot(pc, vc, preferred_element_type=F32)
             + jnp.dot(pn, vn_ref[0, :, sl], preferred_element_type=F32))
        o_ref[0, :, sl] = _head_norm(o, sg, lam0).astype(BF16)

    s_next = scores(0)
    p_prev = None
    for h in range(heads):
        s = s_next
        if h + 1 < heads:
            s_next = scores(h + 1)
        p = weights(*s)
        if p_prev is not None:
            output(h - 1, *p_prev)
        p_prev = p
        yield
    output(heads - 1, *p_prev)


def _attn_a_sample(lp, sg, q, kn, vn, ck, cv, lam0):
    b, nt, d = q.shape
    new = pl.BlockSpec((1, nt, d), lambda bi: (bi, 0, 0))
    old = pl.BlockSpec((1,) + ck.shape[1:], lambda bi: (bi, 0, 0))
    return _Call(functools.partial(_attn_a_sample_kernel, lam0=lam0), (b,),
                 [_resident(lp.shape), _resident(sg.shape), new, new, new, old, old],
                 [lp, sg, q, kn, vn, ck, cv], [new], [jax.ShapeDtypeStruct((b, nt, d), BF16)],
                 "attn_a_sample")


def _post_kernel(*refs, final):
    if final:
        x_ref, o_ref, wo_ref, g_ref, w1_ref, w2_ref, gf_ref, out_ref = refs
    else:
        x_ref, o_ref, wo_ref, g_ref, w1_ref, w2_ref, out_ref = refs
    h = x_ref[...] + jnp.dot(o_ref[...], wo_ref[...], preferred_element_type=F32)
    hn = _rms(h, g_ref[...]).astype(BF16)
    yield
    d_ff = w1_ref.shape[1]
    mlp = None
    for c in range(d_ff // FF_CHUNK):
        sl = slice(c * FF_CHUNK, (c + 1) * FF_CHUNK)
        a = jnp.dot(hn, w1_ref[:, sl], preferred_element_type=F32)
        u = jnp.square(jnp.maximum(a, 0.0)).astype(BF16)
        yield
        part = jnp.dot(u, w2_ref[sl, :], preferred_element_type=F32)
        mlp = part if mlp is None else mlp + part
        yield
    h = h + mlp
    if final:
        h = _rms(h, gf_ref[...])
    out_ref[...] = h


def _post(x, o, wo, g, w1, w2, layer, gf=None):
    t, d = x.shape
    tm = TOKEN_TILE
    final = gf is not None
    row = pl.BlockSpec((tm, d), lambda i: (i, 0))
    in_specs = [row, row, _resident(wo.shape), _resident(g.shape),
                _resident_layer(w1.shape, layer), _resident_layer(w2.shape, layer)]
    args = [x, o, wo, g, w1, w2]
    if final:
        in_specs.append(_resident(gf.shape))
        args.append(gf)
    return _Call(functools.partial(_post_kernel, final=final), (t // tm,), in_specs, args,
                 [row], [jax.ShapeDtypeStruct((t, d), F32)], "post_final" if final else "post")


def _proj_b_kernel(x_ref, gkv_ref, wkv_ref, gq_ref, wq_ref,
                   q_ref, kb_ref, vb_ref, k_ref, v_ref, *, period, v_transposed):
    d = x_ref.shape[1]
    x = x_ref[...]
    xs = x * lax.rsqrt(jnp.mean(x * x, axis=-1, keepdims=True) + EPS)
    xkv = (xs * gkv_ref[...]).astype(BF16)
    xq = (xs * gq_ref[...]).astype(BF16)
    q_ref[...] = (jnp.dot(xq, wq_ref[...], preferred_element_type=F32) * QSCALE).astype(BF16)
    k = jnp.dot(xkv, wkv_ref[:, 0:d], preferred_element_type=F32)
    v = jnp.dot(xkv, wkv_ref[:, d:2 * d], preferred_element_type=F32)
    kb_ref[...] = k.astype(BF16)
    vb_ref[...] = _bf16_values(v, v_transposed)

    @pl.when(pl.program_id(0) % period == period - 1)
    def _():
        k_ref[...] = k
        v_ref[...] = v


def _proj_b(x, gkv, wkv, gq, wq, period, v_transposed):
    t, d = x.shape
    tm = TOKEN_TILE
    row = pl.BlockSpec((tm, d), lambda i: (i, 0))
    keep = pl.BlockSpec((tm, d), lambda i: (i // period, 0))
    vb_spec, vb_shape = _values_out(t, d, tm, v_transposed)
    return pl.pallas_call(
        functools.partial(_proj_b_kernel, period=period, v_transposed=v_transposed),
        grid=(t // tm,),
        in_specs=[row, _resident(gkv.shape), _resident(wkv.shape), _resident(gq.shape), _resident(wq.shape)],
        out_specs=[row, row, vb_spec, keep, keep],
        out_shape=[jax.ShapeDtypeStruct((t, d), BF16), jax.ShapeDtypeStruct((t, d), BF16),
                   vb_shape,
                   jax.ShapeDtypeStruct((t // period, d), F32), jax.ShapeDtypeStruct((t // period, d), F32)],
        compiler_params=_params("proj_b", "arbitrary"),
        name="proj_b",
    )(x, gkv, wkv, gq, wq)


def _toeplitz_bias(r_row, rows):
    rb = jnp.broadcast_to(r_row * LOG2E, (rows, r_row.shape[1]))
    return pltpu.roll(rb, 0, 1, stride=1, stride_axis=0)


def _band_prompt_kernel(r_ref, q_ref, k_ref, v_ref, o_ref, vt_ref, bias):
    tq = BAND_TILE
    s_len = q_ref.shape[1]
    n_pairs = q_ref.shape[2] // LANES
    win = PAD + tq

    @pl.when(pl.program_id(1) == 0)
    def _():
        key = lax.broadcasted_iota(jnp.int32, (win, tq), 0)
        qry = lax.broadcasted_iota(jnp.int32, (win, tq), 1)
        dist = qry // CHUNK + LEFT_CHUNKS - key // CHUNK
        in_band = (dist >= 0) & (dist <= LEFT_CHUNKS)
        for pp in range(n_pairs):
            for hh in range(2):
                rb = jnp.broadcast_to(r_ref[pp, hh:hh + 1, :] * LOG2E, (win, win + tq))
                shifted = pltpu.roll(rb, 0, 1, stride=1, stride_axis=0)
                bias[pp, :, hh * tq:(hh + 1) * tq] = jnp.where(in_band, shifted[:, win:win + tq], NEG_INF)

    lane = lax.broadcasted_iota(jnp.int32, (tq, LANES), 1)

    def window(qi):
        q0 = qi * tq
        return max(0, PAD - q0), max(0, q0 - PAD), q0 + tq

    def sweep(pp):
        ps = slice(pp * LANES, (pp + 1) * LANES)
        vt_ref[pp, 0:LANES, :] = v_ref[ps, :]
        vt_ref[pp, LANES:, :] = jnp.ones((vt_ref.shape[1] - LANES, s_len), BF16)

        def scores(qi):
            lo, k_lo, k_hi = window(qi)
            q = q_ref[0, qi * tq:(qi + 1) * tq, ps]
            zero = jnp.zeros_like(q)
            qz = jnp.concatenate([jnp.where(lane < HEAD_DIM, q, zero),
                                  jnp.where(lane >= HEAD_DIM, q, zero)], axis=0)
            return lax.dot_general(k_ref[0, k_lo:k_hi, ps], qz, NT_DIMS,
                                   preferred_element_type=F32) + bias[pp, lo:win, :]

        n_tiles = s_len // tq
        s_next = scores(0)
        yield
        for qi in range(n_tiles):
            s = s_next
            if qi + 1 < n_tiles:
                s_next = scores(qi + 1)
                yield
            _, k_lo, k_hi = window(qi)
            m = jnp.max(s, axis=0, keepdims=True)
            e = jnp.exp2(s - m)
            o = jnp.dot(vt_ref[pp, :, k_lo:k_hi], e.astype(BF16), preferred_element_type=F32)
            o = o[0:LANES] / o[LANES:LANES + 1]
            o = jnp.concatenate([o[0:HEAD_DIM, 0:tq], o[HEAD_DIM:LANES, tq:2 * tq]], axis=0)
            o_ref[0, qi * tq:(qi + 1) * tq, ps] = o.T.astype(BF16)
            yield

    _alternate([sweep(pp) for pp in range(n_pairs)])


def _band_prompt(r, q, k, v):
    b, s, d = q.shape
    width = BAND_PAIRS_PER_STEP * LANES
    blk = pl.BlockSpec((1, s, width), lambda pi, bi: (bi, 0, pi))
    vt_blk = pl.BlockSpec((width, s), lambda pi, bi: (pi, bi))
    rblk = pl.BlockSpec((BAND_PAIRS_PER_STEP, 2, r.shape[2]), lambda pi, bi: (pi, 0, 0))
    win = PAD + BAND_TILE
    return pl.pallas_call(
        _band_prompt_kernel,
        grid=(d // width, b),
        in_specs=[rblk, blk, blk, vt_blk],
        out_specs=blk,
        out_shape=jax.ShapeDtypeStruct((b, s, d), BF16),
        scratch_shapes=[pltpu.VMEM((BAND_PAIRS_PER_STEP, LANES + ONES_ROWS, s), BF16),
                        pltpu.VMEM((BAND_PAIRS_PER_STEP, win, 2 * BAND_TILE), F32)],
        compiler_params=_params("band_prompt", "arbitrary", "arbitrary"),
        name="band_prompt",
    )(r, q, k, v)


def _band_sample_kernel(r_ref, q_ref, kn_ref, vn_ref, ck_ref, cv_ref, o_ref):
    nt, d = q_ref.shape[1], q_ref.shape[2]
    lb = ck_ref.shape[1]
    off = SAMPLE_BIAS_OFFSET
    lane = lax.broadcasted_iota(jnp.int32, (nt, LANES), 1)
    pairs = d // LANES

    def scores(p):
        sl = slice(p * LANES, (p + 1) * LANES)
        q = q_ref[0, :, sl]
        zero = jnp.zeros_like(q)
        qz = jnp.concatenate([jnp.where(lane < HEAD_DIM, q, zero),
                              jnp.where(lane >= HEAD_DIM, q, zero)], axis=0)
        shifted = jnp.concatenate([_toeplitz_bias(r_ref[p, hh:hh + 1, :], nt)
                                   for hh in range(2)], axis=0)
        sc = (lax.dot_general(qz, ck_ref[0, :, sl].astype(BF16), NT_DIMS, preferred_element_type=F32)
              + shifted[:, off:off + lb])
        sn = (lax.dot_general(qz, kn_ref[0, :, sl], NT_DIMS, preferred_element_type=F32)
              + shifted[:, off + lb:off + lb + nt])
        return sc, sn

    def weights(sc, sn):
        m = jnp.maximum(jnp.max(sc, axis=-1, keepdims=True), jnp.max(sn, axis=-1, keepdims=True))
        ec = jnp.exp2(sc - m)
        en = jnp.exp2(sn - m)
        inv = 1.0 / (jnp.sum(ec, axis=-1, keepdims=True) + jnp.sum(en, axis=-1, keepdims=True))
        return (ec * inv).astype(BF16), (en * inv).astype(BF16)

    def output(p, pc, pn):
        sl = slice(p * LANES, (p + 1) * LANES)
        o = (jnp.dot(pc, cv_ref[0, :, sl].astype(BF16), preferred_element_type=F32)
             + jnp.dot(pn, vn_ref[0, :, sl], preferred_element_type=F32))
        o_ref[0, :, sl] = jnp.where(lane < HEAD_DIM, o[0:nt], o[nt:2 * nt]).astype(BF16)

    s_next = scores(0)
    w_prev = None
    for p in range(pairs):
        s = s_next
        if p + 1 < pairs:
            s_next = scores(p + 1)
        w = weights(*s)
        if w_prev is not None:
            output(p - 1, *w_prev)
        w_prev = w
        yield
    output(pairs - 1, *w_prev)


def _band_sample(r, q, kn, vn, ck, cv):
    b, nt, d = q.shape
    lb = ck.shape[1]
    new = pl.BlockSpec((1, nt, d), lambda bi: (bi, 0, 0))
    old = pl.BlockSpec((1, lb, d), lambda bi: (bi, 0, 0))
    return _Call(_band_sample_kernel, (b,), [_resident(r.shape), new, new, new, old, old],
                 [r, q, kn, vn, ck, cv], [new], [jax.ShapeDtypeStruct((b, nt, d), BF16)], "band_sample")


def _rope_tables(pos):
    half = HEAD_DIM // 2
    inv = 1.0 / (ROPE_THETA ** (jnp.arange(half, dtype=F32) / half))
    ang = pos.astype(F32)[:, None] * inv[None, :]
    cos, sin = jnp.cos(ang), jnp.sin(ang)
    reps = LANES // HEAD_DIM
    return (jnp.tile(jnp.concatenate([cos, cos], axis=-1), (1, reps)),
            jnp.tile(jnp.concatenate([-sin, sin], axis=-1), (1, reps)))


def _bias_rows(table):
    far = PAD + SAMPLE_BIAS_OFFSET - REL_CLIP
    near = table[:, 2 * REL_CLIP:0:-1]
    rows = jnp.concatenate([jnp.broadcast_to(table[:, 2 * REL_CLIP:], (table.shape[0], far)), near], axis=1)
    return rows.reshape(table.shape[0] // 2, 2, rows.shape[1])


def _bias_rows_t(table):
    heads = table.shape[0]
    lo = BAND_TILE - REL_CLIP
    hi = PAD + 2 * BAND_TILE - lo - (2 * REL_CLIP + 1)
    rows = jnp.concatenate([jnp.broadcast_to(table[:, :1], (heads, lo)), table,
                            jnp.broadcast_to(table[:, 2 * REL_CLIP:], (heads, hi))], axis=1)
    return rows.reshape(heads // 2, 2, rows.shape[1])


def _lambda_init(layer):
    return 0.8 - 0.6 * math.exp(-0.3 * layer)


def kernel(x_prompt, x_sample, cache_a_k, cache_a_v, cache_b_k, cache_b_v, g_attn, w_a_qkv, a_lambda, a_subln, w_a_o, g_kv, w_kv, w_b_q, b_rel, w_b_o, g_mlp, w_ff1, w_ff2, g_final):
    nb, seq, d = x_prompt.shape
    db, nt, _ = x_sample.shape
    past = cache_a_k.shape[2]
    lb = cache_b_k.shape[1]
    a_heads = d // (2 * HEAD_DIM)
    b_heads = d // HEAD_DIM
    assert w_a_qkv.shape[0] == 1 and w_b_q.shape[0] == 1, "one differential and one band layer"
    assert (db * nt) % TOKEN_TILE == 0 and TOKEN_TILE % nt == 0 and seq % TOKEN_TILE == 0
    keep = min(PAD, seq)
    assert keep == TOKEN_TILE and lb == PAD

    lam0 = _lambda_init(0)
    w_qkv = w_a_qkv[0].astype(BF16)
    w_ao = w_a_o[0].astype(BF16)
    w_kvb = w_kv.astype(BF16)
    w_bq = w_b_q[0].astype(BF16)
    w_bo = w_b_o[0].astype(BF16)
    w1 = w_ff1.astype(BF16)
    w2 = w_ff2.astype(BF16)
    g_a = g_attn[0][None]
    g_b = g_attn[1][None]
    g_k = g_kv[None]
    g_m0 = g_mlp[0][None]
    g_m1 = g_mlp[1][None]
    g_f = g_final[None]
    lp = a_lambda[0]
    sg = a_subln[0][None]
    r = _bias_rows(b_rel[0])
    r_t = _bias_rows_t(b_rel[0])

    cos_p, sin_p = _rope_tables(jnp.arange(seq))
    cos_s, sin_s = _rope_tables(jnp.arange(past, past + nt))
    reps = TOKEN_TILE // nt
    cos_s, sin_s = jnp.tile(cos_s, (reps, 1)), jnp.tile(sin_s, (reps, 1))

    ck_a = cache_a_k.reshape(db, past * a_heads, 2 * HEAD_DIM)
    cv_a = cache_a_v.reshape(db, past * a_heads, 2 * HEAD_DIM)
    ck_b = cache_b_k.reshape(db, lb, d)
    cv_b = cache_b_v.reshape(db, lb, d)
    xp = x_prompt.reshape(nb * seq, d)
    xs = x_sample.reshape(db * nt, d)
    psh, ssh = (nb, seq, d), (db, nt, d)

    q_p, ak_p, kb_p, av_p, vt_p = _proj_a(xp, g_a, w_qkv, cos_p, sin_p, True)
    q_s, ak_s, kb_s, av_s, vb_s = _proj_a(xs, g_a, w_qkv, cos_s, sin_s, False)
    o_p = _attn_a_prompt(lp, sg, q_p.reshape(psh), kb_p.reshape(psh), vt_p, lam0).reshape(nb * seq, d)
    (h_p,), (o_s,) = _launch_pair(
        _post(xp, o_p, w_ao, g_m0, w1, w2, 0),
        _attn_a_sample(lp, sg, q_s.reshape(ssh), kb_s.reshape(ssh), vb_s.reshape(ssh), ck_a, cv_a, lam0))
    (h_s,) = _launch(_post(xs, o_s.reshape(db * nt, d), w_ao, g_m0, w1, w2, 0))

    qb_p, kbb_p, vtb_p, bk_p, bv_p = _proj_b(h_p, g_k, w_kvb, g_b, w_bq, seq // TOKEN_TILE, True)
    qb_s, kbb_s, vbb_s, bk_s, bv_s = _proj_b(h_s, g_k, w_kvb, g_b, w_bq, 1, False)
    ob_p = _band_prompt(r_t, qb_p.reshape(psh), kbb_p.reshape(psh), vtb_p).reshape(nb * seq, d)
    (y_p,), (ob_s,) = _launch_pair(
        _post(h_p, ob_p, w_bo, g_m1, w1, w2, 1, g_f),
        _band_sample(r, qb_s.reshape(ssh), kbb_s.reshape(ssh), vbb_s.reshape(ssh), ck_b, cv_b))
    (y_s,) = _launch(_post(h_s, ob_s.reshape(db * nt, d), w_bo, g_m1, w1, w2, 1, g_f))

    return (y_p.reshape(nb, seq, d), y_s.reshape(db, nt, d),
            ak_p.reshape(1, nb, seq, a_heads, 2 * HEAD_DIM), av_p.reshape(1, nb, seq, a_heads, 2 * HEAD_DIM),
            bk_p.reshape(nb, keep, b_heads, HEAD_DIM), bv_p.reshape(nb, keep, b_heads, HEAD_DIM),
            ak_s.reshape(1, db, nt, a_heads, 2 * HEAD_DIM), av_s.reshape(1, db, nt, a_heads, 2 * HEAD_DIM),
            bk_s.reshape(db, nt, b_heads, HEAD_DIM), bv_s.reshape(db, nt, b_heads, HEAD_DIM))
```

```python
import functools
import math
from typing import Callable, NamedTuple

import jax
import jax.numpy as jnp
from jax import lax
from jax.experimental import pallas as pl
from jax.experimental.pallas import tpu as pltpu

HEAD_DIM = 64
CHUNK = 64
LEFT_CHUNKS = 8
PAD = LEFT_CHUNKS * CHUNK
REL_CLIP = 128
ROPE_THETA = 10000.0
EPS = 1e-6
NEG_INF = -1e30
LANES = 128
LOG2E = math.log2(math.e)
QSCALE = HEAD_DIM ** -0.5 * LOG2E

TOKEN_TILE = 512
FF_CHUNK = 1024
ATTN_TILE = 512
ATTN_HEADS_PER_STEP = 2
BAND_PAIRS_PER_STEP = 4
ONES_ROWS = 16
BAND_TILE = 128
SAMPLE_BIAS_OFFSET = 128
VMEM_MIB = {"proj_a": 48, "proj_b": 36, "attn_a_prompt": 36, "band_prompt": 40, "single": 44, "paired": 56}

F32 = jnp.float32
BF16 = jnp.bfloat16
NT_DIMS = (((1,), (1,)), ((), ()))


def _rms(x, g):
    ms = jnp.mean(x * x, axis=-1, keepdims=True)
    return x * lax.rsqrt(ms + EPS) * g


def _resident(shape):
    return pl.BlockSpec(shape, lambda *_: (0,) * len(shape), pipeline_mode=pl.Buffered(1))


def _resident_layer(stacked_shape, layer):
    rest = tuple(stacked_shape[1:])
    return pl.BlockSpec((None,) + rest, lambda *_: (layer,) + (0,) * len(rest), pipeline_mode=pl.Buffered(1))


def _params(launch, *sem):
    return pltpu.CompilerParams(dimension_semantics=sem, vmem_limit_bytes=VMEM_MIB[launch] * 1024 * 1024)


class _Call(NamedTuple):
    body: Callable
    grid: tuple
    in_specs: list
    args: list
    out_specs: list
    out_shape: list
    name: str


def _run(stages):
    for _ in stages:
        pass


def _alternate(bodies):
    live = list(bodies)
    while live:
        for g in list(live):
            if next(g, StopIteration) is StopIteration:
                live.remove(g)


def _launch(call):
    outs = pl.pallas_call(lambda *refs: _run(call.body(*refs)), grid=call.grid, in_specs=call.in_specs,
                          out_specs=call.out_specs, out_shape=call.out_shape,
                          compiler_params=_params("single", *["parallel"] * len(call.grid)),
                          name=call.name)(*call.args)
    return outs


def _launch_pair(heavy, light):
    if heavy.grid != light.grid:
        return _launch(heavy), _launch(light)
    n_in_h, n_in_l, n_out_h = len(heavy.args), len(light.args), len(heavy.out_shape)

    def body(*refs):
        ins, outs = refs[:n_in_h + n_in_l], refs[n_in_h + n_in_l:]
        _alternate([light.body(*ins[n_in_h:], *outs[n_out_h:]), heavy.body(*ins[:n_in_h], *outs[:n_out_h])])

    outs = pl.pallas_call(body, grid=heavy.grid, in_specs=heavy.in_specs + light.in_specs,
                          out_specs=heavy.out_specs + light.out_specs,
                          out_shape=heavy.out_shape + light.out_shape,
                          compiler_params=_params("paired", *["parallel"] * len(heavy.grid)),
                          name=heavy.name + "_" + light.name)(*heavy.args, *light.args)
    return outs[:n_out_h], outs[n_out_h:]


def _diff_lambda(lp, lam0):
    a = jnp.sum(lp[0:1] * lp[1:2], axis=-1, keepdims=True)
    b = jnp.sum(lp[2:3] * lp[3:4], axis=-1, keepdims=True)
    return jnp.exp(a) - jnp.exp(b) + lam0


def _head_norm(o, g, lam0):
    return _rms(o, g) * (1.0 - lam0)


def _bf16_values(v, transposed):
    return (v.T if transposed else v).astype(BF16)


def _proj_a_kernel(x_ref, g_ref, w_ref, cos_ref, sin_ref,
                   q_ref, k_ref, kb_ref, v_ref, vb_ref, *, v_transposed):
    d = x_ref.shape[1]
    xn = _rms(x_ref[...], g_ref[...]).astype(BF16)
    cos = cos_ref[...]
    sin = sin_ref[...]
    lane = lax.broadcasted_iota(jnp.int32, cos.shape, 1)
    first_half = (lane % HEAD_DIM) < (HEAD_DIM // 2)

    def rope(a):
        partner = jnp.where(first_half, pltpu.roll(a, LANES - HEAD_DIM // 2, 1),
                            pltpu.roll(a, HEAD_DIM // 2, 1))
        return a * cos + partner * sin

    q = jnp.dot(xn, w_ref[:, 0:d].astype(BF16), preferred_element_type=F32)
    for c in range(d // LANES):
        sl = slice(c * LANES, (c + 1) * LANES)
        q_ref[:, sl] = (rope(q[:, sl]) * QSCALE).astype(BF16)
    tm, heads = x_ref.shape[0], d // LANES
    k = jnp.dot(xn, w_ref[:, d:2 * d].astype(BF16), preferred_element_type=F32)
    for c in range(heads):
        sl = slice(c * LANES, (c + 1) * LANES)
        kr = rope(k[:, sl])
        k_ref[pl.ds(c, tm, stride=heads), :] = kr
        kb_ref[:, sl] = kr.astype(BF16)
    v = jnp.dot(xn, w_ref[:, 2 * d:3 * d].astype(BF16), preferred_element_type=F32)
    for c in range(heads):
        v_ref[pl.ds(c, tm, stride=heads), :] = v[:, c * LANES:(c + 1) * LANES]
    vb_ref[...] = _bf16_values(v, v_transposed)


def _values_out(t, d, tm, transposed):
    if transposed:
        return pl.BlockSpec((d, tm), lambda i: (0, i)), jax.ShapeDtypeStruct((d, t), BF16)
    return pl.BlockSpec((tm, d), lambda i: (i, 0)), jax.ShapeDtypeStruct((t, d), BF16)


def _proj_a(x, g, w, cos, sin, v_transposed):
    t, d = x.shape
    tm = TOKEN_TILE
    period = cos.shape[0] // tm
    heads = d // LANES
    row = pl.BlockSpec((tm, d), lambda i: (i, 0))
    native = pl.BlockSpec((tm * heads, LANES), lambda i: (i, 0))
    tab = pl.BlockSpec((tm, LANES), lambda i: (i % period, 0))
    vb_spec, vb_shape = _values_out(t, d, tm, v_transposed)
    return pl.pallas_call(
        functools.partial(_proj_a_kernel, v_transposed=v_transposed),
        grid=(t // tm,),
        in_specs=[row, _resident((1, d)), _resident((d, 3 * d)), tab, tab],
        out_specs=[row, native, row, native, vb_spec],
        out_shape=[jax.ShapeDtypeStruct((t, d), BF16), jax.ShapeDtypeStruct((t * heads, LANES), F32),
                   jax.ShapeDtypeStruct((t, d), BF16), jax.ShapeDtypeStruct((t * heads, LANES), F32),
                   vb_shape],
        compiler_params=_params("proj_a", "parallel"),
        name="proj_a",
    )(x, g, w, cos, sin)


def _attn_a_prompt_kernel(lp_ref, sg_ref, q_ref, k_ref, v_ref, o_ref, vt_ref, *, lam0):
    t = ATTN_TILE
    s_len = q_ref.shape[1]
    lam = _diff_lambda(lp_ref[...], lam0)
    sg = sg_ref[...]

    th = t // 2
    lane = lax.broadcasted_iota(jnp.int32, (th, LANES), 1)
    key_chunk = lax.broadcasted_iota(jnp.int32, (th, t), 0) // CHUNK
    qry_chunk = (lax.broadcasted_iota(jnp.int32, (th, t), 1) % th) // CHUNK
    diag_half = qry_chunk >= key_chunk

    def sweep(hh):
        hs = slice(hh * LANES, (hh + 1) * LANES)
        vt_ref[hh, 0:LANES, :] = v_ref[hs, :]
        vt_ref[hh, LANES:, :] = jnp.ones((vt_ref.shape[1] - LANES, s_len), BF16)

        def query_rows(qi, halves):
            rows = []
            for u in halves:
                q = q_ref[0, qi * t + u * th:qi * t + (u + 1) * th, hs]
                zero = jnp.zeros_like(q)
                rows += [jnp.where(lane < HEAD_DIM, q, zero), jnp.where(lane >= HEAD_DIM, q, zero)]
            return jnp.concatenate(rows, axis=0)

        def scores(qi, j):
            if j < qi:
                return lax.dot_general(k_ref[0, j * t:(j + 1) * t, hs], query_rows(qi, (0, 1)), NT_DIMS,
                                       preferred_element_type=F32)
            top = lax.dot_general(k_ref[0, qi * t:qi * t + th, hs], query_rows(qi, (0, 1)), NT_DIMS,
                                  preferred_element_type=F32)
            bot = lax.dot_general(k_ref[0, qi * t + th:(qi + 1) * t, hs], query_rows(qi, (1,)), NT_DIMS,
                                  preferred_element_type=F32)
            first = jnp.where(diag_half, top[:, 0:t], NEG_INF)
            second = jnp.concatenate([top[:, t:2 * t], jnp.where(diag_half, bot, NEG_INF)], axis=0)
            return first, second

        def update(s, k_lo, k_hi, m, acc):
            m_new = jnp.maximum(m, jnp.max(s, axis=0, keepdims=True))
            p = jnp.exp2(s - m_new)
            acc = jnp.exp2(m - m_new) * acc + jnp.dot(vt_ref[hh, :, k_lo:k_hi], p.astype(BF16),
                                                      preferred_element_type=F32)
            return m_new, acc

        pairs = [(qi, j) for qi in range(s_len // t) for j in range(qi + 1)]
        s_next = scores(*pairs[0])
        yield
        for n, (qi, j) in enumerate(pairs):
            s = s_next
            if n + 1 < len(pairs):
                s_next = scores(*pairs[n + 1])
                yield
            if j == 0:
                m = jnp.full((1, 2 * t), NEG_INF, F32)
                acc = jnp.zeros((vt_ref.shape[1], 2 * t), F32)
            if j < qi:
                m, acc = update(s, j * t, (j + 1) * t, m, acc)
            else:
                _, acc0 = update(s[0], qi * t, qi * t + th, m[:, 0:t], acc[:, 0:t])
                _, acc1 = update(s[1], qi * t, (qi + 1) * t, m[:, t:2 * t], acc[:, t:2 * t])
                halves = []
                for a in (acc0, acc1):
                    a = a[0:LANES] / a[LANES:LANES + 1]
                    halves.append(a[:, 0:th] - lam * a[:, th:t])
                o = jnp.concatenate(halves, axis=1)
                o = o * lax.rsqrt(jnp.mean(o * o, axis=0, keepdims=True) + EPS)
                o_ref[0, qi * t:(qi + 1) * t, hs] = (o.T * sg * (1.0 - lam0)).astype(BF16)
            yield

    _alternate([sweep(hh) for hh in range(q_ref.shape[2] // LANES)])


def _attn_a_prompt(lp, sg, q, k, v, lam0):
    b, s, d = q.shape
    width = ATTN_HEADS_PER_STEP * LANES
    blk = pl.BlockSpec((1, s, width), lambda bi, hi: (bi, 0, hi))
    vt_blk = pl.BlockSpec((width, s), lambda bi, hi: (hi, bi))
    return pl.pallas_call(
        functools.partial(_attn_a_prompt_kernel, lam0=lam0),
        grid=(b, d // width),
        in_specs=[_resident(lp.shape), _resident(sg.shape), blk, blk, vt_blk],
        out_specs=blk,
        out_shape=jax.ShapeDtypeStruct((b, s, d), BF16),
        scratch_shapes=[pltpu.VMEM((ATTN_HEADS_PER_STEP, LANES + ONES_ROWS, s), BF16)],
        compiler_params=_params("attn_a_prompt", "parallel", "parallel"),
        name="attn_a_prompt",
    )(lp, sg, q, k, v)


def _attn_a_sample_kernel(lp_ref, sg_ref, q_ref, kn_ref, vn_ref, ck_ref, cv_ref, o_ref, *, lam0):
    nt, d = q_ref.shape[1], q_ref.shape[2]
    heads = d // LANES
    past = ck_ref.shape[1] // heads
    lam = _diff_lambda(lp_ref[...], lam0)
    sg = sg_ref[...]
    lane = lax.broadcasted_iota(jnp.int32, (nt, LANES), 1)

    def scores(h):
        sl = slice(h * LANES, (h + 1) * LANES)
        q = q_ref[0, :, sl]
        zero = jnp.zeros_like(q)
        qz = jnp.concatenate([jnp.where(lane < HEAD_DIM, q, zero),
                              jnp.where(lane >= HEAD_DIM, q, zero)], axis=0)
        kc = ck_ref[0, pl.ds(h, past, stride=heads), :].astype(BF16)
        return (lax.dot_general(qz, kc, NT_DIMS, preferred_element_type=F32),
                lax.dot_general(qz, kn_ref[0, :, sl], NT_DIMS, preferred_element_type=F32))

    def weights(sc, sn):
        m = jnp.maximum(jnp.max(sc, axis=-1, keepdims=True), jnp.max(sn, axis=-1, keepdims=True))
        ec = jnp.exp2(sc - m)
        en = jnp.exp2(sn - m)
        inv = 1.0 / (jnp.sum(ec, axis=-1, keepdims=True) + jnp.sum(en, axis=-1, keepdims=True))
        pc = ec * inv
        pn = en * inv
        return ((pc[0:nt] - lam * pc[nt:2 * nt]).astype(BF16),
                (pn[0:nt] - lam * pn[nt:2 * nt]).astype(BF16))

    def output(h, pc, pn):
        sl = slice(h * LANES, (h + 1) * LANES)
        vc = cv_ref[0, pl.ds(h, past, stride=heads), :].astype(BF16)
        o = (jnp.dot(pc, vc, preferred_element_type=F32)
             + jnp.dot(pn, vn_ref[0, :, sl], preferred_element_type=F32))
        o_ref[0, :, sl] = _head_norm(o, sg, lam0).astype(BF16)

    s_next = scores(0)
    p_prev = None
    for h in range(heads):
        s = s_next
        if h + 1 < heads:
            s_next = scores(h + 1)
        p = weights(*s)
        if p_prev is not None:
            output(h - 1, *p_prev)
        p_prev = p
        yield
    output(heads - 1, *p_prev)


def _attn_a_sample(lp, sg, q, kn, vn, ck, cv, lam0):
    b, nt, d = q.shape
    new = pl.BlockSpec((1, nt, d), lambda bi: (bi, 0, 0))
    old = pl.BlockSpec((1,) + ck.shape[1:], lambda bi: (bi, 0, 0))
    return _Call(functools.partial(_attn_a_sample_kernel, lam0=lam0), (b,),
                 [_resident(lp.shape), _resident(sg.shape), new, new, new, old, old],
                 [lp, sg, q, kn, vn, ck, cv], [new], [jax.ShapeDtypeStruct((b, nt, d), BF16)],
                 "attn_a_sample")


def _post_kernel(*refs, final):
    if final:
        x_ref, o_ref, wo_ref, g_ref, w1_ref, w2_ref, gf_ref, out_ref = refs
    else:
        x_ref, o_ref, wo_ref, g_ref, w1_ref, w2_ref, out_ref = refs
    h = x_ref[...] + jnp.dot(o_ref[...], wo_ref[...], preferred_element_type=F32)
    hn = _rms(h, g_ref[...]).astype(BF16)
    yield
    d_ff = w1_ref.shape[1]
    mlp = None
    for c in range(d_ff // FF_CHUNK):
        sl = slice(c * FF_CHUNK, (c + 1) * FF_CHUNK)
        a = jnp.dot(hn, w1_ref[:, sl], preferred_element_type=F32)
        u = jnp.square(jnp.maximum(a, 0.0)).astype(BF16)
        yield
        part = jnp.dot(u, w2_ref[sl, :], preferred_element_type=F32)
        mlp = part if mlp is None else mlp + part
        yield
    h = h + mlp
    if final:
        h = _rms(h, gf_ref[...])
    out_ref[...] = h


def _post(x, o, wo, g, w1, w2, layer, gf=None):
    t, d = x.shape
    tm = TOKEN_TILE
    final = gf is not None
    row = pl.BlockSpec((tm, d), lambda i: (i, 0))
    in_specs = [row, row, _resident(wo.shape), _resident(g.shape),
                _resident_layer(w1.shape, layer), _resident_layer(w2.shape, layer)]
    args = [x, o, wo, g, w1, w2]
    if final:
        in_specs.append(_resident(gf.shape))
        args.append(gf)
    return _Call(functools.partial(_post_kernel, final=final), (t // tm,), in_specs, args,
                 [row], [jax.ShapeDtypeStruct((t, d), F32)], "post_final" if final else "post")


def _proj_b_kernel(x_ref, gkv_ref, wkv_ref, gq_ref, wq_ref,
                   q_ref, kb_ref, vb_ref, k_ref, v_ref, *, period, v_transposed):
    d = x_ref.shape[1]
    x = x_ref[...]
    xs = x * lax.rsqrt(jnp.mean(x * x, axis=-1, keepdims=True) + EPS)
    xkv = (xs * gkv_ref[...]).astype(BF16)
    xq = (xs * gq_ref[...]).astype(BF16)
    q_ref[...] = (jnp.dot(xq, wq_ref[...], preferred_element_type=F32) * QSCALE).astype(BF16)
    k = jnp.dot(xkv, wkv_ref[:, 0:d], preferred_element_type=F32)
    v = jnp.dot(xkv, wkv_ref[:, d:2 * d], preferred_element_type=F32)
    kb_ref[...] = k.astype(BF16)
    vb_ref[...] = _bf16_values(v, v_transposed)

    @pl.when(pl.program_id(0) % period == period - 1)
    def _():
        k_ref[...] = k
        v_ref[...] = v


def _proj_b(x, gkv, wkv, gq, wq, period, v_transposed):
    t, d = x.shape
    tm = TOKEN_TILE
    row = pl.BlockSpec((tm, d), lambda i: (i, 0))
    keep = pl.BlockSpec((tm, d), lambda i: (i // period, 0))
    vb_spec, vb_shape = _values_out(t, d, tm, v_transposed)
    return pl.pallas_call(
        functools.partial(_proj_b_kernel, period=period, v_transposed=v_transposed),
        grid=(t // tm,),
        in_specs=[row, _resident(gkv.shape), _resident(wkv.shape), _resident(gq.shape), _resident(wq.shape)],
        out_specs=[row, row, vb_spec, keep, keep],
        out_shape=[jax.ShapeDtypeStruct((t, d), BF16), jax.ShapeDtypeStruct((t, d), BF16),
                   vb_shape,
                   jax.ShapeDtypeStruct((t // period, d), F32), jax.ShapeDtypeStruct((t // period, d), F32)],
        compiler_params=_params("proj_b", "arbitrary"),
        name="proj_b",
    )(x, gkv, wkv, gq, wq)


def _toeplitz_bias(r_row, rows):
    rb = jnp.broadcast_to(r_row * LOG2E, (rows, r_row.shape[1]))
    return pltpu.roll(rb, 0, 1, stride=1, stride_axis=0)


def _band_prompt_kernel(r_ref, q_ref, k_ref, v_ref, o_ref, vt_ref, bias):
    tq = BAND_TILE
    s_len = q_ref.shape[1]
    n_pairs = q_ref.shape[2] // LANES
    win = PAD + tq

    @pl.when(pl.program_id(1) == 0)
    def _():
        key = lax.broadcasted_iota(jnp.int32, (win, tq), 0)
        qry = lax.broadcasted_iota(jnp.int32, (win, tq), 1)
        dist = qry // CHUNK + LEFT_CHUNKS - key // CHUNK
        in_band = (dist >= 0) & (dist <= LEFT_CHUNKS)
        for pp in range(n_pairs):
            for hh in range(2):
                rb = jnp.broadcast_to(r_ref[pp, hh:hh + 1, :] * LOG2E, (win, win + tq))
                shifted = pltpu.roll(rb, 0, 1, stride=1, stride_axis=0)
                bias[pp, :, hh * tq:(hh + 1) * tq] = jnp.where(in_band, shifted[:, win:win + tq], NEG_INF)

    lane = lax.broadcasted_iota(jnp.int32, (tq, LANES), 1)

    def window(qi):
        q0 = qi * tq
        return max(0, PAD - q0), max(0, q0 - PAD), q0 + tq

    def sweep(pp):
        ps = slice(pp * LANES, (pp + 1) * LANES)
        vt_ref[pp, 0:LANES, :] = v_ref[ps, :]
        vt_ref[pp, LANES:, :] = jnp.ones((vt_ref.shape[1] - LANES, s_len), BF16)

        def scores(qi):
            lo, k_lo, k_hi = window(qi)
            q = q_ref[0, qi * tq:(qi + 1) * tq, ps]
            zero = jnp.zeros_like(q)
            qz = jnp.concatenate([jnp.where(lane < HEAD_DIM, q, zero),
                                  jnp.where(lane >= HEAD_DIM, q, zero)], axis=0)
            return lax.dot_general(k_ref[0, k_lo:k_hi, ps], qz, NT_DIMS,
                                   preferred_element_type=F32) + bias[pp, lo:win, :]

        n_tiles = s_len // tq
        s_next = scores(0)
        yield
        for qi in range(n_tiles):
            s = s_next
            if qi + 1 < n_tiles:
                s_next = scores(qi + 1)
                yield
            _, k_lo, k_hi = window(qi)
            m = jnp.max(s, axis=0, keepdims=True)
            e = jnp.exp2(s - m)
            o = jnp.dot(vt_ref[pp, :, k_lo:k_hi], e.astype(BF16), preferred_element_type=F32)
            o = o[0:LANES] / o[LANES:LANES + 1]
            o = jnp.concatenate([o[0:HEAD_DIM, 0:tq], o[HEAD_DIM:LANES, tq:2 * tq]], axis=0)
            o_ref[0, qi * tq:(qi + 1) * tq, ps] = o.T.astype(BF16)
            yield

    _alternate([sweep(pp) for pp in range(n_pairs)])


def _band_prompt(r, q, k, v):
    b, s, d = q.shape
    width = BAND_PAIRS_PER_STEP * LANES
    blk = pl.BlockSpec((1, s, width), lambda pi, bi: (bi, 0, pi))
    vt_blk = pl.BlockSpec((width, s), lambda pi, bi: (pi, bi))
    rblk = pl.BlockSpec((BAND_PAIRS_PER_STEP, 2, r.shape[2]), lambda pi, bi: (pi, 0, 0))
    win = PAD + BAND_TILE
    return pl.pallas_call(
        _band_prompt_kernel,
        grid=(d // width, b),
        in_specs=[rblk, blk, blk, vt_blk],
        out_specs=blk,
        out_shape=jax.ShapeDtypeStruct((b, s, d), BF16),
        scratch_shapes=[pltpu.VMEM((BAND_PAIRS_PER_STEP, LANES + ONES_ROWS, s), BF16),
                        pltpu.VMEM((BAND_PAIRS_PER_STEP, win, 2 * BAND_TILE), F32)],
        compiler_params=_params("band_prompt", "arbitrary", "arbitrary"),
        name="band_prompt",
    )(r, q, k, v)


def _band_sample_kernel(r_ref, q_ref, kn_ref, vn_ref, ck_ref, cv_ref, o_ref):
    nt, d = q_ref.shape[1], q_ref.shape[2]
    lb = ck_ref.shape[1]
    off = SAMPLE_BIAS_OFFSET
    lane = lax.broadcasted_iota(jnp.int32, (nt, LANES), 1)
    pairs = d // LANES

    def scores(p):
        sl = slice(p * LANES, (p + 1) * LANES)
        q = q_ref[0, :, sl]
        zero = jnp.zeros_like(q)
        qz = jnp.concatenate([jnp.where(lane < HEAD_DIM, q, zero),
                              jnp.where(lane >= HEAD_DIM, q, zero)], axis=0)
        shifted = jnp.concatenate([_toeplitz_bias(r_ref[p, hh:hh + 1, :], nt)
                                   for hh in range(2)], axis=0)
        sc = (lax.dot_general(qz, ck_ref[0, :, sl].astype(BF16), NT_DIMS, preferred_element_type=F32)
              + shifted[:, off:off + lb])
        sn = (lax.dot_general(qz, kn_ref[0, :, sl], NT_DIMS, preferred_element_type=F32)
              + shifted[:, off + lb:off + lb + nt])
        return sc, sn

    def weights(sc, sn):
        m = jnp.maximum(jnp.max(sc, axis=-1, keepdims=True), jnp.max(sn, axis=-1, keepdims=True))
        ec = jnp.exp2(sc - m)
        en = jnp.exp2(sn - m)
        inv = 1.0 / (jnp.sum(ec, axis=-1, keepdims=True) + jnp.sum(en, axis=-1, keepdims=True))
        return (ec * inv).astype(BF16), (en * inv).astype(BF16)

    def output(p, pc, pn):
        sl = slice(p * LANES, (p + 1) * LANES)
        o = (jnp.dot(pc, cv_ref[0, :, sl].astype(BF16), preferred_element_type=F32)
             + jnp.dot(pn, vn_ref[0, :, sl], preferred_element_type=F32))
        o_ref[0, :, sl] = jnp.where(lane < HEAD_DIM, o[0:nt], o[nt:2 * nt]).astype(BF16)

    s_next = scores(0)
    w_prev = None
    for p in range(pairs):
        s = s_next
        if p + 1 < pairs:
            s_next = scores(p + 1)
        w = weights(*s)
        if w_prev is not None:
            output(p - 1, *w_prev)
        w_prev = w
        yield
    output(pairs - 1, *w_prev)


def _band_sample(r, q, kn, vn, ck, cv):
    b, nt, d = q.shape
    lb = ck.shape[1]
    new = pl.BlockSpec((1, nt, d), lambda bi: (bi, 0, 0))
    old = pl.BlockSpec((1, lb, d), lambda bi: (bi, 0, 0))
    return _Call(_band_sample_kernel, (b,), [_resident(r.shape), new, new, new, old, old],
                 [r, q, kn, vn, ck, cv], [new], [jax.ShapeDtypeStruct((b, nt, d), BF16)], "band_sample")


def _rope_tables(pos):
    half = HEAD_DIM // 2
    inv = 1.0 / (ROPE_THETA ** (jnp.arange(half, dtype=F32) / half))
    ang = pos.astype(F32)[:, None] * inv[None, :]
    cos, sin = jnp.cos(ang), jnp.sin(ang)
    reps = LANES // HEAD_DIM
    return (jnp.tile(jnp.concatenate([cos, cos], axis=-1), (1, reps)),
            jnp.tile(jnp.concatenate([-sin, sin], axis=-1), (1, reps)))


def _bias_rows(table):
    far = PAD + SAMPLE_BIAS_OFFSET - REL_CLIP
    near = table[:, 2 * REL_CLIP:0:-1]
    rows = jnp.concatenate([jnp.broadcast_to(table[:, 2 * REL_CLIP:], (table.shape[0], far)), near], axis=1)
    return rows.reshape(table.shape[0] // 2, 2, rows.shape[1])


def _bias_rows_t(table):
    heads = table.shape[0]
    lo = BAND_TILE - REL_CLIP
    hi = PAD + 2 * BAND_TILE - lo - (2 * REL_CLIP + 1)
    rows = jnp.concatenate([jnp.broadcast_to(table[:, :1], (heads, lo)), table,
                            jnp.broadcast_to(table[:, 2 * REL_CLIP:], (heads, hi))], axis=1)
    return rows.reshape(heads // 2, 2, rows.shape[1])


def _lambda_init(layer):
    return 0.8 - 0.6 * math.exp(-0.3 * layer)


def kernel(x_prompt, x_sample, cache_a_k, cache_a_v, cache_b_k, cache_b_v, g_attn, w_a_qkv, a_lambda, a_subln, w_a_o, g_kv, w_kv, w_b_q, b_rel, w_b_o, g_mlp, w_ff1, w_ff2, g_final):
    nb, seq, d = x_prompt.shape
    db, nt, _ = x_sample.shape
    past = cache_a_k.shape[2]
    lb = cache_b_k.shape[1]
    a_heads = d // (2 * HEAD_DIM)
    b_heads = d // HEAD_DIM
    assert w_a_qkv.shape[0] == 1 and w_b_q.shape[0] == 1, "one differential and one band layer"
    assert (db * nt) % TOKEN_TILE == 0 and TOKEN_TILE % nt == 0 and seq % TOKEN_TILE == 0
    keep = min(PAD, seq)
    assert keep == TOKEN_TILE and lb == PAD

    lam0 = _lambda_init(0)
    w_qkv = w_a_qkv[0]
    w_ao = w_a_o[0].astype(BF16)
    w_kvb = w_kv.astype(BF16)
    w_bq = w_b_q[0].astype(BF16)
    w_bo = w_b_o[0].astype(BF16)
    w1 = w_ff1.astype(BF16)
    w2 = w_ff2.astype(BF16)
    g_a = g_attn[0][None]
    g_b = g_attn[1][None]
    g_k = g_kv[None]
    g_m0 = g_mlp[0][None]
    g_m1 = g_mlp[1][None]
    g_f = g_final[None]
    lp = a_lambda[0]
    sg = a_subln[0][None]
    r = _bias_rows(b_rel[0])
    r_t = _bias_rows_t(b_rel[0])

    cos_p, sin_p = _rope_tables(jnp.arange(seq))
    cos_s, sin_s = _rope_tables(jnp.arange(past, past + nt))
    reps = TOKEN_TILE // nt
    cos_s, sin_s = jnp.tile(cos_s, (reps, 1)), jnp.tile(sin_s, (reps, 1))

    ck_a = cache_a_k.reshape(db, past * a_heads, 2 * HEAD_DIM)
    cv_a = cache_a_v.reshape(db, past * a_heads, 2 * HEAD_DIM)
    ck_b = cache_b_k.reshape(db, lb, d)
    cv_b = cache_b_v.reshape(db, lb, d)
    xp = x_prompt.reshape(nb * seq, d)
    xs = x_sample.reshape(db * nt, d)
    psh, ssh = (nb, seq, d), (db, nt, d)

    q_p, ak_p, kb_p, av_p, vt_p = _proj_a(xp, g_a, w_qkv, cos_p, sin_p, True)
    q_s, ak_s, kb_s, av_s, vb_s = _proj_a(xs, g_a, w_qkv, cos_s, sin_s, False)
    o_p = _attn_a_prompt(lp, sg, q_p.reshape(psh), kb_p.reshape(psh), vt_p, lam0).reshape(nb * seq, d)
    (h_p,), (o_s,) = _launch_pair(
        _post(xp, o_p, w_ao, g_m0, w1, w2, 0),
        _attn_a_sample(lp, sg, q_s.reshape(ssh), kb_s.reshape(ssh), vb_s.reshape(ssh), ck_a, cv_a, lam0))
    (h_s,) = _launch(_post(xs, o_s.reshape(db * nt, d), w_ao, g_m0, w1, w2, 0))

    qb_p, kbb_p, vtb_p, bk_p, bv_p = _proj_b(h_p, g_k, w_kvb, g_b, w_bq, seq // TOKEN_TILE, True)
    qb_s, kbb_s, vbb_s, bk_s, bv_s = _proj_b(h_s, g_k, w_kvb, g_b, w_bq, 1, False)
    ob_p = _band_prompt(r_t, qb_p.reshape(psh), kbb_p.reshape(psh), vtb_p).reshape(nb * seq, d)
    (y_p,), (ob_s,) = _launch_pair(
        _post(h_p, ob_p, w_bo, g_m1, w1, w2, 1, g_f),
        _band_sample(r, qb_s.reshape(ssh), kbb_s.reshape(ssh), vbb_s.reshape(ssh), ck_b, cv_b))
    (y_s,) = _launch(_post(h_s, ob_s.reshape(db * nt, d), w_bo, g_m1, w1, w2, 1, g_f))

    return (y_p.reshape(nb, seq, d), y_s.reshape(db, nt, d),
            ak_p.reshape(1, nb, seq, a_heads, 2 * HEAD_DIM), av_p.reshape(1, nb, seq, a_heads, 2 * HEAD_DIM),
            bk_p.reshape(nb, keep, b_heads, HEAD_DIM), bv_p.reshape(nb, keep, b_heads, HEAD_DIM),
            ak_s.reshape(1, db, nt, a_heads, 2 * HEAD_DIM), av_s.reshape(1, db, nt, a_heads, 2 * HEAD_DIM),
            bk_s.reshape(db, nt, b_heads, HEAD_DIM), bv_s.reshape(db, nt, b_heads, HEAD_DIM))
```
